```python
import math
import jax
import jax.numpy as jnp
from jax import lax
import numpy as np

D_MODEL = 1024
BATCH = 2
SEQ = 8192
DEPTH = 1
DEC_BATCH = 128
DEC_SEQ = 4
PAST_LEN = 2048
PAGE_SIZE = 128

M_HEADS = 4
M_HEAD_DIM = 128
M_WIDTH = M_HEADS * M_HEAD_DIM
M_CHUNK = 128
A_HEADS = 8
A_HEAD_DIM = 64
A_WIDTH = A_HEADS * A_HEAD_DIM
MOBA_BLOCK = 256
MOBA_TOPK = 3
Q_ROWS = 256
N_EXPERTS = 256
TOP_K = 8
D_EXPERT = 256
D_SHARED = 256
ROUTE_SCALE = 2.5
EXPERT_ROWS = 128
EPS = 1e-6
NEG = -1e30
IN_WIDTHS = (M_WIDTH, M_WIDTH, M_WIDTH, M_WIDTH, M_HEADS, M_HEADS, A_WIDTH, A_WIDTH, A_WIDTH, D_MODEL, D_MODEL)
D_IN = sum(IN_WIDTHS)

kernel_name = 'hybrid_mlstm_moba_moe_step'


def rmsnorm(x, g):
    xf = x.astype(jnp.float32)
    xf = xf * lax.rsqrt(jnp.mean(xf * xf, axis=-1, keepdims=True) + EPS)
    return (xf * g.astype(jnp.float32)).astype(x.dtype)


def alibi_slopes():
    return 2.0 ** (-8.0 * jnp.arange(1, A_HEADS + 1, dtype=jnp.float32) / A_HEADS)


def mlstm_scan(q, k, v, log_i, log_f, C0, n0, m0):
    B, T, H, Dh = q.shape
    L = min(M_CHUNK, T)
    nc = -(-T // L)
    pad = nc * L - T
    if pad:
        pw = ((0, 0), (0, pad), (0, 0), (0, 0))
        q, k, v = jnp.pad(q, pw), jnp.pad(k, pw), jnp.pad(v, pw)
        log_i = jnp.pad(log_i, ((0, 0), (0, pad), (0, 0)), constant_values=NEG)
        log_f = jnp.pad(log_f, ((0, 0), (0, pad), (0, 0)))

    def to_chunks(a):
        a = a.reshape((B, nc, L) + a.shape[2:])
        return jnp.moveaxis(a, (1, 3), (0, 2))

    causal = jnp.tril(jnp.ones((L, L), dtype=bool))

    def step(carry, inp):
        C, n, m = carry
        qc, kc, vc, li, lf = inp
        b = jnp.cumsum(lf, axis=-1)
        inter = m[..., None] + b
        intra = li[..., None, :] + b[..., :, None] - b[..., None, :]
        intra = jnp.where(causal, intra, NEG)
        m_t = jnp.maximum(inter, jnp.max(intra, axis=-1))
        w_inter = jnp.exp(inter - m_t)
        s = jnp.einsum('bhtd,bhsd->bhts', qc, kc) * jnp.exp(intra - m_t[..., None])
        num = w_inter[..., None] * jnp.einsum('bhvk,bhtk->bhtv', C, qc) + jnp.einsum('bhts,bhsv->bhtv', s, vc)
        den = w_inter * jnp.einsum('bhk,bhtk->bht', n, qc) + jnp.sum(s, axis=-1)
        h = num / jnp.maximum(jnp.abs(den), jnp.exp(-m_t))[..., None]
        m_last = m_t[..., -1]
        w_c = jnp.exp(m + b[..., -1] - m_last)
        w_s = jnp.exp(li + b[..., -1:] - b - m_last[..., None])
        C_new = w_c[..., None, None] * C + jnp.einsum('bhs,bhsv,bhsk->bhvk', w_s, vc, kc)
        n_new = w_c[..., None] * n + jnp.einsum('bhs,bhsk->bhk', w_s, kc)
        return (C_new, n_new, m_last), h

    (C, n, m), hs = lax.scan(step, (C0, n0, m0),
                             (to_chunks(q), to_chunks(k), to_chunks(v), to_chunks(log_i), to_chunks(log_f)))
    h = jnp.moveaxis(hs, (0, 2), (1, 3)).reshape(B, nc * L, H, Dh)[:, :T]
    return h, C, n, m


def moba_attention(q, k_all, v_all, pos0, qb):
    B, T, H, Dh = q.shape
    Lk = k_all.shape[1]
    nb = -(-Lk // MOBA_BLOCK)
    padk = nb * MOBA_BLOCK - Lk
    pw = ((0, 0), (0, padk), (0, 0), (0, 0))
    kb = jnp.pad(k_all, pw).reshape(B, nb, MOBA_BLOCK, H, Dh).transpose(0, 3, 1, 2, 4)
    vb = jnp.pad(v_all, pw).reshape(B, nb, MOBA_BLOCK, H, Dh).transpose(0, 3, 1, 2, 4)
    kmean = jnp.mean(kb.astype(jnp.float32), axis=3)
    n_sel = min(MOBA_TOPK, nb)
    slopes = alibi_slopes()
    scale = A_HEAD_DIM ** -0.5
    bi = jnp.arange(B)[:, None, None, None]
    hi = jnp.arange(H)[None, :, None, None]
    offs = jnp.arange(MOBA_BLOCK, dtype=jnp.int32)

    def attend(args):
        qblk, pos = args
        nq = pos.shape[0]
        qh = qblk.transpose(0, 2, 1, 3)
        own = pos // MOBA_BLOCK
        gate = jnp.einsum('bhqd,bhnd->bhqn', qh.astype(jnp.float32), kmean)
        gate = jnp.where(jnp.arange(nb)[None, :] < own[:, None], gate, NEG)
        _, top = lax.top_k(gate, n_sel)
        blocks = jnp.concatenate(
            [top.astype(jnp.int32), jnp.broadcast_to(own[None, None, :, None], (B, H, nq, 1)).astype(jnp.int32)], axis=-1)
        ok = jnp.concatenate([jnp.arange(n_sel)[None, :] < own[:, None], jnp.ones((nq, 1), dtype=bool)], axis=-1)
        kg = kb[bi, hi, blocks]
        vg = vb[bi, hi, blocks]
        s = jnp.einsum('bhqd,bhqrkd->bhqrk', qh, kg).astype(jnp.float32) * scale
        dist = pos[:, None, None] - (blocks[..., None] * MOBA_BLOCK + offs)
        s = s - slopes[:, None, None, None] * dist.astype(jnp.float32)
        s = jnp.where(ok[:, :, None] & (dist >= 0), s, NEG)
        pr = jax.nn.softmax(s.reshape(B, H, nq, -1), axis=-1).reshape(s.shape)
        return jnp.einsum('bhqrk,bhqrkd->bqhd', pr.astype(vg.dtype), vg)

    n_qblk = T // qb
    qs = q.reshape(B, n_qblk, qb, H, Dh).transpose(1, 0, 2, 3, 4)
    ps = (pos0 + jnp.arange(T, dtype=jnp.int32)).reshape(n_qblk, qb)
    out = lax.map(attend, (qs, ps))
    return out.transpose(1, 0, 2, 3, 4).reshape(B, T, H, Dh)


def moe_ffn(h, p):
    N, D = h.shape
    scores = jax.nn.sigmoid((h @ p['w_router']).astype(jnp.float32))
    _, idx = lax.top_k(scores + p['b_router'].astype(jnp.float32), TOP_K)
    w = jnp.take_along_axis(scores, idx, axis=-1)
    w = w / jnp.sum(w, axis=-1, keepdims=True) * ROUTE_SCALE
    shared = (jax.nn.silu(h @ p['ws_gate']) * (h @ p['ws_up'])) @ p['ws_down']
    A = N * TOP_K
    e = idx.reshape(A).astype(jnp.int32)
    tok = jnp.repeat(jnp.arange(N, dtype=jnp.int32), TOP_K)
    wt = w.reshape(A)
    order = jnp.argsort(e, stable=True)
    se = e[order]
    cnt = jnp.bincount(e, length=N_EXPERTS)
    padded = (cnt + EXPERT_ROWS - 1) // EXPERT_ROWS * EXPERT_ROWS
    pend = jnp.cumsum(padded)
    pstart = pend - padded
    first = jnp.cumsum(cnt) - cnt
    dest = pstart[se] + jnp.arange(A, dtype=jnp.int32) - first[se]
    n_blocks = -(-(A + N_EXPERTS * (EXPERT_ROWS - 1)) // EXPERT_ROWS)
    R = n_blocks * EXPERT_ROWS
    row_tok = jnp.full((R,), N, jnp.int32).at[dest].set(tok[order])
    row_w = jnp.zeros((R,), jnp.float32).at[dest].set(wt[order])
    blk_e = jnp.minimum(jnp.searchsorted(pend, jnp.arange(n_blocks) * EXPERT_ROWS, side='right'), N_EXPERTS - 1)
    h_pad = jnp.concatenate([h, jnp.zeros((1, D), h.dtype)], axis=0)

    def expert_block(args):
        toks, wr, ex = args
        xb = h_pad[toks]
        hb = jax.nn.silu(xb @ p['w_gate'][ex]) * (xb @ p['w_up'][ex])
        return (hb @ p['w_down'][ex]) * wr[:, None].astype(h.dtype)

    y = lax.map(expert_block, (row_tok.reshape(n_blocks, EXPERT_ROWS), row_w.reshape(n_blocks, EXPERT_ROWS), blk_e))
    routed = jax.ops.segment_sum(y.reshape(R, D), row_tok, num_segments=N + 1)[:N]
    return shared + routed.astype(shared.dtype)


def decoder_layer(x, c, k_past, v_past, C0, n0, m0, p):
    B, T, D = x.shape
    f32 = jnp.float32
    mod = jax.nn.silu(c) @ p['w_ada'] + p['b_ada']
    sh1, sc1, gt1, sh2, sc2, gt2 = [a[:, None, :] for a in jnp.split(mod, 6, axis=-1)]
    h = rmsnorm(x, p['g_norm1']) * (1 + sc1) + sh1
    z = h @ p['w_in']
    mq, mk, mv, mo, mi, mf, aq, ak, av, ga, gb = jnp.split(z, np.cumsum(IN_WIDTHS)[:-1].tolist(), axis=-1)
    qm = mq.reshape(B, T, M_HEADS, M_HEAD_DIM).astype(f32)
    km = mk.reshape(B, T, M_HEADS, M_HEAD_DIM).astype(f32) * (M_HEAD_DIM ** -0.5)
    vm = mv.reshape(B, T, M_HEADS, M_HEAD_DIM).astype(f32)
    b_i, b_f = jnp.split(p['b_gates'].astype(f32), 2)
    log_i = mi.astype(f32) + b_i
    log_f = jax.nn.log_sigmoid(mf.astype(f32) + b_f)
    hm, C, n, m = mlstm_scan(qm, km, vm, log_i, log_f, C0, n0, m0)
    hm = rmsnorm(hm, p['g_mh']).reshape(B, T, M_WIDTH).astype(x.dtype) * jax.nn.sigmoid(mo)
    qa = rmsnorm(aq.reshape(B, T, A_HEADS, A_HEAD_DIM), p['g_q'])
    ka = rmsnorm(ak.reshape(B, T, A_HEADS, A_HEAD_DIM), p['g_k'])
    va = av.reshape(B, T, A_HEADS, A_HEAD_DIM)
    if k_past is None:
        k_all, v_all, pos0 = ka, va, 0
    else:
        k_all = jnp.concatenate([k_past.astype(ka.dtype), ka], axis=1)
        v_all = jnp.concatenate([v_past.astype(va.dtype), va], axis=1)
        pos0 = k_past.shape[1]
    qb = math.gcd(T, max(1, Q_ROWS // B))
    ha = moba_attention(qa, k_all, v_all, pos0, qb).reshape(B, T, A_WIDTH)
    merged = jax.nn.sigmoid(ga) * (hm @ p['w_br_m']) + jax.nn.sigmoid(gb) * (ha @ p['w_br_a'])
    x1 = x + gt1 * (merged @ p['w_out'])
    h2 = rmsnorm(x1, p['g_norm2']) * (1 + sc2) + sh2
    f = moe_ffn(h2.reshape(B * T, D), p).reshape(B, T, D)
    y = x1 + gt2 * f
    return y, ka, va, C, n, m


def setup_inputs(seed: int = 0) -> dict:
    key = jax.random.key(seed)
    ks = list(jax.random.split(key, 40))
    f32 = jnp.float32
    L = DEPTH

    def nrm(shape, scale):
        return jax.random.normal(ks.pop(), shape, f32) * scale

    n_pages = PAST_LEN // PAGE_SIZE
    n_used = DEC_BATCH * n_pages
    n_pool = n_used + max(1, n_used // 4)
    page_table = jax.random.permutation(ks.pop(), n_pool)[:n_used].reshape(DEC_BATCH, n_pages).astype(jnp.int32)
    b_gates = jnp.concatenate([nrm((L, M_HEADS), 0.1),
                               jax.random.uniform(ks.pop(), (L, M_HEADS), f32, 3.0, 6.0)], axis=-1)
    return {
        'x_prompt': nrm((BATCH, SEQ, D_MODEL), 1.0),
        'x_sample': nrm((DEC_BATCH, DEC_SEQ, D_MODEL), 1.0),
        'cache_k': nrm((L, n_pool, PAGE_SIZE, A_HEADS, A_HEAD_DIM), 1.0),
        'cache_v': nrm((L, n_pool, PAGE_SIZE, A_HEADS, A_HEAD_DIM), 1.0),
        'page_table': page_table,
        'state_mlstm_C': nrm((L, DEC_BATCH, M_HEADS, M_HEAD_DIM, M_HEAD_DIM), 0.05),
        'state_mlstm_n': nrm((L, DEC_BATCH, M_HEADS, M_HEAD_DIM), 0.1),
        'state_mlstm_m': nrm((L, DEC_BATCH, M_HEADS), 1.0),
        'c_prompt': nrm((BATCH, D_MODEL), 1.0),
        'c_sample': nrm((DEC_BATCH, D_MODEL), 1.0),
        'w_ada': nrm((L, D_MODEL, 6 * D_MODEL), 0.5 * D_MODEL ** -0.5),
        'b_ada': nrm((L, 6 * D_MODEL), 0.01),
        'g_norm1': 1.0 + nrm((L, D_MODEL), 0.01),
        'w_in': nrm((L, D_MODEL, D_IN), D_MODEL ** -0.5),
        'b_gates': b_gates,
        'g_q': 1.0 + nrm((L, A_HEAD_DIM), 0.01),
        'g_k': 1.0 + nrm((L, A_HEAD_DIM), 0.01),
        'g_mh': 1.0 + nrm((L, M_HEADS, M_HEAD_DIM), 0.01),
        'w_br_m': nrm((L, M_WIDTH, D_MODEL), M_WIDTH ** -0.5),
        'w_br_a': nrm((L, A_WIDTH, D_MODEL), A_WIDTH ** -0.5),
        'w_out': nrm((L, D_MODEL, D_MODEL), D_MODEL ** -0.5),
        'g_norm2': 1.0 + nrm((L, D_MODEL), 0.01),
        'w_router': nrm((L, D_MODEL, N_EXPERTS), D_MODEL ** -0.5),
        'b_router': nrm((L, N_EXPERTS), 0.01),
        'w_gate': nrm((L, N_EXPERTS, D_MODEL, D_EXPERT), D_MODEL ** -0.5),
        'w_up': nrm((L, N_EXPERTS, D_MODEL, D_EXPERT), D_MODEL ** -0.5),
        'w_down': nrm((L, N_EXPERTS, D_EXPERT, D_MODEL), D_EXPERT ** -0.5),
        'ws_gate': nrm((L, D_MODEL, D_SHARED), D_MODEL ** -0.5),
        'ws_up': nrm((L, D_MODEL, D_SHARED), D_MODEL ** -0.5),
        'ws_down': nrm((L, D_SHARED, D_MODEL), D_SHARED ** -0.5),
    }


def reference(x_prompt, x_sample, cache_k, cache_v, page_table, state_mlstm_C, state_mlstm_n, state_mlstm_m,
              c_prompt, c_sample, w_ada, b_ada, g_norm1, w_in, b_gates, g_q, g_k, g_mh, w_br_m, w_br_a, w_out,
              g_norm2, w_router, b_router, w_gate, w_up, w_down, ws_gate, ws_up, ws_down):
    f32 = jnp.float32
    B = x_prompt.shape[0]
    DB = x_sample.shape[0]
    n_pages = page_table.shape[1]
    page = cache_k.shape[2]
    xp, xs = x_prompt, x_sample
    kp_l, vp_l, Cp_l, np_l, mp_l = [], [], [], [], []
    ks_l, vs_l, Cs_l, ns_l, ms_l = [], [], [], [], []
    for l in range(DEPTH):
        p = dict(w_ada=w_ada[l], b_ada=b_ada[l], g_norm1=g_norm1[l], w_in=w_in[l], b_gates=b_gates[l],
                 g_q=g_q[l], g_k=g_k[l], g_mh=g_mh[l], w_br_m=w_br_m[l], w_br_a=w_br_a[l], w_out=w_out[l],
                 g_norm2=g_norm2[l], w_router=w_router[l], b_router=b_router[l], w_gate=w_gate[l],
                 w_up=w_up[l], w_down=w_down[l], ws_gate=ws_gate[l], ws_up=ws_up[l], ws_down=ws_down[l])
        C0 = jnp.zeros((B, M_HEADS, M_HEAD_DIM, M_HEAD_DIM), f32)
        n0 = jnp.zeros((B, M_HEADS, M_HEAD_DIM), f32)
        m0 = jnp.zeros((B, M_HEADS), f32)
        xp, kp, vp, Cp, n_p, mp = decoder_layer(xp, c_prompt, None, None, C0, n0, m0, p)
        k_past = cache_k[l][page_table].reshape(DB, n_pages * page, A_HEADS, A_HEAD_DIM)
        v_past = cache_v[l][page_table].reshape(DB, n_pages * page, A_HEADS, A_HEAD_DIM)
        xs, ks, vs, Cs, n_s, ms = decoder_layer(xs, c_sample, k_past, v_past, state_mlstm_C[l].astype(f32),
                                                state_mlstm_n[l].astype(f32), state_mlstm_m[l].astype(f32), p)
        kp_l.append(kp); vp_l.append(vp); Cp_l.append(Cp); np_l.append(n_p); mp_l.append(mp)
        ks_l.append(ks); vs_l.append(vs); Cs_l.append(Cs); ns_l.append(n_s); ms_l.append(ms)
    return (xp, xs, jnp.stack(kp_l), jnp.stack(vp_l), jnp.stack(Cp_l), jnp.stack(np_l), jnp.stack(mp_l),
            jnp.stack(ks_l), jnp.stack(vs_l), jnp.stack(Cs_l), jnp.stack(ns_l), jnp.stack(ms_l))
```

```python
import functools

import jax
import jax.numpy as jnp
from jax import lax
from jax.experimental import pallas as pl
from jax.experimental.pallas import tpu as pltpu

F32, BF16, I32 = jnp.float32, jnp.bfloat16, jnp.int32

D = 1024
MH, MD = 4, 128
AH, AD = 8, 64
MW, AW = MH * MD, AH * AD
MCHUNK = 128
BLK = 256
TOPB = 3
NE, TOPK, DE, DSH = 256, 8, 256, 256
ROUTE_SCALE = 2.5
GROUP = 128
EPS = 1e-6
NEG = -1e30
MASKV = -3.0e38
VMEM_LIMIT = 56 * 1024 * 1024
_OFF = {}
_o = 0
for _n, _w in (("mq", MW), ("mk", MW), ("mv", MW), ("mo", MW), ("mi", MH), ("mf", MH),
               ("aq", AW), ("ak", AW), ("av", AW), ("ga", D), ("gb", D)):
    _OFF[_n] = (_o, _o + _w)
    _o += _w


def _cparams(*sem):
    return pltpu.CompilerParams(dimension_semantics=sem, vmem_limit_bytes=VMEM_LIMIT)


def _mm(a, b):
    return jnp.dot(a.astype(BF16), b.astype(BF16), preferred_element_type=F32)


def _mm_nt(a, b):
    return lax.dot_general(a.astype(BF16), b.astype(BF16), (((1,), (1,)), ((), ())),
                           preferred_element_type=F32)


def _mm_tn(a, b):
    return lax.dot_general(a.astype(BF16), b.astype(BF16), (((0,), (0,)), ((), ())),
                           preferred_element_type=F32)


def _split2(x):
    hi = x.astype(BF16)
    return hi, (x - hi.astype(F32)).astype(BF16)


def _mm_x3(a, b):
    ah, al = _split2(a)
    bh, bl = _split2(b)
    d = functools.partial(jnp.dot, preferred_element_type=F32)
    return d(ah, bh) + d(al, bh) + d(ah, bl)


def _mm_nt_x2(a_bf16, b):
    bh, bl = _split2(b)
    return _mm_nt(a_bf16, bh) + _mm_nt(a_bf16, bl)


def _rms(x, g):
    return x * lax.rsqrt(jnp.mean(x * x, axis=-1, keepdims=True) + EPS) * g


def _silu(x):
    return x * jax.nn.sigmoid(x)


def _logsig(x):
    return jnp.minimum(x, 0.0) - jnp.log1p(jnp.exp(-jnp.abs(x)))


def _ada_kernel(c_ref, w_ref, b_ref, o_ref):
    o_ref[...] = _mm_x3(_silu(c_ref[...]), w_ref[...]) + b_ref[...]


def _ada(c_all, w_ada, b_ada):
    r = c_all.shape[0]
    return pl.pallas_call(
        _ada_kernel, grid=(6,),
        in_specs=[pl.BlockSpec((r, D), lambda j: (0, 0)),
                  pl.BlockSpec((D, D), lambda j: (0, j)),
                  pl.BlockSpec((1, D), lambda j: (0, j))],
        out_specs=pl.BlockSpec((r, D), lambda j: (0, j)),
        out_shape=jax.ShapeDtypeStruct((r, 6 * D), F32),
        compiler_params=_cparams("parallel"), name="ada")(c_all, w_ada, b_ada.reshape(1, 6 * D))


def _mod_spec(per_token, tm, tiles_per_batch, chunk):
    if per_token:
        return pl.BlockSpec((1, tm, D), lambda i, *_: (i, 0, chunk))
    return pl.BlockSpec((1, 1, D), lambda i, *_: (i // tiles_per_batch, 0, chunk))


def _group_ms(x, bd):
    hi, lo = _split2(x * x)
    d = functools.partial(jnp.dot, preferred_element_type=F32)
    return d(hi, bd) + d(lo, bd)


def _inproj_kernel(x_ref, sh_ref, sc_ref, g1_ref, wm_ref, wg_ref, gq_ref, gk_ref, bd_ref,
                   mq_ref, mk_ref, mv_ref, gt_ref, qa_ref, ka_ref, va_ref, kab_ref, vab_ref):
    h = _rms(x_ref[...], g1_ref[...]) * (1.0 + sc_ref[0]) + sh_ref[0]
    hb = h.astype(BF16)
    d = functools.partial(jnp.dot, preferred_element_type=F32)
    mq_ref[...] = d(hb, wm_ref[:, 0:MW]).astype(BF16)
    mk_ref[...] = (d(hb, wm_ref[:, MW:2 * MW]) * (MD ** -0.5)).astype(BF16)
    mv_ref[...] = d(hb, wm_ref[:, 2 * MW:3 * MW]).astype(BF16)
    o = 3 * MW
    aq = d(hb, wm_ref[:, o:o + AW])
    ak = d(hb, wm_ref[:, o + AW:o + 2 * AW])
    av = d(hb, wm_ref[:, o + 2 * AW:o + 3 * AW])
    bd = bd_ref[...]
    qa = aq * lax.rsqrt(_group_ms(aq, bd) + EPS) * gq_ref[...]
    ka = ak * lax.rsqrt(_group_ms(ak, bd) + EPS) * gk_ref[...]
    qa_ref[...] = qa.astype(BF16)
    ka_ref[...] = ka
    va_ref[...] = av
    kab_ref[...] = ka.astype(BF16)
    vab_ref[...] = av.astype(BF16)
    gt_ref[...] = d(hb, wg_ref[...])[:, 0:2 * MH]


def _inproj(x2, mod3, per_token, tm, tiles_per_batch, g1, wm, wg, gq, gk, bd):
    n = x2.shape[0]
    row = lambda w: pl.BlockSpec((tm, w), lambda i: (i, 0))
    full = lambda a: pl.BlockSpec(a.shape, lambda i: (0,) * a.ndim)
    sds = jax.ShapeDtypeStruct
    return pl.pallas_call(
        _inproj_kernel, grid=(n // tm,),
        in_specs=[row(D), _mod_spec(per_token, tm, tiles_per_batch, 0),
                  _mod_spec(per_token, tm, tiles_per_batch, 1),
                  full(g1), full(wm), full(wg), full(gq), full(gk), full(bd)],
        out_specs=[row(MW), row(MW), row(MW), row(2 * MH), row(AW), row(AW), row(AW), row(AW), row(AW)],
        out_shape=[sds((n, MW), BF16), sds((n, MW), BF16), sds((n, MW), BF16), sds((n, 2 * MH), F32),
                   sds((n, AW), BF16), sds((n, AW), F32), sds((n, AW), F32),
                   sds((n, AW), BF16), sds((n, AW), BF16)],
        compiler_params=_cparams("parallel"), name="inproj")(x2, mod3, mod3, g1, wm, wg, gq, gk, bd)


def _mlstm_kernel(q_ref, k_ref, v_ref, gr_ref, gc_ref, bgc_ref, bgr_ref, c0_ref, n0_ref, m0_ref,
                  h_ref, cn_ref, nn_ref, mn_ref, c_s, n_s, m_s, *, bb, t_valid):
    L = MCHUNK
    c = pl.program_id(1)

    @pl.when(c == 0)
    def _init():
        c_s[...] = c0_ref[...]
        n_s[...] = n0_ref[...]
        m_s[...] = m0_ref[...]

    rowi = lax.broadcasted_iota(I32, (L, L), 0)
    coli = lax.broadcasted_iota(I32, (L, L), 1)
    tri = rowi >= coli
    ok_r = (c * L + lax.broadcasted_iota(I32, (1, L), 1)) < t_valid
    ok_c = (c * L + lax.broadcasted_iota(I32, (L, 1), 0)) < t_valid
    for b in range(bb):
        g_r = gr_ref[b] + bgc_ref[...]
        g_c = gc_ref[b] + bgr_ref[...]
        for h in range(MH):
            li_r = jnp.where(ok_r, g_r[h:h + 1, :], NEG)
            lf_r = jnp.where(ok_r, _logsig(g_r[MH + h:MH + h + 1, :]), 0.0)
            li_c = jnp.where(ok_c, g_c[:, h:h + 1], NEG)
            lf_c = jnp.where(ok_c, _logsig(g_c[:, MH + h:MH + h + 1]), 0.0)
            b_c = jnp.sum(jnp.where(tri, lf_r, 0.0), axis=1, keepdims=True)
            b_r = jnp.sum(jnp.where(rowi <= coli, lf_c, 0.0), axis=0, keepdims=True)
            q = q_ref[b, :, h * MD:(h + 1) * MD]
            k = k_ref[b, :, h * MD:(h + 1) * MD]
            v = v_ref[b, :, h * MD:(h + 1) * MD]
            cm = c_s[b, h]
            nv = n_s[b, h:h + 1, :]
            m_prev = m_s[b, h:h + 1, 0:1]
            inter = m_prev + b_c
            intra = jnp.where(tri, li_r + b_c - b_r, NEG)
            m_t = jnp.maximum(inter, jnp.max(intra, axis=1, keepdims=True))
            w_inter = jnp.exp(inter - m_t)
            s = _mm_nt(q, k) * jnp.exp(intra - m_t)
            num = w_inter * _mm_nt(q, cm) + _mm(s, v)
            qn = jnp.sum(q.astype(F32) * nv, axis=1, keepdims=True)
            den = w_inter * qn + jnp.sum(s, axis=1, keepdims=True)
            h_ref[b, :, h * MD:(h + 1) * MD] = num / jnp.maximum(jnp.abs(den), jnp.exp(-m_t))
            m_last = m_t[L - 1:L, :]
            b_last = b_c[L - 1:L, :]
            w_c = jnp.exp(m_prev + b_last - m_last)
            w_s = jnp.exp(li_c + b_last - b_c - m_last)
            c_s[b, h] = w_c * cm + _mm_tn(v.astype(F32) * w_s, k)
            n_s[b, h:h + 1, :] = w_c * nv + jnp.sum(k.astype(F32) * w_s, axis=0, keepdims=True)
            m_s[b, h:h + 1, :] = jnp.broadcast_to(m_last, (1, MD))

    @pl.when(c == pl.num_programs(1) - 1)
    def _fin():
        cn_ref[...] = c_s[...]
        nn_ref[...] = n_s[...]
        mn_ref[...] = m_s[...]


def _mlstm(q, k, v, g_row, g_col, b_gates, c0, n0, m0, bb, t_valid):
    bsz, t, _ = q.shape
    L = MCHUNK
    seq = pl.BlockSpec((bb, L, MW), lambda g, c: (g, c, 0))
    st4 = pl.BlockSpec((bb, MH, MD, MD), lambda g, c: (g, 0, 0, 0))
    st3 = pl.BlockSpec((bb, MH, MD), lambda g, c: (g, 0, 0))
    sds = jax.ShapeDtypeStruct
    return pl.pallas_call(
        functools.partial(_mlstm_kernel, bb=bb, t_valid=t_valid),
        grid=(bsz // bb, t // L),
        in_specs=[seq, seq, seq,
                  pl.BlockSpec((bb, 2 * MH, L), lambda g, c: (g, 0, c)),
                  pl.BlockSpec((bb, L, 2 * MH), lambda g, c: (g, c, 0)),
                  pl.BlockSpec((2 * MH, 1), lambda g, c: (0, 0)),
                  pl.BlockSpec((1, 2 * MH), lambda g, c: (0, 0)),
                  st4, st3, st3],
        out_specs=[seq, st4, st3, st3],
        out_shape=[sds((bsz, t, MW), F32), sds((bsz, MH, MD, MD), F32),
                   sds((bsz, MH, MD), F32), sds((bsz, MH, MD), F32)],
        scratch_shapes=[pltpu.VMEM((bb, MH, MD, MD), F32), pltpu.VMEM((bb, MH, MD), F32),
                        pltpu.VMEM((bb, MH, MD), F32)],
        compiler_params=_cparams("parallel", "arbitrary"), name="mlstm")(
            q, k, v, g_row, g_col, b_gates.reshape(2 * MH, 1), b_gates.reshape(1, 2 * MH), c0, n0, m0)


def _kmean_kernel(k_ref, o_ref):
    for r in range(o_ref.shape[0]):
        o_ref[r:r + 1, :] = jnp.mean(k_ref[r * BLK:(r + 1) * BLK, :], axis=0, keepdims=True)


def _kmean(ka):
    n = ka.shape[0]
    nblk = n // BLK
    r = 8 if nblk % 8 == 0 else nblk
    return pl.pallas_call(
        _kmean_kernel, grid=(nblk // r,),
        in_specs=[pl.BlockSpec((r * BLK, AW), lambda i: (i, 0))],
        out_specs=pl.BlockSpec((r, AW), lambda i: (i, 0)),
        out_shape=jax.ShapeDtypeStruct((nblk, AW), F32),
        compiler_params=_cparams("parallel"), name="kmean")(ka)


def _select_blocks(gate, own, n_sel):
    nb = gate.shape[1]
    col = lax.broadcasted_iota(I32, gate.shape, 1).astype(F32)
    ownf = jnp.asarray(own, F32) if not isinstance(own, int) else float(own)
    g = jnp.where(col < ownf, gate, NEG)
    sel = jnp.zeros(gate.shape, F32)
    for _ in range(n_sel):
        mx = jnp.max(g, axis=1, keepdims=True)
        idx = jnp.min(jnp.where(g == mx, col, float(nb)), axis=1, keepdims=True)
        hit = col == idx
        sel = jnp.where(hit & (idx < ownf), 1.0, sel)
        g = jnp.where(hit, MASKV, g)
    return sel


def _moba_p_kernel(slope_ref, q_ref, k_ref, v_ref, km_ref, o_ref):
    h = pl.program_id(1)
    i = pl.program_id(2)
    slope = slope_ref[h]
    q = q_ref[0, 0]
    gate = _mm_nt_x2(q, km_ref[0, 0])
    n_sel = min(TOPB, km_ref.shape[2])
    sel = _select_blocks(gate, i.astype(F32), n_sel)
    colb = lax.broadcasted_iota(I32, sel.shape, 1)
    qs = (q.astype(F32) * (AD ** -0.5)).astype(BF16)
    rel = lax.broadcasted_iota(I32, (BLK, BLK), 0) - lax.broadcasted_iota(I32, (BLK, BLK), 1)

    def body(j, carry):
        m, l, acc = carry
        off = pl.multiple_of(j * BLK, BLK)
        kj = k_ref[0, 0, pl.ds(off, BLK), :]
        vj = v_ref[0, 0, pl.ds(off, BLK), :]
        dist = rel + (i - j) * BLK
        s = _mm_nt(qs, kj) - slope * dist.astype(F32)
        picked = jnp.sum(jnp.where(colb == j, sel, 0.0), axis=1, keepdims=True) > 0.5
        ok = (dist >= 0) & (picked | (j == i))
        s = jnp.where(ok, s, MASKV)
        m_new = jnp.maximum(m, jnp.max(s, axis=1, keepdims=True))
        p = jnp.exp(s - m_new)
        alpha = jnp.exp(m - m_new)
        l = alpha * l + jnp.sum(p, axis=1, keepdims=True)
        acc = alpha * acc + _mm(p, vj)
        return m_new, l, acc

    init = (jnp.full((BLK, 1), NEG, F32), jnp.zeros((BLK, 1), F32), jnp.zeros((BLK, AD), F32))
    _, l, acc = lax.fori_loop(0, i + 1, body, init)
    o_ref[0, 0] = acc / l


def _moba_prompt(qh, kh, vh, kmh, slopes):
    bsz, _, t, _ = qh.shape
    nb = t // BLK
    gs = pltpu.PrefetchScalarGridSpec(
        num_scalar_prefetch=1, grid=(bsz, AH, nb),
        in_specs=[pl.BlockSpec((1, 1, BLK, AD), lambda b, h, i, s: (b, h, i, 0)),
                  pl.BlockSpec((1, 1, t, AD), lambda b, h, i, s: (b, h, 0, 0)),
                  pl.BlockSpec((1, 1, t, AD), lambda b, h, i, s: (b, h, 0, 0)),
                  pl.BlockSpec((1, 1, nb, AD), lambda b, h, i, s: (b, h, 0, 0))],
        out_specs=pl.BlockSpec((1, 1, BLK, AD), lambda b, h, i, s: (b, h, i, 0)))
    return pl.pallas_call(
        _moba_p_kernel, grid_spec=gs,
        out_shape=jax.ShapeDtypeStruct((bsz, AH, t, AD), F32),
        compiler_params=_cparams("parallel", "parallel", "parallel"), name="moba_p")(
            slopes, qh, kh, vh, kmh)


def _moba_s_kernel(pt_ref, *refs, ppb, page, nbp, past, rq):
    k_pages = refs[0:ppb]
    v_pages = refs[ppb:2 * ppb]
    q_ref, kn_ref, vn_ref, o_ref, k_s, v_s, km_s = refs[2 * ppb:]
    b = pl.program_id(0)
    j = pl.program_id(1)
    lk = (nbp + 1) * BLK

    @pl.when((b == 0) & (j == 0))
    def _zero():
        k_s[past:lk, :] = jnp.zeros((BLK, AW), BF16)
        v_s[past:lk, :] = jnp.zeros((BLK, AW), BF16)
        km_s[...] = jnp.zeros(km_s.shape, F32)

    ksum = jnp.zeros((1, AW), F32)
    for p in range(ppb):
        kp = k_pages[p][0]
        off = pl.multiple_of(j * BLK + p * page, page)
        k_s[pl.ds(off, page), :] = kp.astype(BF16)
        v_s[pl.ds(off, page), :] = v_pages[p][0].astype(BF16)
        ksum = ksum + jnp.sum(kp, axis=0, keepdims=True)
    km_s[pl.ds(j, 1), :] = ksum * (1.0 / BLK)

    @pl.when(j == nbp - 1)
    def _attend():
        k_s[past:past + rq, :] = kn_ref[0].astype(BF16)
        v_s[past:past + rq, :] = vn_ref[0].astype(BF16)
        rows = AH * rq
        q8 = q_ref[0]
        qt = jnp.concatenate([q8] * AH, axis=0)
        rowh = lax.broadcasted_iota(I32, (rows, AW), 0) // rq
        laneh = lax.broadcasted_iota(I32, (rows, AW), 1) // AD
        qe = jnp.where(rowh == laneh, qt, 0.0)
        qe_b = qe.astype(BF16)
        gate = _mm_nt_x2(qe_b, km_s[...])
        sel = _select_blocks(gate, nbp, min(TOPB, nbp + 1))
        qs = (qe * (AD ** -0.5)).astype(BF16)
        hrow = lax.broadcasted_iota(I32, (rows, 1), 0) // rq
        slope = jnp.zeros((rows, 1), F32)
        for hh in range(AH):
            slope = jnp.where(hrow == hh, 2.0 ** (-8.0 * (hh + 1) / AH), slope)
        rq_pos = past + lax.broadcasted_iota(I32, (rows, BLK), 0) % rq
        lane = lax.broadcasted_iota(I32, (rows, BLK), 1)
        segs = []
        m = jnp.full((rows, 1), NEG, F32)
        for cb in range(nbp + 1):
            s = _mm_nt(qs, k_s[cb * BLK:(cb + 1) * BLK, :])
            dist = rq_pos - (lane + cb * BLK)
            s = s - slope * dist.astype(F32)
            ok = dist >= 0
            if cb < nbp:
                ok = ok & (sel[:, cb:cb + 1] > 0.5)
            s = jnp.where(ok, s, MASKV)
            segs.append(s)
            m = jnp.maximum(m, jnp.max(s, axis=1, keepdims=True))
        l = jnp.zeros((rows, 1), F32)
        acc = jnp.zeros((rows, AW), F32)
        for cb in range(nbp + 1):
            p = jnp.exp(segs[cb] - m)
            l = l + jnp.sum(p, axis=1, keepdims=True)
            acc = acc + _mm(p, v_s[cb * BLK:(cb + 1) * BLK, :])
        acc = acc / l
        out = jnp.zeros((rq, AW), F32)
        lh = lax.broadcasted_iota(I32, (rq, AW), 1) // AD
        for hh in range(AH):
            out = out + jnp.where(lh == hh, acc[hh * rq:(hh + 1) * rq, :], 0.0)
        o_ref[0] = out


def _moba_sample(page_table, cache_k, cache_v, q, k_new, v_new):
    db, n_pages = page_table.shape
    page = cache_k.shape[1]
    rq = q.shape[1]
    past = n_pages * page
    assert past % BLK == 0 and BLK % page == 0
    ppb = BLK // page
    nbp = past // BLK

    def page_spec(p):
        return pl.BlockSpec((1, page, AW), lambda b, j, pt: (pt[b * n_pages + j * ppb + p], 0, 0))

    tok = pl.BlockSpec((1, rq, AW), lambda b, j, pt: (b, 0, 0))
    gs = pltpu.PrefetchScalarGridSpec(
        num_scalar_prefetch=1, grid=(db, nbp),
        in_specs=[page_spec(p) for p in range(ppb)] * 2 + [tok, tok, tok],
        out_specs=tok,
        scratch_shapes=[pltpu.VMEM(((nbp + 1) * BLK, AW), BF16), pltpu.VMEM(((nbp + 1) * BLK, AW), BF16),
                        pltpu.VMEM((max(8, -(-(nbp + 1) // 8) * 8), AW), F32)])
    return pl.pallas_call(
        functools.partial(_moba_s_kernel, ppb=ppb, page=page, nbp=nbp, past=past, rq=rq),
        grid_spec=gs, out_shape=jax.ShapeDtypeStruct((db, rq, AW), F32),
        compiler_params=_cparams("arbitrary", "arbitrary"), name="moba_s")(
            page_table.reshape(-1), *([cache_k] * ppb), *([cache_v] * ppb), q, k_new, v_new)


def _merge_kernel(x_ref, hm_ref, ha_ref, sh1_ref, sc1_ref, gt1_ref, sh2_ref, sc2_ref, gt2_ref,
                  g1_ref, g2_ref, gmh_ref, wl_ref, wbm_ref, wba_ref, wo_ref, wr_ref, br_ref,
                  wsg_ref, wsu_ref, wsd_ref, cin_ref,
                  base_ref, h2_ref, idx_ref, wt_ref, rank_ref, cout_ref, cnt_s):
    i = pl.program_id(0)
    tm = x_ref.shape[0]

    @pl.when(i == 0)
    def _init():
        cnt_s[...] = cin_ref[...]

    x = x_ref[...]
    h1 = (_rms(x, g1_ref[...]) * (1.0 + sc1_ref[0]) + sh1_ref[0]).astype(BF16)
    d = functools.partial(jnp.dot, preferred_element_type=F32)
    mo = d(h1, wl_ref[:, 0:MW])
    ga = d(h1, wl_ref[:, MW:MW + D])
    gb = d(h1, wl_ref[:, MW + D:MW + 2 * D])
    hm = hm_ref[...]
    parts = []
    for h in range(MH):
        xh = hm[:, h * MD:(h + 1) * MD]
        parts.append(_rms(xh, gmh_ref[:, h * MD:(h + 1) * MD]))
    hmn = jnp.concatenate(parts, axis=1) * jax.nn.sigmoid(mo)
    merged = jax.nn.sigmoid(ga) * _mm(hmn, wbm_ref[...]) + jax.nn.sigmoid(gb) * _mm(ha_ref[...], wba_ref[...])
    x1 = x + gt1_ref[0] * _mm(merged, wo_ref[...])
    h2 = _rms(x1, g2_ref[...]) * (1.0 + sc2_ref[0]) + sh2_ref[0]
    h2_ref[...] = h2
    h2b = h2.astype(BF16)
    shared = _mm(_silu(d(h2b, wsg_ref[...])) * d(h2b, wsu_ref[...]), wsd_ref[...])
    base_ref[...] = x1 + gt2_ref[0] * shared
    scores = jax.nn.sigmoid(_mm_x3(h2, wr_ref[...]))
    col = lax.broadcasted_iota(I32, (tm, NE), 1).astype(F32)
    g = scores + br_ref[...]
    idxs, wts = [], []
    onehot = jnp.zeros((tm, NE), F32)
    for _ in range(TOPK):
        mx = jnp.max(g, axis=1, keepdims=True)
        idx = jnp.min(jnp.where(g == mx, col, float(NE)), axis=1, keepdims=True)
        hit = col == idx
        idxs.append(idx)
        wts.append(jnp.sum(jnp.where(hit, scores, 0.0), axis=1, keepdims=True))
        onehot = jnp.where(hit, 1.0, onehot)
        g = jnp.where(hit, MASKV, g)
    wsum = wts[0]
    for w in wts[1:]:
        wsum = wsum + w
    ri = lax.broadcasted_iota(I32, (tm, tm), 0)
    ci = lax.broadcasted_iota(I32, (tm, tm), 1)
    before = jnp.where(ci < ri, 1.0, 0.0).astype(BF16)
    pref = cnt_s[...] + d(before, onehot.astype(BF16))
    lane8 = lax.broadcasted_iota(I32, (tm, TOPK), 1)
    idx_o = jnp.zeros((tm, TOPK), F32)
    wt_o = jnp.zeros((tm, TOPK), F32)
    rk_o = jnp.zeros((tm, TOPK), F32)
    for kk in range(TOPK):
        rk = jnp.sum(jnp.where(col == idxs[kk], pref, 0.0), axis=1, keepdims=True)
        idx_o = jnp.where(lane8 == kk, idxs[kk], idx_o)
        wt_o = jnp.where(lane8 == kk, wts[kk] / wsum * ROUTE_SCALE, wt_o)
        rk_o = jnp.where(lane8 == kk, rk, rk_o)
    idx_ref[...] = idx_o.astype(I32)
    wt_ref[...] = wt_o
    rank_ref[...] = rk_o.astype(I32)
    cnt_s[...] = cnt_s[...] + jnp.sum(onehot, axis=0, keepdims=True)
    cout_ref[...] = cnt_s[...]


def _merge(x2, hm, ha, mod3, per_token, tm, tiles_per_batch, weights, cnt_in):
    n = x2.shape[0]
    row = lambda w: pl.BlockSpec((tm, w), lambda i: (i, 0))
    full = lambda a: pl.BlockSpec(a.shape, lambda i: (0,) * a.ndim)
    mods = [_mod_spec(per_token, tm, tiles_per_batch, c) for c in range(6)]
    sds = jax.ShapeDtypeStruct
    return pl.pallas_call(
        _merge_kernel, grid=(n // tm,),
        in_specs=[row(D), row(MW), row(AW)] + mods + [full(w) for w in weights] + [full(cnt_in)],
        out_specs=[row(D), row(D), row(TOPK), row(TOPK), row(TOPK), full(cnt_in)],
        out_shape=[sds((n, D), F32), sds((n, D), F32), sds((n, TOPK), I32), sds((n, TOPK), F32),
                   sds((n, TOPK), I32), sds((1, NE), F32)],
        scratch_shapes=[pltpu.VMEM((1, NE), F32)],
        compiler_params=_cparams("arbitrary"), name="merge")(
            x2, hm, ha, *([mod3] * 6), *weights, cnt_in)


def _row_copy(src, dst, sem):
    return pltpu.make_async_copy(src, dst, sem)


def _dispatch_kernel(pstart_ref, pend_ref, cnt_ref, idx_ref, rank_ref, h_ref, xs_ref, zbuf, sem):
    step = pl.program_id(0)
    td = h_ref.shape[0]

    @pl.when(step == 0)
    def _zero_tails():
        zbuf[...] = jnp.zeros(zbuf.shape, F32)

        def tail(e):
            return _row_copy(zbuf, xs_ref.at[pl.ds(pl.multiple_of(pend_ref[e] - GROUP, GROUP), GROUP)], sem)

        def start(e, c):
            @pl.when(cnt_ref[e] > 0)
            def _():
                tail(e).start()
            return c

        def wait(e, c):
            @pl.when(cnt_ref[e] > 0)
            def _():
                tail(e).wait()
            return c

        lax.fori_loop(0, NE, start, 0)
        lax.fori_loop(0, NE, wait, 0)

    def copy(t, kk):
        a = t * TOPK + kk
        dst = pstart_ref[idx_ref[0, 0, a]] + rank_ref[0, 0, a]
        return _row_copy(h_ref.at[pl.ds(t, 1)], xs_ref.at[pl.ds(dst, 1)], sem)

    def start(t, c):
        for kk in range(TOPK):
            copy(t, kk).start()
        return c

    def wait(t, c):
        for kk in range(TOPK):
            copy(t, kk).wait()
        return c

    lax.fori_loop(0, td, start, 0)
    lax.fori_loop(0, td, wait, 0)


def _dispatch(pstart, pend, cnt, idx, rank, h2, n_rows, td):
    n = h2.shape[0]
    smem_blk = pl.BlockSpec((1, 1, td * TOPK), lambda i, *_: (i, 0, 0), memory_space=pltpu.SMEM)
    gs = pltpu.PrefetchScalarGridSpec(
        num_scalar_prefetch=3, grid=(n // td,),
        in_specs=[smem_blk, smem_blk, pl.BlockSpec((td, D), lambda i, *_: (i, 0))],
        out_specs=pl.BlockSpec(memory_space=pl.ANY),
        scratch_shapes=[pltpu.VMEM((GROUP, D), F32), pltpu.SemaphoreType.DMA(())])
    return pl.pallas_call(
        _dispatch_kernel, grid_spec=gs, out_shape=jax.ShapeDtypeStruct((n_rows, D), F32),
        compiler_params=_cparams("arbitrary"), name="dispatch")(
            pstart, pend, cnt, idx.reshape(n // td, 1, td * TOPK), rank.reshape(n // td, 1, td * TOPK), h2)


def _experts_kernel(be_ref, nu_ref, x_ref, wg_ref, wu_ref, wd_ref, o_ref, wg_s, wu_s, wd_s):
    b = pl.program_id(0)

    @pl.when(b < nu_ref[0])
    def _run():
        @pl.when((b == 0) | (be_ref[b] != be_ref[jnp.maximum(b - 1, 0)]))
        def _load():
            wg_s[...] = wg_ref[0].astype(BF16)
            wu_s[...] = wu_ref[0].astype(BF16)
            wd_s[...] = wd_ref[0].astype(BF16)

        d = functools.partial(jnp.dot, preferred_element_type=F32)
        x = x_ref[...].astype(BF16)
        hb = _silu(d(x, wg_s[...])) * d(x, wu_s[...])
        o_ref[...] = d(hb.astype(BF16), wd_s[...])


def _experts(blk_e, n_used, xs, w_gate, w_up, w_down):
    n_rows = xs.shape[0]
    rows = lambda b, be, nu: (jnp.minimum(b, nu[0] - 1), 0)
    gs = pltpu.PrefetchScalarGridSpec(
        num_scalar_prefetch=2, grid=(n_rows // GROUP,),
        in_specs=[pl.BlockSpec((GROUP, D), rows),
                  pl.BlockSpec((1, D, DE), lambda b, be, nu: (be[b], 0, 0)),
                  pl.BlockSpec((1, D, DE), lambda b, be, nu: (be[b], 0, 0)),
                  pl.BlockSpec((1, DE, D), lambda b, be, nu: (be[b], 0, 0))],
        out_specs=pl.BlockSpec((GROUP, D), rows),
        scratch_shapes=[pltpu.VMEM((D, DE), BF16), pltpu.VMEM((D, DE), BF16), pltpu.VMEM((DE, D), BF16)])
    return pl.pallas_call(
        _experts_kernel, grid_spec=gs, out_shape=jax.ShapeDtypeStruct((n_rows, D), F32),
        compiler_params=_cparams("arbitrary"), name="experts")(blk_e, n_used, xs, w_gate, w_up, w_down)


def _combine_kernel(pstart_ref, idx_ref, rank_ref, ys_ref, wt_ref, base_ref, gt2_ref, o_ref, buf, sem):
    tc = base_ref.shape[0]

    def copy(t, kk):
        a = t * TOPK + kk
        src = pstart_ref[idx_ref[0, 0, a]] + rank_ref[0, 0, a]
        return _row_copy(ys_ref.at[pl.ds(src, 1)], buf.at[kk, pl.ds(t, 1)], sem)

    def start(t, c):
        for kk in range(TOPK):
            copy(t, kk).start()
        return c

    def wait(t, c):
        for kk in range(TOPK):
            copy(t, kk).wait()
        return c

    lax.fori_loop(0, tc, start, 0)
    lax.fori_loop(0, tc, wait, 0)
    wt = wt_ref[...]
    acc = jnp.zeros((tc, D), F32)
    for kk in range(TOPK):
        acc = acc + wt[:, kk:kk + 1] * buf[kk]
    o_ref[...] = base_ref[...] + gt2_ref[0] * acc


def _combine(pstart, idx, rank, ys, wt, base, mod3, per_token, tc, tiles_per_batch):
    n = base.shape[0]
    smem_blk = pl.BlockSpec((1, 1, tc * TOPK), lambda i, *_: (i, 0, 0), memory_space=pltpu.SMEM)
    gs = pltpu.PrefetchScalarGridSpec(
        num_scalar_prefetch=1, grid=(n // tc,),
        in_specs=[smem_blk, smem_blk, pl.BlockSpec(memory_space=pl.ANY),
                  pl.BlockSpec((tc, TOPK), lambda i, *_: (i, 0)),
                  pl.BlockSpec((tc, D), lambda i, *_: (i, 0)),
                  _mod_spec(per_token, tc, tiles_per_batch, 5)],
        out_specs=pl.BlockSpec((tc, D), lambda i, *_: (i, 0)),
        scratch_shapes=[pltpu.VMEM((TOPK, tc, D), F32), pltpu.SemaphoreType.DMA(())])
    return pl.pallas_call(
        _combine_kernel, grid_spec=gs, out_shape=jax.ShapeDtypeStruct((n, D), F32),
        compiler_params=_cparams("arbitrary"), name="combine")(
            pstart, idx.reshape(n // tc, 1, tc * TOPK), rank.reshape(n // tc, 1, tc * TOPK), ys, wt, base, mod3)


def _pick_tile(n, pref):
    t = pref
    while n % t:
        t //= 2
    return t


def kernel(x_prompt, x_sample, cache_k, cache_v, page_table, state_mlstm_C, state_mlstm_n, state_mlstm_m,
           c_prompt, c_sample, w_ada, b_ada, g_norm1, w_in, b_gates, g_q, g_k, g_mh, w_br_m, w_br_a, w_out,
           g_norm2, w_router, b_router, w_gate, w_up, w_down, ws_gate, ws_up, ws_down):
    depth = w_ada.shape[0]
    assert depth == 1
    bsz, t, _ = x_prompt.shape
    db, ts, _ = x_sample.shape
    n_p, n_s = bsz * t, db * ts
    assert t % BLK == 0 and t % MCHUNK == 0 and ts <= 8
    l = 0

    nc = bsz + db
    ncp = -(-nc // 8) * 8
    c_all = jnp.pad(jnp.concatenate([c_prompt, c_sample], axis=0), ((0, ncp - nc), (0, 0)))
    mod = _ada(c_all, w_ada[l], b_ada[l])
    mod_p = mod[:bsz].reshape(bsz, 1, 6 * D)
    tm_s = _pick_tile(n_s, 256)
    mod_s = jnp.repeat(mod[bsz:nc], ts, axis=0).reshape(n_s // tm_s, tm_s, 6 * D)

    wi = w_in[l]
    sl = lambda name: wi[:, _OFF[name][0]:_OFF[name][1]]
    wm = jnp.concatenate([sl("mq"), sl("mk"), sl("mv"), sl("aq"), sl("ak"), sl("av")], axis=1).astype(BF16)
    wg = jnp.pad(jnp.concatenate([sl("mi"), sl("mf")], axis=1), ((0, 0), (0, 128 - 2 * MH))).astype(BF16)
    wl = jnp.concatenate([sl("mo"), sl("ga"), sl("gb")], axis=1).astype(BF16)
    g1 = g_norm1[l].reshape(1, D)
    g2 = g_norm2[l].reshape(1, D)
    gq = jnp.tile(g_q[l], AH).reshape(1, AW)
    gk = jnp.tile(g_k[l], AH).reshape(1, AW)
    gmh = g_mh[l].reshape(1, MW)
    grp = jnp.arange(AW) // AD
    bd = jnp.where(grp[:, None] == grp[None, :], 1.0 / AD, 0.0).astype(BF16)
    merge_w = (g1, g2, gmh, wl, w_br_m[l].astype(BF16), w_br_a[l].astype(BF16), w_out[l].astype(BF16),
               w_router[l], b_router[l].reshape(1, NE), ws_gate[l].astype(BF16), ws_up[l].astype(BF16),
               ws_down[l].astype(BF16))
    slopes = 2.0 ** (-8.0 * jnp.arange(1, AH + 1, dtype=F32) / AH)

    tm_p = _pick_tile(t, 512)
    xp2 = x_prompt.reshape(n_p, D)
    xs2 = x_sample.reshape(n_s, D)
    mq_p, mk_p, mv_p, gt_p, qa_p, ka_p, va_p, kab_p, vab_p = _inproj(
        xp2, mod_p, False, tm_p, t // tm_p, g1, wm, wg, gq, gk, bd)
    mq_s, mk_s, mv_s, gt_s, qa_s, ka_s, va_s, _, _ = _inproj(
        xs2, mod_s, True, tm_s, 1, g1, wm, wg, gq, gk, bd)

    zeros = functools.partial(jnp.zeros, dtype=F32)
    gp = gt_p.reshape(bsz, t, 2 * MH)
    hm_p, c_p, nn_p, m_p = _mlstm(
        mq_p.reshape(bsz, t, MW), mk_p.reshape(bsz, t, MW), mv_p.reshape(bsz, t, MW),
        gp.transpose(0, 2, 1), gp, b_gates[l], zeros((bsz, MH, MD, MD)), zeros((bsz, MH, MD)),
        zeros((bsz, MH, MD)), _pick_tile(bsz, 2), t)
    pad_t = lambda a: jnp.pad(a.reshape(db, ts, -1), ((0, 0), (0, MCHUNK - ts), (0, 0)))
    gs_ = pad_t(gt_s)
    m0 = jnp.broadcast_to(state_mlstm_m[l].astype(F32)[:, :, None], (db, MH, MD))
    hm_s, c_sm, nn_s, m_sm = _mlstm(
        pad_t(mq_s), pad_t(mk_s), pad_t(mv_s), gs_.transpose(0, 2, 1), gs_, b_gates[l],
        state_mlstm_C[l].astype(F32), state_mlstm_n[l].astype(F32), m0, _pick_tile(db, 2), ts)
    hm_s = hm_s[:, :ts].reshape(n_s, MW)

    km = _kmean(ka_p)
    heads = lambda a: a.reshape(bsz, -1, AH, AD).transpose(0, 2, 1, 3)
    ha_p = _moba_prompt(heads(qa_p), heads(kab_p), heads(vab_p), heads(km), slopes)
    ha_p = ha_p.transpose(0, 2, 1, 3).reshape(n_p, AW)
    rq = 8
    pad_q = lambda a: jnp.pad(a.astype(F32).reshape(db, ts, AW), ((0, 0), (0, rq - ts), (0, 0)))
    n_pool, page = cache_k.shape[1], cache_k.shape[2]
    ha_s = _moba_sample(page_table, cache_k[l].reshape(n_pool, page, AW), cache_v[l].reshape(n_pool, page, AW),
                        pad_q(qa_s), pad_q(ka_s), pad_q(va_s))
    ha_s = ha_s[:, :ts].reshape(n_s, AW)

    tmm_p = _pick_tile(t, 256)
    base_p, h2_p, idx_p, wt_p, rk_p, cnt1 = _merge(
        xp2, hm_p.reshape(n_p, MW), ha_p, mod_p, False, tmm_p, t // tmm_p, merge_w, zeros((1, NE)))
    base_s, h2_s, idx_s, wt_s, rk_s, cnt2 = _merge(
        xs2, hm_s, ha_s, mod_s, True, tm_s, 1, merge_w, cnt1)

    n_all = n_p + n_s
    cnt = cnt2.reshape(NE).astype(I32)
    padded = (cnt + GROUP - 1) // GROUP * GROUP
    pend = jnp.cumsum(padded)
    pstart = pend - padded
    n_blocks = -(-(n_all * TOPK + NE * (GROUP - 1)) // GROUP)
    n_used = (pend[-1] // GROUP).reshape(1)
    blk = jnp.minimum(jnp.arange(n_blocks, dtype=I32), n_used[0] - 1) * GROUP
    blk_e = jnp.minimum(jnp.searchsorted(pend, blk, side="right"), NE - 1).astype(I32)

    h2 = jnp.concatenate([h2_p, h2_s], axis=0)
    idx = jnp.concatenate([idx_p, idx_s], axis=0)
    rank = jnp.concatenate([rk_p, rk_s], axis=0)
    td = _pick_tile(n_all, 128)
    xs = _dispatch(pstart, pend, cnt, idx, rank, h2, n_blocks * GROUP, td)
    ys = _experts(blk_e, n_used, xs, w_gate[l], w_up[l], w_down[l])
    tc_p = _pick_tile(t, 64)
    y_p = _combine(pstart, idx_p, rk_p, ys, wt_p, base_p, mod_p, False, tc_p, t // tc_p)
    tc_s = _pick_tile(tm_s, 64)
    mod_sc = mod_s.reshape(n_s // tc_s, tc_s, 6 * D)
    y_s = _combine(pstart, idx_s, rk_s, ys, wt_s, base_s, mod_sc, True, tc_s, 1)

    st = lambda a: a[None]
    return (y_p.reshape(bsz, t, D), y_s.reshape(db, ts, D),
            st(ka_p.reshape(bsz, t, AH, AD)), st(va_p.reshape(bsz, t, AH, AD)),
            st(c_p), st(nn_p), st(m_p[:, :, 0]),
            st(ka_s.reshape(db, ts, AH, AD)), st(va_s.reshape(db, ts, AH, AD)),
            st(c_sm), st(nn_s), st(m_sm[:, :, 0]))
```

```python
import functools

import jax
import jax.numpy as jnp
from jax import lax
from jax.experimental import pallas as pl
from jax.experimental.pallas import tpu as pltpu

F32, BF16, I32 = jnp.float32, jnp.bfloat16, jnp.int32

D = 1024
MH, MD = 4, 128
AH, AD = 8, 64
MW, AW = MH * MD, AH * AD
MCHUNK = 128
BLK = 256
TOPB = 3
NE, TOPK, DE, DSH = 256, 8, 256, 256
ROUTE_SCALE = 2.5
GROUP = 256
EPS = 1e-6
NEG = -1e30
MASKV = -3.0e38
VMEM_LIMIT = 56 * 1024 * 1024
_OFF = {}
_o = 0
for _n, _w in (("mq", MW), ("mk", MW), ("mv", MW), ("mo", MW), ("mi", MH), ("mf", MH),
               ("aq", AW), ("ak", AW), ("av", AW), ("ga", D), ("gb", D)):
    _OFF[_n] = (_o, _o + _w)
    _o += _w


def _cparams(*sem):
    return pltpu.CompilerParams(dimension_semantics=sem, vmem_limit_bytes=VMEM_LIMIT)


def _mm(a, b):
    return jnp.dot(a.astype(BF16), b.astype(BF16), preferred_element_type=F32)


def _mm_nt(a, b):
    return lax.dot_general(a.astype(BF16), b.astype(BF16), (((1,), (1,)), ((), ())),
                           preferred_element_type=F32)


def _mm_tn(a, b):
    return lax.dot_general(a.astype(BF16), b.astype(BF16), (((0,), (0,)), ((), ())),
                           preferred_element_type=F32)


def _split2(x):
    hi = x.astype(BF16)
    return hi, (x - hi.astype(F32)).astype(BF16)


def _mm_x3(a, b):
    ah, al = _split2(a)
    bh, bl = _split2(b)
    d = functools.partial(jnp.dot, preferred_element_type=F32)
    return d(ah, bh) + d(al, bh) + d(ah, bl)


def _mm_nt_x2(a_bf16, b):
    bh, bl = _split2(b)
    return _mm_nt(a_bf16, bh) + _mm_nt(a_bf16, bl)


def _rms(x, g):
    return x * lax.rsqrt(jnp.mean(x * x, axis=-1, keepdims=True) + EPS) * g


def _silu(x):
    return x * jax.nn.sigmoid(x)


def _logsig(x):
    return jnp.minimum(x, 0.0) - jnp.log1p(jnp.exp(-jnp.abs(x)))


def _ada_kernel(c_ref, w_ref, b_ref, o_ref):
    o_ref[...] = _mm_x3(_silu(c_ref[...]), w_ref[...]) + b_ref[...]


def _ada(c_all, w_ada, b_ada):
    r = c_all.shape[0]
    return pl.pallas_call(
        _ada_kernel, grid=(6,),
        in_specs=[pl.BlockSpec((r, D), lambda j: (0, 0)),
                  pl.BlockSpec((D, D), lambda j: (0, j)),
                  pl.BlockSpec((1, D), lambda j: (0, j))],
        out_specs=pl.BlockSpec((r, D), lambda j: (0, j)),
        out_shape=jax.ShapeDtypeStruct((r, 6 * D), F32),
        compiler_params=_cparams("parallel"), name="ada")(c_all, w_ada, b_ada.reshape(1, 6 * D))


def _mod_spec(per_token, tm, tiles_per_batch, chunk):
    if per_token:
        return pl.BlockSpec((1, tm, D), lambda i, *_: (i, 0, chunk))
    return pl.BlockSpec((1, 1, D), lambda i, *_: (i // tiles_per_batch, 0, chunk))


def _group_ms(x, bd):
    hi, lo = _split2(x * x)
    d = functools.partial(jnp.dot, preferred_element_type=F32)
    return d(hi, bd) + d(lo, bd)


def _inproj_kernel(x_ref, sh_ref, sc_ref, g1_ref, wm_ref, wg_ref, gq_ref, gk_ref, bd_ref,
                   mq_ref, mk_ref, mv_ref, gt_ref, qa_ref, ka_ref, va_ref, *attn_refs):
    h = _rms(x_ref[...], g1_ref[...]) * (1.0 + sc_ref[0]) + sh_ref[0]
    hb = h.astype(BF16)
    d = functools.partial(jnp.dot, preferred_element_type=F32)
    mq_ref[...] = d(hb, wm_ref[:, 0:MW]).astype(BF16)
    mk_ref[...] = (d(hb, wm_ref[:, MW:2 * MW]) * (MD ** -0.5)).astype(BF16)
    mv_ref[...] = d(hb, wm_ref[:, 2 * MW:3 * MW]).astype(BF16)
    o = 3 * MW
    aq = d(hb, wm_ref[:, o:o + AW])
    ak = d(hb, wm_ref[:, o + AW:o + 2 * AW])
    av = d(hb, wm_ref[:, o + 2 * AW:o + 3 * AW])
    bd = bd_ref[...]
    qa = aq * lax.rsqrt(_group_ms(aq, bd) + EPS) * gq_ref[...]
    ka = ak * lax.rsqrt(_group_ms(ak, bd) + EPS) * gk_ref[...]
    qa_ref[...] = qa.astype(BF16)
    ka_ref[...] = ka
    va_ref[...] = av
    gt_ref[...] = d(hb, wg_ref[...])[:, 0:2 * MH]
    if attn_refs:
        kb_ref, vt_ref = attn_refs
        for r in range(vt_ref.shape[0]):
            kb_ref[r] = ka[r * BLK:(r + 1) * BLK, :].astype(BF16)
            vt_ref[r] = av[r * BLK:(r + 1) * BLK, :].T.astype(BF16)


def _inproj(x2, mod3, per_token, tm, tiles_per_batch, g1, wm, wg, gq, gk, bd, attn_layouts):
    n = x2.shape[0]
    row = lambda w: pl.BlockSpec((tm, w), lambda i: (i, 0))
    full = lambda a: pl.BlockSpec(a.shape, lambda i: (0,) * a.ndim)
    sds = jax.ShapeDtypeStruct
    out_specs = [row(MW), row(MW), row(MW), row(2 * MH), row(AW), row(AW), row(AW)]
    out_shape = [sds((n, MW), BF16), sds((n, MW), BF16), sds((n, MW), BF16), sds((n, 2 * MH), F32),
                 sds((n, AW), BF16), sds((n, AW), F32), sds((n, AW), F32)]
    if attn_layouts:
        out_specs += [pl.BlockSpec((tm // BLK, BLK, AW), lambda i: (i, 0, 0)),
                      pl.BlockSpec((tm // BLK, AW, BLK), lambda i: (i, 0, 0))]
        out_shape += [sds((n // BLK, BLK, AW), BF16), sds((n // BLK, AW, BLK), BF16)]
    return pl.pallas_call(
        _inproj_kernel, grid=(n // tm,),
        in_specs=[row(D), _mod_spec(per_token, tm, tiles_per_batch, 0),
                  _mod_spec(per_token, tm, tiles_per_batch, 1),
                  full(g1), full(wm), full(wg), full(gq), full(gk), full(bd)],
        out_specs=out_specs, out_shape=out_shape,
        compiler_params=_cparams("parallel"), name="inproj")(x2, mod3, mod3, g1, wm, wg, gq, gk, bd)


def _mlstm_kernel(q_ref, k_ref, v_ref, gr_ref, gc_ref, bgc_ref, bgr_ref, c0_ref, n0_ref, m0_ref,
                  h_ref, cn_ref, nn_ref, mn_ref, c_s, n_s, m_s, *, bb, t_valid):
    L = MCHUNK
    c = pl.program_id(1)

    @pl.when(c == 0)
    def _init():
        c_s[...] = c0_ref[...]
        n_s[...] = n0_ref[...]
        m_s[...] = m0_ref[...]

    rowi = lax.broadcasted_iota(I32, (L, L), 0)
    coli = lax.broadcasted_iota(I32, (L, L), 1)
    tri = rowi >= coli
    ok_r = (c * L + lax.broadcasted_iota(I32, (1, L), 1)) < t_valid
    ok_c = (c * L + lax.broadcasted_iota(I32, (L, 1), 0)) < t_valid
    for b in range(bb):
        g_r = gr_ref[b] + bgc_ref[...]
        g_c = gc_ref[b] + bgr_ref[...]
        for h in range(MH):
            li_r = jnp.where(ok_r, g_r[h:h + 1, :], NEG)
            lf_r = jnp.where(ok_r, _logsig(g_r[MH + h:MH + h + 1, :]), 0.0)
            li_c = jnp.where(ok_c, g_c[:, h:h + 1], NEG)
            lf_c = jnp.where(ok_c, _logsig(g_c[:, MH + h:MH + h + 1]), 0.0)
            b_c = jnp.sum(jnp.where(tri, lf_r, 0.0), axis=1, keepdims=True)
            b_r = jnp.sum(jnp.where(rowi <= coli, lf_c, 0.0), axis=0, keepdims=True)
            q = q_ref[b, :, h * MD:(h + 1) * MD]
            k = k_ref[b, :, h * MD:(h + 1) * MD]
            v = v_ref[b, :, h * MD:(h + 1) * MD]
            cm = c_s[b, h]
            nv = n_s[b, h:h + 1, :]
            m_prev = m_s[b, h:h + 1, 0:1]
            inter = m_prev + b_c
            intra = jnp.where(tri, li_r + b_c - b_r, NEG)
            m_t = jnp.maximum(inter, jnp.max(intra, axis=1, keepdims=True))
            w_inter = jnp.exp(inter - m_t)
            s = _mm_nt(q, k) * jnp.exp(intra - m_t)
            num = w_inter * _mm_nt(q, cm) + _mm(s, v)
            qn = jnp.sum(q.astype(F32) * nv, axis=1, keepdims=True)
            den = w_inter * qn + jnp.sum(s, axis=1, keepdims=True)
            h_ref[b, :, h * MD:(h + 1) * MD] = num / jnp.maximum(jnp.abs(den), jnp.exp(-m_t))
            m_last = m_t[L - 1:L, :]
            b_last = b_c[L - 1:L, :]
            w_c = jnp.exp(m_prev + b_last - m_last)
            w_s = jnp.exp(li_c + b_last - b_c - m_last)
            c_s[b, h] = w_c * cm + _mm_tn(v.astype(F32) * w_s, k)
            n_s[b, h:h + 1, :] = w_c * nv + jnp.sum(k.astype(F32) * w_s, axis=0, keepdims=True)
            m_s[b, h:h + 1, :] = jnp.broadcast_to(m_last, (1, MD))

    @pl.when(c == pl.num_programs(1) - 1)
    def _fin():
        cn_ref[...] = c_s[...]
        nn_ref[...] = n_s[...]
        mn_ref[...] = m_s[...]


def _mlstm(q, k, v, g_row, g_col, b_gates, c0, n0, m0, bb, t_valid):
    bsz, t, _ = q.shape
    L = MCHUNK
    seq = pl.BlockSpec((bb, L, MW), lambda g, c: (g, c, 0))
    st4 = pl.BlockSpec((bb, MH, MD, MD), lambda g, c: (g, 0, 0, 0))
    st3 = pl.BlockSpec((bb, MH, MD), lambda g, c: (g, 0, 0))
    sds = jax.ShapeDtypeStruct
    return pl.pallas_call(
        functools.partial(_mlstm_kernel, bb=bb, t_valid=t_valid),
        grid=(bsz // bb, t // L),
        in_specs=[seq, seq, seq,
                  pl.BlockSpec((bb, 2 * MH, L), lambda g, c: (g, 0, c)),
                  pl.BlockSpec((bb, L, 2 * MH), lambda g, c: (g, c, 0)),
                  pl.BlockSpec((2 * MH, 1), lambda g, c: (0, 0)),
                  pl.BlockSpec((1, 2 * MH), lambda g, c: (0, 0)),
                  st4, st3, st3],
        out_specs=[seq, st4, st3, st3],
        out_shape=[sds((bsz, t, MW), F32), sds((bsz, MH, MD, MD), F32),
                   sds((bsz, MH, MD), F32), sds((bsz, MH, MD), F32)],
        scratch_shapes=[pltpu.VMEM((bb, MH, MD, MD), F32), pltpu.VMEM((bb, MH, MD), F32),
                        pltpu.VMEM((bb, MH, MD), F32)],
        compiler_params=_cparams("parallel", "arbitrary"), name="mlstm")(
            q, k, v, g_row, g_col, b_gates.reshape(2 * MH, 1), b_gates.reshape(1, 2 * MH), c0, n0, m0)


def _kmean_kernel(k_ref, o_ref):
    for r in range(o_ref.shape[0]):
        o_ref[r:r + 1, :] = jnp.mean(k_ref[r * BLK:(r + 1) * BLK, :], axis=0, keepdims=True)


def _kmean(ka):
    n = ka.shape[0]
    nblk = n // BLK
    r = 8 if nblk % 8 == 0 else nblk
    return pl.pallas_call(
        _kmean_kernel, grid=(nblk // r,),
        in_specs=[pl.BlockSpec((r * BLK, AW), lambda i: (i, 0))],
        out_specs=pl.BlockSpec((r, AW), lambda i: (i, 0)),
        out_shape=jax.ShapeDtypeStruct((nblk, AW), F32),
        compiler_params=_cparams("parallel"), name="kmean")(ka)


def _select_blocks(gate, own, n_sel):
    nb = gate.shape[1]
    col = lax.broadcasted_iota(I32, gate.shape, 1).astype(F32)
    ownf = jnp.asarray(own, F32) if not isinstance(own, int) else float(own)
    g = jnp.where(col < ownf, gate, NEG)
    sel = jnp.zeros(gate.shape, F32)
    for _ in range(n_sel):
        mx = jnp.max(g, axis=1, keepdims=True)
        idx = jnp.min(jnp.where(g == mx, col, float(nb)), axis=1, keepdims=True)
        hit = col == idx
        sel = jnp.where(hit & (idx < ownf), 1.0, sel)
        g = jnp.where(hit, MASKV, g)
    return sel


def _select_blocks_t(gate_t, own, n_sel):
    nb = gate_t.shape[0]
    row = lax.broadcasted_iota(I32, gate_t.shape, 0).astype(F32)
    g = jnp.where(row < own, gate_t, NEG)
    sel = jnp.zeros(gate_t.shape, F32)
    for _ in range(n_sel):
        mx = jnp.max(g, axis=0, keepdims=True)
        idx = jnp.min(jnp.where(g == mx, row, float(nb)), axis=0, keepdims=True)
        hit = row == idx
        sel = jnp.where(hit & (idx < own), 1.0, sel)
        g = jnp.where(hit, MASKV, g)
    return sel


def _moba_p_kernel(slope_ref, q_ref, k_ref, vt_ref, km_ref, o_ref, a_s, s_s, bias_s, qs_s, m_s, l_s, acc_s):
    b = pl.program_id(0)
    i = pl.program_id(1)
    nb = km_ref.shape[1]
    rel_t = lax.broadcasted_iota(I32, (BLK, BLK), 1) - lax.broadcasted_iota(I32, (BLK, BLK), 0)

    @pl.when((b == 0) & (i == 0))
    def _alibi():
        relf = rel_t.astype(F32)
        for h in range(AH):
            a_s[h] = slope_ref[h] * relf

    causal = rel_t >= 0
    own = i.astype(F32)
    for h in range(AH):
        hs = slice(h * AD, (h + 1) * AD)
        q = q_ref[0, :, hs]
        kmh, kml = _split2(km_ref[0, :, hs])
        gate_t = _mm_nt(kmh, q) + _mm_nt(kml, q)
        sel = _select_blocks_t(gate_t, own, min(TOPB, nb))
        bias_s[h] = jnp.where(sel > 0.5, 0.0, MASKV)
        qs = (q.astype(F32) * (AD ** -0.5)).astype(BF16)
        qs_s[h] = qs
        s_s[h] = jnp.where(causal, _mm_nt(k_ref[0, i, :, hs], qs) - a_s[h], MASKV)
    for h in range(AH):
        hs = slice(h * AD, (h + 1) * AD)
        m = jnp.max(s_s[h], axis=0, keepdims=True)
        p = jnp.exp(s_s[h] - m)
        m_s[h] = m
        l_s[h] = jnp.sum(p, axis=0, keepdims=True)
        acc_s[h] = _mm(vt_ref[0, i, hs, :], p)

    def body(j, c):
        dj = ((i - j) * BLK).astype(F32)
        for h in range(AH):
            hs = slice(h * AD, (h + 1) * AD)
            rowterm = bias_s[h, pl.ds(j, 1), :] - slope_ref[h] * dj
            s_s[h] = (_mm_nt(k_ref[0, j, :, hs], qs_s[h]) - a_s[h]) + rowterm
        for h in range(AH):
            hs = slice(h * AD, (h + 1) * AD)
            m = m_s[h]
            m_new = jnp.maximum(m, jnp.max(s_s[h], axis=0, keepdims=True))
            p = jnp.exp(s_s[h] - m_new)
            alpha = jnp.exp(m - m_new)
            l_s[h] = alpha * l_s[h] + jnp.sum(p, axis=0, keepdims=True)
            acc_s[h] = alpha * acc_s[h] + _mm(vt_ref[0, j, hs, :], p)
            m_s[h] = m_new
        return c

    lax.fori_loop(0, i, body, 0)
    out_t = jnp.concatenate([acc_s[h] / l_s[h] for h in range(AH)], axis=0)
    o_ref[0] = out_t.T


def _moba_prompt(q, kb, vt, km, slopes):
    bsz, t, _ = q.shape
    nb = t // BLK
    gs = pltpu.PrefetchScalarGridSpec(
        num_scalar_prefetch=1, grid=(bsz, nb),
        in_specs=[pl.BlockSpec((1, BLK, AW), lambda b, i, s: (b, i, 0)),
                  pl.BlockSpec((1, nb, BLK, AW), lambda b, i, s: (b, 0, 0, 0)),
                  pl.BlockSpec((1, nb, AW, BLK), lambda b, i, s: (b, 0, 0, 0)),
                  pl.BlockSpec((1, nb, AW), lambda b, i, s: (b, 0, 0))],
        out_specs=pl.BlockSpec((1, BLK, AW), lambda b, i, s: (b, i, 0)),
        scratch_shapes=[pltpu.VMEM((AH, BLK, BLK), F32), pltpu.VMEM((AH, BLK, BLK), F32),
                        pltpu.VMEM((AH, nb, BLK), F32),
                        pltpu.VMEM((AH, BLK, AD), BF16), pltpu.VMEM((AH, 1, BLK), F32),
                        pltpu.VMEM((AH, 1, BLK), F32), pltpu.VMEM((AH, AD, BLK), F32)])
    return pl.pallas_call(
        _moba_p_kernel, grid_spec=gs,
        out_shape=jax.ShapeDtypeStruct((bsz, t, AW), F32),
        compiler_params=_cparams("arbitrary", "arbitrary"), name="moba_p")(slopes, q, kb, vt, km)


def _moba_s_kernel(pt_ref, *refs, ppb, page, nbp, past, rq):
    k_pages = refs[0:ppb]
    v_pages = refs[ppb:2 * ppb]
    q_ref, kn_ref, vn_ref, o_ref, k_s, v_s, km_s = refs[2 * ppb:]
    b = pl.program_id(0)
    j = pl.program_id(1)
    lk = (nbp + 1) * BLK

    @pl.when((b == 0) & (j == 0))
    def _zero():
        k_s[past:lk, :] = jnp.zeros((BLK, AW), BF16)
        v_s[past:lk, :] = jnp.zeros((BLK, AW), BF16)
        km_s[...] = jnp.zeros(km_s.shape, F32)

    ksum = jnp.zeros((1, AW), F32)
    for p in range(ppb):
        kp = k_pages[p][0]
        off = pl.multiple_of(j * BLK + p * page, page)
        k_s[pl.ds(off, page), :] = kp
        v_s[pl.ds(off, page), :] = v_pages[p][0]
        ksum = ksum + jnp.sum(kp.astype(F32), axis=0, keepdims=True)
    km_s[pl.ds(j, 1), :] = ksum * (1.0 / BLK)

    @pl.when(j == nbp - 1)
    def _attend():
        k_s[past:past + rq, :] = kn_ref[0].astype(BF16)
        v_s[past:past + rq, :] = vn_ref[0].astype(BF16)
        rows = AH * rq
        q8 = q_ref[0]
        qt = jnp.concatenate([q8] * AH, axis=0)
        rowh = lax.broadcasted_iota(I32, (rows, AW), 0) // rq
        laneh = lax.broadcasted_iota(I32, (rows, AW), 1) // AD
        qe = jnp.where(rowh == laneh, qt, 0.0)
        qe_b = qe.astype(BF16)
        gate = _mm_nt_x2(qe_b, km_s[...])
        sel = _select_blocks(gate, nbp, min(TOPB, nbp + 1))
        qs = (qe * (AD ** -0.5)).astype(BF16)
        hrow = lax.broadcasted_iota(I32, (rows, 1), 0) // rq
        slope = jnp.zeros((rows, 1), F32)
        for hh in range(AH):
            slope = jnp.where(hrow == hh, 2.0 ** (-8.0 * (hh + 1) / AH), slope)
        rq_pos = past + lax.broadcasted_iota(I32, (rows, BLK), 0) % rq
        lane = lax.broadcasted_iota(I32, (rows, BLK), 1)
        segs = []
        m = jnp.full((rows, 1), NEG, F32)
        for cb in range(nbp + 1):
            s = _mm_nt(qs, k_s[cb * BLK:(cb + 1) * BLK, :])
            dist = rq_pos - (lane + cb * BLK)
            s = s - slope * dist.astype(F32)
            ok = dist >= 0
            if cb < nbp:
                ok = ok & (sel[:, cb:cb + 1] > 0.5)
            s = jnp.where(ok, s, MASKV)
            segs.append(s)
            m = jnp.maximum(m, jnp.max(s, axis=1, keepdims=True))
        l = jnp.zeros((rows, 1), F32)
        acc = jnp.zeros((rows, AW), F32)
        for cb in range(nbp + 1):
            p = jnp.exp(segs[cb] - m)
            l = l + jnp.sum(p, axis=1, keepdims=True)
            acc = acc + _mm(p, v_s[cb * BLK:(cb + 1) * BLK, :])
        acc = acc / l
        out = jnp.zeros((rq, AW), F32)
        lh = lax.broadcasted_iota(I32, (rq, AW), 1) // AD
        for hh in range(AH):
            out = out + jnp.where(lh == hh, acc[hh * rq:(hh + 1) * rq, :], 0.0)
        o_ref[0] = out


def _moba_sample(page_table, cache_k, cache_v, q, k_new, v_new):
    db, n_pages = page_table.shape
    page = cache_k.shape[1]
    rq = q.shape[1]
    past = n_pages * page
    assert past % BLK == 0 and BLK % page == 0
    ppb = BLK // page
    nbp = past // BLK

    def page_spec(p):
        return pl.BlockSpec((1, page, AW), lambda b, j, pt: (pt[b * n_pages + j * ppb + p], 0, 0))

    tok = pl.BlockSpec((1, rq, AW), lambda b, j, pt: (b, 0, 0))
    gs = pltpu.PrefetchScalarGridSpec(
        num_scalar_prefetch=1, grid=(db, nbp),
        in_specs=[page_spec(p) for p in range(ppb)] * 2 + [tok, tok, tok],
        out_specs=tok,
        scratch_shapes=[pltpu.VMEM(((nbp + 1) * BLK, AW), BF16), pltpu.VMEM(((nbp + 1) * BLK, AW), BF16),
                        pltpu.VMEM((max(8, -(-(nbp + 1) // 8) * 8), AW), F32)])
    return pl.pallas_call(
        functools.partial(_moba_s_kernel, ppb=ppb, page=page, nbp=nbp, past=past, rq=rq),
        grid_spec=gs, out_shape=jax.ShapeDtypeStruct((db, rq, AW), F32),
        compiler_params=_cparams("arbitrary", "arbitrary"), name="moba_s")(
            page_table.reshape(-1), *([cache_k] * ppb), *([cache_v] * ppb), q, k_new, v_new)


def _merge_kernel(x_ref, hm_ref, ha_ref, sh1_ref, sc1_ref, gt1_ref, sh2_ref, sc2_ref, gt2_ref,
                  g1_ref, g2_ref, gmh_ref, wl_ref, wbm_ref, wba_ref, wo_ref, wr_ref, br_ref,
                  wsg_ref, wsu_ref, wsd_ref, cin_ref,
                  base_ref, h2_ref, idx_ref, wt_ref, rank_ref, cout_ref, cnt_s):
    i = pl.program_id(0)
    tm = x_ref.shape[0]

    @pl.when(i == 0)
    def _init():
        cnt_s[...] = cin_ref[...]

    x = x_ref[...]
    h1 = (_rms(x, g1_ref[...]) * (1.0 + sc1_ref[0]) + sh1_ref[0]).astype(BF16)
    d = functools.partial(jnp.dot, preferred_element_type=F32)
    mo = d(h1, wl_ref[:, 0:MW])
    ga = d(h1, wl_ref[:, MW:MW + D])
    gb = d(h1, wl_ref[:, MW + D:MW + 2 * D])
    hm = hm_ref[...]
    parts = []
    for h in range(MH):
        xh = hm[:, h * MD:(h + 1) * MD]
        parts.append(_rms(xh, gmh_ref[:, h * MD:(h + 1) * MD]))
    hmn = jnp.concatenate(parts, axis=1) * jax.nn.sigmoid(mo)
    merged = jax.nn.sigmoid(ga) * _mm(hmn, wbm_ref[...]) + jax.nn.sigmoid(gb) * _mm(ha_ref[...], wba_ref[...])
    x1 = x + gt1_ref[0] * _mm(merged, wo_ref[...])
    h2 = _rms(x1, g2_ref[...]) * (1.0 + sc2_ref[0]) + sh2_ref[0]
    h2_ref[...] = h2
    h2b = h2.astype(BF16)
    shared = _mm(_silu(d(h2b, wsg_ref[...])) * d(h2b, wsu_ref[...]), wsd_ref[...])
    base_ref[...] = x1 + gt2_ref[0] * shared
    scores = jax.nn.sigmoid(_mm_x3(h2, wr_ref[...]))
    col = lax.broadcasted_iota(I32, (tm, NE), 1).astype(F32)
    g = scores + br_ref[...]
    idxs, wts = [], []
    onehot = jnp.zeros((tm, NE), F32)
    for _ in range(TOPK):
        mx = jnp.max(g, axis=1, keepdims=True)
        idx = jnp.min(jnp.where(g == mx, col, float(NE)), axis=1, keepdims=True)
        hit = col == idx
        idxs.append(idx)
        wts.append(jnp.sum(jnp.where(hit, scores, 0.0), axis=1, keepdims=True))
        onehot = jnp.where(hit, 1.0, onehot)
        g = jnp.where(hit, MASKV, g)
    wsum = wts[0]
    for w in wts[1:]:
        wsum = wsum + w
    ri = lax.broadcasted_iota(I32, (tm, tm), 0)
    ci = lax.broadcasted_iota(I32, (tm, tm), 1)
    before = jnp.where(ci < ri, 1.0, 0.0).astype(BF16)
    pref = cnt_s[...] + d(before, onehot.astype(BF16))
    lane8 = lax.broadcasted_iota(I32, (tm, TOPK), 1)
    idx_o = jnp.zeros((tm, TOPK), F32)
    wt_o = jnp.zeros((tm, TOPK), F32)
    rk_o = jnp.zeros((tm, TOPK), F32)
    for kk in range(TOPK):
        rk = jnp.sum(jnp.where(col == idxs[kk], pref, 0.0), axis=1, keepdims=True)
        idx_o = jnp.where(lane8 == kk, idxs[kk], idx_o)
        wt_o = jnp.where(lane8 == kk, wts[kk] / wsum * ROUTE_SCALE, wt_o)
        rk_o = jnp.where(lane8 == kk, rk, rk_o)
    idx_ref[...] = idx_o.astype(I32)
    wt_ref[...] = wt_o
    rank_ref[...] = rk_o.astype(I32)
    cnt_s[...] = cnt_s[...] + jnp.sum(onehot, axis=0, keepdims=True)
    cout_ref[...] = cnt_s[...]


def _merge(x2, hm, ha, mod3, per_token, tm, tiles_per_batch, weights, cnt_in):
    n = x2.shape[0]
    row = lambda w: pl.BlockSpec((tm, w), lambda i: (i, 0))
    full = lambda a: pl.BlockSpec(a.shape, lambda i: (0,) * a.ndim)
    mods = [_mod_spec(per_token, tm, tiles_per_batch, c) for c in range(6)]
    sds = jax.ShapeDtypeStruct
    return pl.pallas_call(
        _merge_kernel, grid=(n // tm,),
        in_specs=[row(D), row(MW), row(AW)] + mods + [full(w) for w in weights] + [full(cnt_in)],
        out_specs=[row(D), row(D), row(TOPK), row(TOPK), row(TOPK), full(cnt_in)],
        out_shape=[sds((n, D), F32), sds((n, D), F32), sds((n, TOPK), I32), sds((n, TOPK), F32),
                   sds((n, TOPK), I32), sds((1, NE), F32)],
        scratch_shapes=[pltpu.VMEM((1, NE), F32)],
        compiler_params=_cparams("arbitrary"), name="merge")(
            x2, hm, ha, *([mod3] * 6), *weights, cnt_in)


def _row_copy(src, dst, sem):
    return pltpu.make_async_copy(src, dst, sem)


def _dispatch_kernel(pend_ref, cnt_ref, dest_ref, h_ref, xs_ref, zbuf, sem):
    step = pl.program_id(0)
    td = h_ref.shape[0]

    @pl.when(step == 0)
    def _zero_tails():
        zbuf[...] = jnp.zeros(zbuf.shape, F32)

        def tail(e):
            return _row_copy(zbuf, xs_ref.at[pl.ds(pl.multiple_of(pend_ref[e] - GROUP, GROUP), GROUP)], sem)

        def start(e, c):
            @pl.when(cnt_ref[e] > 0)
            def _():
                tail(e).start()
            return c

        def wait(e, c):
            @pl.when(cnt_ref[e] > 0)
            def _():
                tail(e).wait()
            return c

        lax.fori_loop(0, NE, start, 0)
        lax.fori_loop(0, NE, wait, 0)

    def copy(t, kk):
        dst = dest_ref[0, 0, t * TOPK + kk]
        return _row_copy(h_ref.at[pl.ds(t, 1)], xs_ref.at[pl.ds(dst, 1)], sem)

    def start(t, c):
        for kk in range(TOPK):
            copy(t, kk).start(priority=kk % 2)
        return c

    def wait(t, c):
        for kk in range(TOPK):
            copy(t, kk).wait()
        return c

    lax.fori_loop(0, td, start, 0)
    lax.fori_loop(0, td, wait, 0)


def _dispatch(pend, cnt, dest, h2, n_rows, td):
    n = h2.shape[0]
    gs = pltpu.PrefetchScalarGridSpec(
        num_scalar_prefetch=2, grid=(n // td,),
        in_specs=[pl.BlockSpec((1, 1, td * TOPK), lambda i, *_: (i, 0, 0), memory_space=pltpu.SMEM),
                  pl.BlockSpec((td, D), lambda i, *_: (i, 0))],
        out_specs=pl.BlockSpec(memory_space=pl.ANY),
        scratch_shapes=[pltpu.VMEM((GROUP, D), F32), pltpu.SemaphoreType.DMA(())])
    return pl.pallas_call(
        _dispatch_kernel, grid_spec=gs, out_shape=jax.ShapeDtypeStruct((n_rows, D), F32),
        compiler_params=_cparams("arbitrary"), name="dispatch")(
            pend, cnt, dest.reshape(n // td, 1, td * TOPK), h2)


def _experts_kernel(be_ref, nu_ref, x_ref, wg_ref, wu_ref, wd_ref, o_ref, wg_s, wu_s, wd_s):
    b = pl.program_id(0)

    @pl.when(b < nu_ref[0])
    def _run():
        @pl.when((b == 0) | (be_ref[b] != be_ref[jnp.maximum(b - 1, 0)]))
        def _load():
            wg_s[...] = wg_ref[0].astype(BF16)
            wu_s[...] = wu_ref[0].astype(BF16)
            wd_s[...] = wd_ref[0].astype(BF16)

        d = functools.partial(jnp.dot, preferred_element_type=F32)
        x = x_ref[...].astype(BF16)
        hb = _silu(d(x, wg_s[...])) * d(x, wu_s[...])
        o_ref[...] = d(hb.astype(BF16), wd_s[...])


def _experts(blk_e, n_used, xs, w_gate, w_up, w_down):
    n_rows = xs.shape[0]
    rows = lambda b, be, nu: (jnp.minimum(b, nu[0] - 1), 0)
    gs = pltpu.PrefetchScalarGridSpec(
        num_scalar_prefetch=2, grid=(n_rows // GROUP,),
        in_specs=[pl.BlockSpec((GROUP, D), rows),
                  pl.BlockSpec((1, D, DE), lambda b, be, nu: (be[b], 0, 0)),
                  pl.BlockSpec((1, D, DE), lambda b, be, nu: (be[b], 0, 0)),
                  pl.BlockSpec((1, DE, D), lambda b, be, nu: (be[b], 0, 0))],
        out_specs=pl.BlockSpec((GROUP, D), rows),
        scratch_shapes=[pltpu.VMEM((D, DE), BF16), pltpu.VMEM((D, DE), BF16), pltpu.VMEM((DE, D), BF16)])
    return pl.pallas_call(
        _experts_kernel, grid_spec=gs, out_shape=jax.ShapeDtypeStruct((n_rows, D), F32),
        compiler_params=_cparams("arbitrary"), name="experts")(blk_e, n_used, xs, w_gate, w_up, w_down)


def _combine_kernel(dest_ref, ys_ref, wt_ref, base_ref, gt2_ref, o_ref, buf, sem):
    tc = base_ref.shape[0]

    def copy(t, kk):
        src = dest_ref[0, 0, t * TOPK + kk]
        return _row_copy(ys_ref.at[pl.ds(src, 1)], buf.at[kk, pl.ds(t, 1)], sem)

    def start(t, c):
        for kk in range(TOPK):
            copy(t, kk).start(priority=kk % 2)
        return c

    def wait(t, c):
        for kk in range(TOPK):
            copy(t, kk).wait()
        return c

    lax.fori_loop(0, tc, start, 0)
    lax.fori_loop(0, tc, wait, 0)
    wt = wt_ref[...]
    acc = jnp.zeros((tc, D), F32)
    for kk in range(TOPK):
        acc = acc + wt[:, kk:kk + 1] * buf[kk]
    o_ref[...] = base_ref[...] + gt2_ref[0] * acc


def _combine(dest, ys, wt, base, mod3, per_token, tc, tiles_per_batch):
    n = base.shape[0]
    return pl.pallas_call(
        _combine_kernel, grid=(n // tc,),
        in_specs=[pl.BlockSpec((1, 1, tc * TOPK), lambda i: (i, 0, 0), memory_space=pltpu.SMEM),
                  pl.BlockSpec(memory_space=pl.ANY),
                  pl.BlockSpec((tc, TOPK), lambda i: (i, 0)),
                  pl.BlockSpec((tc, D), lambda i: (i, 0)),
                  _mod_spec(per_token, tc, tiles_per_batch, 5)],
        out_specs=pl.BlockSpec((tc, D), lambda i: (i, 0)),
        out_shape=jax.ShapeDtypeStruct((n, D), F32),
        scratch_shapes=[pltpu.VMEM((TOPK, tc, D), F32), pltpu.SemaphoreType.DMA(())],
        compiler_params=_cparams("arbitrary"), name="combine")(
            dest.reshape(n // tc, 1, tc * TOPK), ys, wt, base, mod3)


def _pick_tile(n, pref):
    t = pref
    while n % t:
        t //= 2
    return t


def kernel(x_prompt, x_sample, cache_k, cache_v, page_table, state_mlstm_C, state_mlstm_n, state_mlstm_m,
           c_prompt, c_sample, w_ada, b_ada, g_norm1, w_in, b_gates, g_q, g_k, g_mh, w_br_m, w_br_a, w_out,
           g_norm2, w_router, b_router, w_gate, w_up, w_down, ws_gate, ws_up, ws_down):
    depth = w_ada.shape[0]
    assert depth == 1
    bsz, t, _ = x_prompt.shape
    db, ts, _ = x_sample.shape
    n_p, n_s = bsz * t, db * ts
    assert t % BLK == 0 and t % MCHUNK == 0 and ts <= 8
    l = 0

    nc = bsz + db
    ncp = -(-nc // 8) * 8
    c_all = jnp.pad(jnp.concatenate([c_prompt, c_sample], axis=0), ((0, ncp - nc), (0, 0)))
    mod = _ada(c_all, w_ada[l], b_ada[l])
    mod_p = mod[:bsz].reshape(bsz, 1, 6 * D)
    tm_s = _pick_tile(n_s, 256)
    mod_s = jnp.repeat(mod[bsz:nc], ts, axis=0).reshape(n_s // tm_s, tm_s, 6 * D)

    wi = w_in[l]
    sl = lambda name: wi[:, _OFF[name][0]:_OFF[name][1]]
    wm = jnp.concatenate([sl("mq"), sl("mk"), sl("mv"), sl("aq"), sl("ak"), sl("av")], axis=1).astype(BF16)
    wg = jnp.pad(jnp.concatenate([sl("mi"), sl("mf")], axis=1), ((0, 0), (0, 128 - 2 * MH))).astype(BF16)
    wl = jnp.concatenate([sl("mo"), sl("ga"), sl("gb")], axis=1).astype(BF16)
    g1 = g_norm1[l].reshape(1, D)
    g2 = g_norm2[l].reshape(1, D)
    gq = jnp.tile(g_q[l], AH).reshape(1, AW)
    gk = jnp.tile(g_k[l], AH).reshape(1, AW)
    gmh = g_mh[l].reshape(1, MW)
    grp = jnp.arange(AW) // AD
    bd = jnp.where(grp[:, None] == grp[None, :], 1.0 / AD, 0.0).astype(BF16)
    merge_w = (g1, g2, gmh, wl, w_br_m[l].astype(BF16), w_br_a[l].astype(BF16), w_out[l].astype(BF16),
               w_router[l], b_router[l].reshape(1, NE), ws_gate[l].astype(BF16), ws_up[l].astype(BF16),
               ws_down[l].astype(BF16))
    slopes = 2.0 ** (-8.0 * jnp.arange(1, AH + 1, dtype=F32) / AH)

    tm_p = _pick_tile(t, 512)
    xp2 = x_prompt.reshape(n_p, D)
    xs2 = x_sample.reshape(n_s, D)
    mq_p, mk_p, mv_p, gt_p, qa_p, ka_p, va_p, kb_p, vt_p = _inproj(
        xp2, mod_p, False, tm_p, t // tm_p, g1, wm, wg, gq, gk, bd, True)
    mq_s, mk_s, mv_s, gt_s, qa_s, ka_s, va_s = _inproj(
        xs2, mod_s, True, tm_s, 1, g1, wm, wg, gq, gk, bd, False)

    zeros = functools.partial(jnp.zeros, dtype=F32)
    gp = gt_p.reshape(bsz, t, 2 * MH)
    hm_p, c_p, nn_p, m_p = _mlstm(
        mq_p.reshape(bsz, t, MW), mk_p.reshape(bsz, t, MW), mv_p.reshape(bsz, t, MW),
        gp.transpose(0, 2, 1), gp, b_gates[l], zeros((bsz, MH, MD, MD)), zeros((bsz, MH, MD)),
        zeros((bsz, MH, MD)), _pick_tile(bsz, 2), t)
    pad_t = lambda a: jnp.pad(a.reshape(db, ts, -1), ((0, 0), (0, MCHUNK - ts), (0, 0)))
    gs_ = pad_t(gt_s)
    m0 = jnp.broadcast_to(state_mlstm_m[l].astype(F32)[:, :, None], (db, MH, MD))
    hm_s, c_sm, nn_s, m_sm = _mlstm(
        pad_t(mq_s), pad_t(mk_s), pad_t(mv_s), gs_.transpose(0, 2, 1), gs_, b_gates[l],
        state_mlstm_C[l].astype(F32), state_mlstm_n[l].astype(F32), m0, _pick_tile(db, 2), ts)
    hm_s = hm_s[:, :ts].reshape(n_s, MW)

    nb = t // BLK
    km = _kmean(ka_p)
    ha_p = _moba_prompt(qa_p.reshape(bsz, t, AW), kb_p.reshape(bsz, nb, BLK, AW), vt_p.reshape(bsz, nb, AW, BLK),
                        km.reshape(bsz, nb, AW), slopes).reshape(n_p, AW)
    rq = 8
    pad_q = lambda a: jnp.pad(a.astype(F32).reshape(db, ts, AW), ((0, 0), (0, rq - ts), (0, 0)))
    n_pool, page = cache_k.shape[1], cache_k.shape[2]
    dense = lambda c: c[l].reshape(n_pool, page, AW).astype(BF16)
    ha_s = _moba_sample(page_table, dense(cache_k), dense(cache_v), pad_q(qa_s), pad_q(ka_s), pad_q(va_s))
    ha_s = ha_s[:, :ts].reshape(n_s, AW)

    tmm_p = _pick_tile(t, 256)
    base_p, h2_p, idx_p, wt_p, rk_p, cnt1 = _merge(
        xp2, hm_p.reshape(n_p, MW), ha_p, mod_p, False, tmm_p, t // tmm_p, merge_w, zeros((1, NE)))
    base_s, h2_s, idx_s, wt_s, rk_s, cnt2 = _merge(
        xs2, hm_s, ha_s, mod_s, True, tm_s, 1, merge_w, cnt1)

    n_all = n_p + n_s
    cnt = cnt2.reshape(NE).astype(I32)
    padded = (cnt + GROUP - 1) // GROUP * GROUP
    pend = jnp.cumsum(padded)
    pstart = pend - padded
    n_blocks = -(-(n_all * TOPK + NE * (GROUP - 1)) // GROUP)
    n_used = (pend[-1] // GROUP).reshape(1)
    blk = jnp.minimum(jnp.arange(n_blocks, dtype=I32), n_used[0] - 1) * GROUP
    blk_e = jnp.minimum(jnp.searchsorted(pend, blk, side="right"), NE - 1).astype(I32)

    h2 = jnp.concatenate([h2_p, h2_s], axis=0)
    dest_p = pstart[idx_p] + rk_p
    dest_s = pstart[idx_s] + rk_s
    td = _pick_tile(n_all, 128)
    xs = _dispatch(pend, cnt, jnp.concatenate([dest_p, dest_s], axis=0), h2, n_blocks * GROUP, td)
    ys = _experts(blk_e, n_used, xs, w_gate[l], w_up[l], w_down[l])
    tc_p = _pick_tile(t, 64)
    y_p = _combine(dest_p, ys, wt_p, base_p, mod_p, False, tc_p, t // tc_p)
    tc_s = _pick_tile(tm_s, 64)
    mod_sc = mod_s.reshape(n_s // tc_s, tc_s, 6 * D)
    y_s = _combine(dest_s, ys, wt_s, base_s, mod_sc, True, tc_s, 1)

    st = lambda a: a[None]
    return (y_p.reshape(bsz, t, D), y_s.reshape(db, ts, D),
            st(ka_p.reshape(bsz, t, AH, AD)), st(va_p.reshape(bsz, t, AH, AD)),
            st(c_p), st(nn_p), st(m_p[:, :, 0]),
            st(ka_s.reshape(db, ts, AH, AD)), st(va_s.reshape(db, ts, AH, AD)),
            st(c_sm), st(nn_s), st(m_sm[:, :, 0]))
```

```python
import functools

import jax
import jax.numpy as jnp
from jax import lax
from jax.experimental import pallas as pl
from jax.experimental.pallas import tpu as pltpu

F32, BF16, I32 = jnp.float32, jnp.bfloat16, jnp.int32

D = 1024
MH, MD = 4, 128
AH, AD = 8, 64
MW, AW = MH * MD, AH * AD
MCHUNK = 128
BLK = 256
TOPB = 3
NE, TOPK, DE, DSH = 256, 8, 256, 256
ROUTE_SCALE = 2.5
GROUP = 256
EPS = 1e-6
NEG = -1e30
MASKV = -3.0e38
VMEM_LIMIT = 56 * 1024 * 1024
_OFF = {}
_o = 0
for _n, _w in (("mq", MW), ("mk", MW), ("mv", MW), ("mo", MW), ("mi", MH), ("mf", MH),
               ("aq", AW), ("ak", AW), ("av", AW), ("ga", D), ("gb", D)):
    _OFF[_n] = (_o, _o + _w)
    _o += _w


def _cparams(*sem):
    return pltpu.CompilerParams(dimension_semantics=sem, vmem_limit_bytes=VMEM_LIMIT)


def _mm(a, b):
    return jnp.dot(a.astype(BF16), b.astype(BF16), preferred_element_type=F32)


def _mm_nt(a, b):
    return lax.dot_general(a.astype(BF16), b.astype(BF16), (((1,), (1,)), ((), ())),
                           preferred_element_type=F32)


def _mm_tn(a, b):
    return lax.dot_general(a.astype(BF16), b.astype(BF16), (((0,), (0,)), ((), ())),
                           preferred_element_type=F32)


def _split2(x):
    hi = x.astype(BF16)
    return hi, (x - hi.astype(F32)).astype(BF16)


def _mm_x3(a, b):
    ah, al = _split2(a)
    bh, bl = _split2(b)
    d = functools.partial(jnp.dot, preferred_element_type=F32)
    return d(ah, bh) + d(al, bh) + d(ah, bl)


def _mm_nt_x2(a_bf16, b):
    bh, bl = _split2(b)
    return _mm_nt(a_bf16, bh) + _mm_nt(a_bf16, bl)


def _rms(x, g):
    return x * lax.rsqrt(jnp.mean(x * x, axis=-1, keepdims=True) + EPS) * g


def _silu(x):
    return x * jax.nn.sigmoid(x)


def _logsig(x):
    return jnp.minimum(x, 0.0) - jnp.log1p(jnp.exp(-jnp.abs(x)))


def _ada_kernel(c_ref, w_ref, b_ref, o_ref):
    o_ref[...] = _mm_x3(_silu(c_ref[...]), w_ref[...]) + b_ref[...]


def _ada(c_all, w_ada, b_ada):
    r = c_all.shape[0]
    return pl.pallas_call(
        _ada_kernel, grid=(6,),
        in_specs=[pl.BlockSpec((r, D), lambda j: (0, 0)),
                  pl.BlockSpec((D, D), lambda j: (0, j)),
                  pl.BlockSpec((1, D), lambda j: (0, j))],
        out_specs=pl.BlockSpec((r, D), lambda j: (0, j)),
        out_shape=jax.ShapeDtypeStruct((r, 6 * D), F32),
        compiler_params=_cparams("parallel"), name="ada")(c_all, w_ada, b_ada.reshape(1, 6 * D))


def _mod_spec(per_token, tm, tiles_per_batch, chunk):
    if per_token:
        return pl.BlockSpec((1, tm, D), lambda i, *_: (i, 0, chunk))
    return pl.BlockSpec((1, 1, D), lambda i, *_: (i // tiles_per_batch, 0, chunk))


def _group_ms(x, bd):
    hi, lo = _split2(x * x)
    d = functools.partial(jnp.dot, preferred_element_type=F32)
    return d(hi, bd) + d(lo, bd)


def _inproj_kernel(x_ref, sh_ref, sc_ref, g1_ref, wm_ref, wg_ref, gq_ref, gk_ref, bd_ref,
                   mq_ref, mk_ref, mv_ref, gt_ref, qa_ref, ka_ref, va_ref, *attn_refs):
    h = _rms(x_ref[...], g1_ref[...]) * (1.0 + sc_ref[0]) + sh_ref[0]
    hb = h.astype(BF16)
    d = functools.partial(jnp.dot, preferred_element_type=F32)
    mq_ref[...] = d(hb, wm_ref[:, 0:MW]).astype(BF16)
    mk_ref[...] = (d(hb, wm_ref[:, MW:2 * MW]) * (MD ** -0.5)).astype(BF16)
    mv_ref[...] = d(hb, wm_ref[:, 2 * MW:3 * MW]).astype(BF16)
    o = 3 * MW
    aq = d(hb, wm_ref[:, o:o + AW])
    ak = d(hb, wm_ref[:, o + AW:o + 2 * AW])
    av = d(hb, wm_ref[:, o + 2 * AW:o + 3 * AW])
    bd = bd_ref[...]
    qa = aq * lax.rsqrt(_group_ms(aq, bd) + EPS) * gq_ref[...]
    ka = ak * lax.rsqrt(_group_ms(ak, bd) + EPS) * gk_ref[...]
    qa_ref[...] = qa.astype(BF16)
    ka_ref[...] = ka
    va_ref[...] = av
    gt_ref[...] = d(hb, wg_ref[...])[:, 0:2 * MH]
    if attn_refs:
        kb_ref, vt_ref = attn_refs
        for r in range(vt_ref.shape[0]):
            kb_ref[r] = ka[r * BLK:(r + 1) * BLK, :].astype(BF16)
            vt_ref[r] = av[r * BLK:(r + 1) * BLK, :].T.astype(BF16)


def _inproj(x2, mod3, per_token, tm, tiles_per_batch, g1, wm, wg, gq, gk, bd, attn_layouts):
    n = x2.shape[0]
    row = lambda w: pl.BlockSpec((tm, w), lambda i: (i, 0))
    full = lambda a: pl.BlockSpec(a.shape, lambda i: (0,) * a.ndim)
    sds = jax.ShapeDtypeStruct
    out_specs = [row(MW), row(MW), row(MW), row(2 * MH), row(AW), row(AW), row(AW)]
    out_shape = [sds((n, MW), BF16), sds((n, MW), BF16), sds((n, MW), BF16), sds((n, 2 * MH), F32),
                 sds((n, AW), BF16), sds((n, AW), F32), sds((n, AW), F32)]
    if attn_layouts:
        out_specs += [pl.BlockSpec((tm // BLK, BLK, AW), lambda i: (i, 0, 0)),
                      pl.BlockSpec((tm // BLK, AW, BLK), lambda i: (i, 0, 0))]
        out_shape += [sds((n // BLK, BLK, AW), BF16), sds((n // BLK, AW, BLK), BF16)]
    return pl.pallas_call(
        _inproj_kernel, grid=(n // tm,),
        in_specs=[row(D), _mod_spec(per_token, tm, tiles_per_batch, 0),
                  _mod_spec(per_token, tm, tiles_per_batch, 1),
                  full(g1), full(wm), full(wg), full(gq), full(gk), full(bd)],
        out_specs=out_specs, out_shape=out_shape,
        compiler_params=_cparams("parallel"), name="inproj")(x2, mod3, mod3, g1, wm, wg, gq, gk, bd)


def _mlstm_kernel(q_ref, k_ref, v_ref, gr_ref, gc_ref, bgc_ref, bgr_ref, c0_ref, n0_ref, m0_ref,
                  h_ref, cn_ref, nn_ref, mn_ref, c_s, n_s, m_s, *, bb, t_valid):
    L = MCHUNK
    c = pl.program_id(1)

    @pl.when(c == 0)
    def _init():
        c_s[...] = c0_ref[...]
        n_s[...] = n0_ref[...]
        m_s[...] = m0_ref[...]

    rowi = lax.broadcasted_iota(I32, (L, L), 0)
    coli = lax.broadcasted_iota(I32, (L, L), 1)
    tri = rowi >= coli
    ok_r = (c * L + lax.broadcasted_iota(I32, (1, L), 1)) < t_valid
    ok_c = (c * L + lax.broadcasted_iota(I32, (L, 1), 0)) < t_valid
    for b in range(bb):
        g_r = gr_ref[b] + bgc_ref[...]
        g_c = gc_ref[b] + bgr_ref[...]
        for h in range(MH):
            li_r = jnp.where(ok_r, g_r[h:h + 1, :], NEG)
            lf_r = jnp.where(ok_r, _logsig(g_r[MH + h:MH + h + 1, :]), 0.0)
            li_c = jnp.where(ok_c, g_c[:, h:h + 1], NEG)
            lf_c = jnp.where(ok_c, _logsig(g_c[:, MH + h:MH + h + 1]), 0.0)
            b_c = jnp.sum(jnp.where(tri, lf_r, 0.0), axis=1, keepdims=True)
            b_r = jnp.sum(jnp.where(rowi <= coli, lf_c, 0.0), axis=0, keepdims=True)
            q = q_ref[b, :, h * MD:(h + 1) * MD]
            k = k_ref[b, :, h * MD:(h + 1) * MD]
            v = v_ref[b, :, h * MD:(h + 1) * MD]
            cm = c_s[b, h]
            nv = n_s[b, h:h + 1, :]
            m_prev = m_s[b, h:h + 1, 0:1]
            inter = m_prev + b_c
            intra = jnp.where(tri, li_r + b_c - b_r, NEG)
            m_t = jnp.maximum(inter, jnp.max(intra, axis=1, keepdims=True))
            w_inter = jnp.exp(inter - m_t)
            s = _mm_nt(q, k) * jnp.exp(intra - m_t)
            num = w_inter * _mm_nt(q, cm) + _mm(s, v)
            qn = jnp.sum(q.astype(F32) * nv, axis=1, keepdims=True)
            den = w_inter * qn + jnp.sum(s, axis=1, keepdims=True)
            h_ref[b, :, h * MD:(h + 1) * MD] = num / jnp.maximum(jnp.abs(den), jnp.exp(-m_t))
            m_last = m_t[L - 1:L, :]
            b_last = b_c[L - 1:L, :]
            w_c = jnp.exp(m_prev + b_last - m_last)
            w_s = jnp.exp(li_c + b_last - b_c - m_last)
            c_s[b, h] = w_c * cm + _mm_tn(v.astype(F32) * w_s, k)
            n_s[b, h:h + 1, :] = w_c * nv + jnp.sum(k.astype(F32) * w_s, axis=0, keepdims=True)
            m_s[b, h:h + 1, :] = jnp.broadcast_to(m_last, (1, MD))

    @pl.when(c == pl.num_programs(1) - 1)
    def _fin():
        cn_ref[...] = c_s[...]
        nn_ref[...] = n_s[...]
        mn_ref[...] = m_s[...]


def _mlstm(q, k, v, g_row, g_col, b_gates, c0, n0, m0, bb, t_valid):
    bsz, t, _ = q.shape
    L = MCHUNK
    seq = pl.BlockSpec((bb, L, MW), lambda g, c: (g, c, 0))
    st4 = pl.BlockSpec((bb, MH, MD, MD), lambda g, c: (g, 0, 0, 0))
    st3 = pl.BlockSpec((bb, MH, MD), lambda g, c: (g, 0, 0))
    sds = jax.ShapeDtypeStruct
    return pl.pallas_call(
        functools.partial(_mlstm_kernel, bb=bb, t_valid=t_valid),
        grid=(bsz // bb, t // L),
        in_specs=[seq, seq, seq,
                  pl.BlockSpec((bb, 2 * MH, L), lambda g, c: (g, 0, c)),
                  pl.BlockSpec((bb, L, 2 * MH), lambda g, c: (g, c, 0)),
                  pl.BlockSpec((2 * MH, 1), lambda g, c: (0, 0)),
                  pl.BlockSpec((1, 2 * MH), lambda g, c: (0, 0)),
                  st4, st3, st3],
        out_specs=[seq, st4, st3, st3],
        out_shape=[sds((bsz, t, MW), F32), sds((bsz, MH, MD, MD), F32),
                   sds((bsz, MH, MD), F32), sds((bsz, MH, MD), F32)],
        scratch_shapes=[pltpu.VMEM((bb, MH, MD, MD), F32), pltpu.VMEM((bb, MH, MD), F32),
                        pltpu.VMEM((bb, MH, MD), F32)],
        compiler_params=_cparams("parallel", "arbitrary"), name="mlstm")(
            q, k, v, g_row, g_col, b_gates.reshape(2 * MH, 1), b_gates.reshape(1, 2 * MH), c0, n0, m0)


def _kmean_kernel(k_ref, o_ref):
    for r in range(o_ref.shape[0]):
        o_ref[r:r + 1, :] = jnp.mean(k_ref[r * BLK:(r + 1) * BLK, :], axis=0, keepdims=True)


def _kmean(ka):
    n = ka.shape[0]
    nblk = n // BLK
    r = 8 if nblk % 8 == 0 else nblk
    return pl.pallas_call(
        _kmean_kernel, grid=(nblk // r,),
        in_specs=[pl.BlockSpec((r * BLK, AW), lambda i: (i, 0))],
        out_specs=pl.BlockSpec((r, AW), lambda i: (i, 0)),
        out_shape=jax.ShapeDtypeStruct((nblk, AW), F32),
        compiler_params=_cparams("parallel"), name="kmean")(ka)


def _select_blocks(gate, own, n_sel):
    nb = gate.shape[1]
    col = lax.broadcasted_iota(I32, gate.shape, 1).astype(F32)
    ownf = jnp.asarray(own, F32) if not isinstance(own, int) else float(own)
    g = jnp.where(col < ownf, gate, NEG)
    sel = jnp.zeros(gate.shape, F32)
    for _ in range(n_sel):
        mx = jnp.max(g, axis=1, keepdims=True)
        idx = jnp.min(jnp.where(g == mx, col, float(nb)), axis=1, keepdims=True)
        hit = col == idx
        sel = jnp.where(hit & (idx < ownf), 1.0, sel)
        g = jnp.where(hit, MASKV, g)
    return sel


def _select_blocks_t(gate_t, own, n_sel):
    nb = gate_t.shape[0]
    row = lax.broadcasted_iota(I32, gate_t.shape, 0).astype(F32)
    g = jnp.where(row < own, gate_t, NEG)
    sel = jnp.zeros(gate_t.shape, F32)
    for _ in range(n_sel):
        mx = jnp.max(g, axis=0, keepdims=True)
        idx = jnp.min(jnp.where(g == mx, row, float(nb)), axis=0, keepdims=True)
        hit = row == idx
        sel = jnp.where(hit & (idx < own), 1.0, sel)
        g = jnp.where(hit, MASKV, g)
    return sel


def _moba_p_kernel(slope_ref, q_ref, k_ref, vt_ref, km_ref, o_ref, a_s, s_s, bias_s, qs_s, m_s, l_s, acc_s):
    b = pl.program_id(0)
    i = pl.program_id(1)
    nb = km_ref.shape[1]
    rel_t = lax.broadcasted_iota(I32, (BLK, BLK), 1) - lax.broadcasted_iota(I32, (BLK, BLK), 0)

    @pl.when((b == 0) & (i == 0))
    def _alibi():
        relf = rel_t.astype(F32)
        for h in range(AH):
            a_s[h] = slope_ref[h] * relf

    causal = rel_t >= 0
    own = i.astype(F32)
    for h in range(AH):
        hs = slice(h * AD, (h + 1) * AD)
        q = q_ref[0, :, hs]
        kmh, kml = _split2(km_ref[0, :, hs])
        gate_t = _mm_nt(kmh, q) + _mm_nt(kml, q)
        sel = _select_blocks_t(gate_t, own, min(TOPB, nb))
        bias_s[h] = jnp.where(sel > 0.5, 0.0, MASKV)
        qs = (q.astype(F32) * (AD ** -0.5)).astype(BF16)
        qs_s[h] = qs
        s_s[h] = jnp.where(causal, _mm_nt(k_ref[0, i, :, hs], qs) - a_s[h], MASKV)
    for h in range(AH):
        hs = slice(h * AD, (h + 1) * AD)
        m = jnp.max(s_s[h], axis=0, keepdims=True)
        p = jnp.exp(s_s[h] - m)
        m_s[h] = m
        l_s[h] = jnp.sum(p, axis=0, keepdims=True)
        acc_s[h] = _mm(vt_ref[0, i, hs, :], p)

    def body(j, c):
        dj = ((i - j) * BLK).astype(F32)
        for h in range(AH):
            hs = slice(h * AD, (h + 1) * AD)
            rowterm = bias_s[h, pl.ds(j, 1), :] - slope_ref[h] * dj
            s_s[h] = (_mm_nt(k_ref[0, j, :, hs], qs_s[h]) - a_s[h]) + rowterm
        for h in range(AH):
            hs = slice(h * AD, (h + 1) * AD)
            m = m_s[h]
            m_new = jnp.maximum(m, jnp.max(s_s[h], axis=0, keepdims=True))
            p = jnp.exp(s_s[h] - m_new)
            alpha = jnp.exp(m - m_new)
            l_s[h] = alpha * l_s[h] + jnp.sum(p, axis=0, keepdims=True)
            acc_s[h] = alpha * acc_s[h] + _mm(vt_ref[0, j, hs, :], p)
            m_s[h] = m_new
        return c

    lax.fori_loop(0, i, body, 0)
    out_t = jnp.concatenate([acc_s[h] / l_s[h] for h in range(AH)], axis=0)
    o_ref[0] = out_t.T


def _moba_prompt(q, kb, vt, km, slopes):
    bsz, t, _ = q.shape
    nb = t // BLK
    gs = pltpu.PrefetchScalarGridSpec(
        num_scalar_prefetch=1, grid=(bsz, nb),
        in_specs=[pl.BlockSpec((1, BLK, AW), lambda b, i, s: (b, i, 0)),
                  pl.BlockSpec((1, nb, BLK, AW), lambda b, i, s: (b, 0, 0, 0)),
                  pl.BlockSpec((1, nb, AW, BLK), lambda b, i, s: (b, 0, 0, 0)),
                  pl.BlockSpec((1, nb, AW), lambda b, i, s: (b, 0, 0))],
        out_specs=pl.BlockSpec((1, BLK, AW), lambda b, i, s: (b, i, 0)),
        scratch_shapes=[pltpu.VMEM((AH, BLK, BLK), F32), pltpu.VMEM((AH, BLK, BLK), F32),
                        pltpu.VMEM((AH, nb, BLK), F32),
                        pltpu.VMEM((AH, BLK, AD), BF16), pltpu.VMEM((AH, 1, BLK), F32),
                        pltpu.VMEM((AH, 1, BLK), F32), pltpu.VMEM((AH, AD, BLK), F32)])
    return pl.pallas_call(
        _moba_p_kernel, grid_spec=gs,
        out_shape=jax.ShapeDtypeStruct((bsz, t, AW), F32),
        compiler_params=_cparams("arbitrary", "arbitrary"), name="moba_p")(slopes, q, kb, vt, km)


def _moba_s_kernel(pt_ref, *refs, ppb, page, nbp, past, rq):
    k_pages = refs[0:ppb]
    v_pages = refs[ppb:2 * ppb]
    q_ref, kn_ref, vn_ref, o_ref, k_s, v_s, km_s = refs[2 * ppb:]
    b = pl.program_id(0)
    j = pl.program_id(1)
    lk = (nbp + 1) * BLK

    @pl.when((b == 0) & (j == 0))
    def _zero():
        k_s[past:lk, :] = jnp.zeros((BLK, AW), BF16)
        v_s[past:lk, :] = jnp.zeros((BLK, AW), BF16)
        km_s[...] = jnp.zeros(km_s.shape, F32)

    ksum = jnp.zeros((1, AW), F32)
    for p in range(ppb):
        kp = k_pages[p][0]
        off = pl.multiple_of(j * BLK + p * page, page)
        k_s[pl.ds(off, page), :] = kp
        v_s[pl.ds(off, page), :] = v_pages[p][0]
        ksum = ksum + jnp.sum(kp.astype(F32), axis=0, keepdims=True)
    km_s[pl.ds(j, 1), :] = ksum * (1.0 / BLK)

    @pl.when(j == nbp - 1)
    def _attend():
        k_s[past:past + rq, :] = kn_ref[0].astype(BF16)
        v_s[past:past + rq, :] = vn_ref[0].astype(BF16)
        rows = AH * rq
        q8 = q_ref[0]
        qt = jnp.concatenate([q8] * AH, axis=0)
        rowh = lax.broadcasted_iota(I32, (rows, AW), 0) // rq
        laneh = lax.broadcasted_iota(I32, (rows, AW), 1) // AD
        qe = jnp.where(rowh == laneh, qt, 0.0)
        qe_b = qe.astype(BF16)
        gate = _mm_nt_x2(qe_b, km_s[...])
        sel = _select_blocks(gate, nbp, min(TOPB, nbp + 1))
        qs = (qe * (AD ** -0.5)).astype(BF16)
        hrow = lax.broadcasted_iota(I32, (rows, 1), 0) // rq
        slope = jnp.zeros((rows, 1), F32)
        for hh in range(AH):
            slope = jnp.where(hrow == hh, 2.0 ** (-8.0 * (hh + 1) / AH), slope)
        rq_pos = past + lax.broadcasted_iota(I32, (rows, BLK), 0) % rq
        lane = lax.broadcasted_iota(I32, (rows, BLK), 1)
        segs = []
        m = jnp.full((rows, 1), NEG, F32)
        for cb in range(nbp + 1):
            s = _mm_nt(qs, k_s[cb * BLK:(cb + 1) * BLK, :])
            dist = rq_pos - (lane + cb * BLK)
            s = s - slope * dist.astype(F32)
            ok = dist >= 0
            if cb < nbp:
                ok = ok & (sel[:, cb:cb + 1] > 0.5)
            s = jnp.where(ok, s, MASKV)
            segs.append(s)
            m = jnp.maximum(m, jnp.max(s, axis=1, keepdims=True))
        l = jnp.zeros((rows, 1), F32)
        acc = jnp.zeros((rows, AW), F32)
        for cb in range(nbp + 1):
            p = jnp.exp(segs[cb] - m)
            l = l + jnp.sum(p, axis=1, keepdims=True)
            acc = acc + _mm(p, v_s[cb * BLK:(cb + 1) * BLK, :])
        acc = acc / l
        out = jnp.zeros((rq, AW), F32)
        lh = lax.broadcasted_iota(I32, (rq, AW), 1) // AD
        for hh in range(AH):
            out = out + jnp.where(lh == hh, acc[hh * rq:(hh + 1) * rq, :], 0.0)
        o_ref[0] = out


def _moba_sample(page_table, cache_k, cache_v, q, k_new, v_new):
    db, n_pages = page_table.shape
    page = cache_k.shape[1]
    rq = q.shape[1]
    past = n_pages * page
    assert past % BLK == 0 and BLK % page == 0
    ppb = BLK // page
    nbp = past // BLK

    def page_spec(p):
        return pl.BlockSpec((1, page, AW), lambda b, j, pt: (pt[b * n_pages + j * ppb + p], 0, 0))

    tok = pl.BlockSpec((1, rq, AW), lambda b, j, pt: (b, 0, 0))
    gs = pltpu.PrefetchScalarGridSpec(
        num_scalar_prefetch=1, grid=(db, nbp),
        in_specs=[page_spec(p) for p in range(ppb)] * 2 + [tok, tok, tok],
        out_specs=tok,
        scratch_shapes=[pltpu.VMEM(((nbp + 1) * BLK, AW), BF16), pltpu.VMEM(((nbp + 1) * BLK, AW), BF16),
                        pltpu.VMEM((max(8, -(-(nbp + 1) // 8) * 8), AW), F32)])
    return pl.pallas_call(
        functools.partial(_moba_s_kernel, ppb=ppb, page=page, nbp=nbp, past=past, rq=rq),
        grid_spec=gs, out_shape=jax.ShapeDtypeStruct((db, rq, AW), F32),
        compiler_params=_cparams("arbitrary", "arbitrary"), name="moba_s")(
            page_table.reshape(-1), *([cache_k] * ppb), *([cache_v] * ppb), q, k_new, v_new)


def _moba_sn_kernel(pt_ref, *refs, ppb, page, nbp, past, ts):
    k_pages = refs[0:ppb]
    v_pages = refs[ppb:2 * ppb]
    q_ref, kn_ref, vn_ref, o_ref, s_s, v_s, km_s = refs[2 * ppb:]
    j = pl.program_id(1)
    rows = AH * ts
    pl_rows = page * AH
    scale = AD ** -0.5
    qx = q_ref[0]
    qs = (qx * scale).astype(BF16)

    ksum = jnp.zeros((AH, AD), F32)
    for p in range(ppb):
        kp = k_pages[p][0]
        pg = j * ppb + p
        s_s[pg] = _mm_nt(qs, kp.reshape(pl_rows, AD))
        v_s[pg] = v_pages[p][0].reshape(pl_rows, AD).astype(BF16)
        ksum = ksum + jnp.sum(kp, axis=0)
    km_s[j] = ksum * (1.0 / BLK)

    @pl.when(j == nbp - 1)
    def _attend():
        rh = lax.broadcasted_iota(I32, (rows, 1), 0) // ts
        rr = lax.broadcasted_iota(I32, (rows, 1), 0) % ts
        slope = jnp.zeros((rows, 1), F32)
        for hh in range(AH):
            slope = jnp.where(rh == hh, 2.0 ** (-8.0 * (hh + 1) / AH), slope)
        ncol = nbp * AH
        gate = _mm_nt_x2(qx.astype(BF16), km_s[...].reshape(ncol, AD))
        gcol = lax.broadcasted_iota(I32, (rows, ncol), 1)
        colf = gcol.astype(F32)
        g = jnp.where(gcol % AH == rh, gate, MASKV)
        picks = []
        for _ in range(min(TOPB, nbp)):
            mx = jnp.max(g, axis=1, keepdims=True)
            idx = jnp.min(jnp.where(g == mx, colf, float(ncol)), axis=1, keepdims=True)
            picks.append(jnp.floor(idx * (1.0 / AH)))
            g = jnp.where(colf == idx, MASKV, g)
        lane = lax.broadcasted_iota(I32, (rows, pl_rows), 1)
        head_ok = lane % AH == rh
        kpos = (lane // AH).astype(F32)
        qpos = (past + rr).astype(F32)
        lane_n = lax.broadcasted_iota(I32, (rows, ts * AH), 1)
        dist_n = rr - lane_n // AH
        s_new = _mm_nt(qs, kn_ref[0]) - slope * dist_n.astype(F32)
        s_new = jnp.where((lane_n % AH == rh) & (dist_n >= 0), s_new, MASKV)
        m = jnp.max(s_new, axis=1, keepdims=True)
        n_pg = nbp * ppb
        for pg in range(n_pg):
            blk = float(pg // ppb)
            chosen = picks[0] == blk
            for c in picks[1:]:
                chosen = chosen | (c == blk)
            s = s_s[pg] - slope * (qpos - (kpos + float(pg * page)))
            s = jnp.where(head_ok & chosen, s, MASKV)
            s_s[pg] = s
            m = jnp.maximum(m, jnp.max(s, axis=1, keepdims=True))
        p_new = jnp.exp(s_new - m)
        l = jnp.sum(p_new, axis=1, keepdims=True)
        acc = _mm(p_new, vn_ref[0])
        for pg in range(n_pg):
            p = jnp.exp(s_s[pg] - m)
            l = l + jnp.sum(p, axis=1, keepdims=True)
            acc = acc + _mm(p, v_s[pg])
        o_ref[0] = acc / l


def _moba_sample_native(page_table, cache_k, cache_v, qx, k_new, v_new, ts):
    db, n_pages = page_table.shape
    page = cache_k.shape[1]
    past = n_pages * page
    assert past % BLK == 0 and BLK % page == 0
    ppb = BLK // page
    nbp = past // BLK
    rows = AH * ts

    def page_spec(p):
        return pl.BlockSpec((1, page, AH, AD), lambda b, j, pt: (pt[b * n_pages + j * ppb + p], 0, 0, 0))

    tok = pl.BlockSpec((1, rows, AD), lambda b, j, pt: (b, 0, 0))
    gs = pltpu.PrefetchScalarGridSpec(
        num_scalar_prefetch=1, grid=(db, nbp),
        in_specs=[page_spec(p) for p in range(ppb)] * 2 + [tok, tok, tok],
        out_specs=tok,
        scratch_shapes=[pltpu.VMEM((n_pages, rows, page * AH), F32), pltpu.VMEM((n_pages, page * AH, AD), BF16),
                        pltpu.VMEM((nbp, AH, AD), F32)])
    return pl.pallas_call(
        functools.partial(_moba_sn_kernel, ppb=ppb, page=page, nbp=nbp, past=past, ts=ts),
        grid_spec=gs, out_shape=jax.ShapeDtypeStruct((db, rows, AD), F32),
        compiler_params=_cparams("arbitrary", "arbitrary"), name="moba_s")(
            page_table.reshape(-1), *([cache_k] * ppb), *([cache_v] * ppb), qx, k_new, v_new)


def _merge_kernel(x_ref, hm_ref, ha_ref, sh1_ref, sc1_ref, gt1_ref, sh2_ref, sc2_ref, gt2_ref,
                  g1_ref, g2_ref, gmh_ref, wl_ref, wbm_ref, wba_ref, wo_ref, wr_ref, br_ref,
                  wsg_ref, wsu_ref, wsd_ref, cin_ref,
                  base_ref, h2_ref, idx_ref, wt_ref, rank_ref, cout_ref, cnt_s):
    i = pl.program_id(0)
    tm = x_ref.shape[0]

    @pl.when(i == 0)
    def _init():
        cnt_s[...] = cin_ref[...]

    x = x_ref[...]
    h1 = (_rms(x, g1_ref[...]) * (1.0 + sc1_ref[0]) + sh1_ref[0]).astype(BF16)
    d = functools.partial(jnp.dot, preferred_element_type=F32)
    mo = d(h1, wl_ref[:, 0:MW])
    ga = d(h1, wl_ref[:, MW:MW + D])
    gb = d(h1, wl_ref[:, MW + D:MW + 2 * D])
    hm = hm_ref[...]
    parts = []
    for h in range(MH):
        xh = hm[:, h * MD:(h + 1) * MD]
        parts.append(_rms(xh, gmh_ref[:, h * MD:(h + 1) * MD]))
    hmn = jnp.concatenate(parts, axis=1) * jax.nn.sigmoid(mo)
    merged = jax.nn.sigmoid(ga) * _mm(hmn, wbm_ref[...]) + jax.nn.sigmoid(gb) * _mm(ha_ref[...], wba_ref[...])
    x1 = x + gt1_ref[0] * _mm(merged, wo_ref[...])
    h2 = _rms(x1, g2_ref[...]) * (1.0 + sc2_ref[0]) + sh2_ref[0]
    h2_ref[...] = h2
    h2b = h2.astype(BF16)
    shared = _mm(_silu(d(h2b, wsg_ref[...])) * d(h2b, wsu_ref[...]), wsd_ref[...])
    base_ref[...] = x1 + gt2_ref[0] * shared
    scores = jax.nn.sigmoid(_mm_x3(h2, wr_ref[...]))
    col = lax.broadcasted_iota(I32, (tm, NE), 1).astype(F32)
    g = scores + br_ref[...]
    idxs, wts = [], []
    onehot = jnp.zeros((tm, NE), F32)
    for _ in range(TOPK):
        mx = jnp.max(g, axis=1, keepdims=True)
        idx = jnp.min(jnp.where(g == mx, col, float(NE)), axis=1, keepdims=True)
        hit = col == idx
        idxs.append(idx)
        wts.append(jnp.sum(jnp.where(hit, scores, 0.0), axis=1, keepdims=True))
        onehot = jnp.where(hit, 1.0, onehot)
        g = jnp.where(hit, MASKV, g)
    wsum = wts[0]
    for w in wts[1:]:
        wsum = wsum + w
    ri = lax.broadcasted_iota(I32, (tm, tm), 0)
    ci = lax.broadcasted_iota(I32, (tm, tm), 1)
    before = jnp.where(ci < ri, 1.0, 0.0).astype(BF16)
    pref = cnt_s[...] + d(before, onehot.astype(BF16))
    lane8 = lax.broadcasted_iota(I32, (tm, TOPK), 1)
    idx_o = jnp.zeros((tm, TOPK), F32)
    wt_o = jnp.zeros((tm, TOPK), F32)
    rk_o = jnp.zeros((tm, TOPK), F32)
    for kk in range(TOPK):
        rk = jnp.sum(jnp.where(col == idxs[kk], pref, 0.0), axis=1, keepdims=True)
        idx_o = jnp.where(lane8 == kk, idxs[kk], idx_o)
        wt_o = jnp.where(lane8 == kk, wts[kk] / wsum * ROUTE_SCALE, wt_o)
        rk_o = jnp.where(lane8 == kk, rk, rk_o)
    idx_ref[...] = idx_o.astype(I32)
    wt_ref[...] = wt_o
    rank_ref[...] = rk_o.astype(I32)
    cnt_s[...] = cnt_s[...] + jnp.sum(onehot, axis=0, keepdims=True)
    cout_ref[...] = cnt_s[...]


def _merge(x2, hm, ha, mod3, per_token, tm, tiles_per_batch, weights, cnt_in):
    n = x2.shape[0]
    row = lambda w: pl.BlockSpec((tm, w), lambda i: (i, 0))
    full = lambda a: pl.BlockSpec(a.shape, lambda i: (0,) * a.ndim)
    mods = [_mod_spec(per_token, tm, tiles_per_batch, c) for c in range(6)]
    sds = jax.ShapeDtypeStruct
    return pl.pallas_call(
        _merge_kernel, grid=(n // tm,),
        in_specs=[row(D), row(MW), row(AW)] + mods + [full(w) for w in weights] + [full(cnt_in)],
        out_specs=[row(D), row(D), row(TOPK), row(TOPK), row(TOPK), full(cnt_in)],
        out_shape=[sds((n, D), F32), sds((n, D), F32), sds((n, TOPK), I32), sds((n, TOPK), F32),
                   sds((n, TOPK), I32), sds((1, NE), F32)],
        scratch_shapes=[pltpu.VMEM((1, NE), F32)],
        compiler_params=_cparams("arbitrary"), name="merge")(
            x2, hm, ha, *([mod3] * 6), *weights, cnt_in)


def _dest_kernel(idx_ref, rank_ref, pstart_ref, o_ref):
    tm = idx_ref.shape[0]
    col = lax.broadcasted_iota(I32, (tm, NE), 1)
    lane8 = lax.broadcasted_iota(I32, (tm, TOPK), 1)
    idx = idx_ref[...]
    out = rank_ref[...]
    for kk in range(TOPK):
        start = jnp.sum(jnp.where(col == idx[:, kk:kk + 1], pstart_ref[...], 0.0), axis=1, keepdims=True)
        out = out + jnp.where(lane8 == kk, start.astype(I32), 0)
    o_ref[...] = out


def _dest(idx, rank, pstart):
    n = idx.shape[0]
    tm = _pick_tile(n, 512)
    row = pl.BlockSpec((tm, TOPK), lambda i: (i, 0))
    return pl.pallas_call(
        _dest_kernel, grid=(n // tm,),
        in_specs=[row, row, pl.BlockSpec((1, NE), lambda i: (0, 0))],
        out_specs=row, out_shape=jax.ShapeDtypeStruct((n, TOPK), I32),
        compiler_params=_cparams("parallel"), name="dest")(idx, rank, pstart.astype(F32).reshape(1, NE))


def _row_copy(src, dst, sem):
    return pltpu.make_async_copy(src, dst, sem)


def _dispatch_kernel(pend_ref, cnt_ref, dest_ref, h_ref, xs_ref, zbuf, sem):
    step = pl.program_id(0)
    td = h_ref.shape[0]

    @pl.when(step == 0)
    def _zero_tails():
        zbuf[...] = jnp.zeros(zbuf.shape, F32)

        def tail(e):
            return _row_copy(zbuf, xs_ref.at[pl.ds(pl.multiple_of(pend_ref[e] - GROUP, GROUP), GROUP)], sem)

        def start(e, c):
            @pl.when(cnt_ref[e] > 0)
            def _():
                tail(e).start()
            return c

        def wait(e, c):
            @pl.when(cnt_ref[e] > 0)
            def _():
                tail(e).wait()
            return c

        lax.fori_loop(0, NE, start, 0)
        lax.fori_loop(0, NE, wait, 0)

    def copy(t, kk):
        dst = dest_ref[0, 0, t * TOPK + kk]
        return _row_copy(h_ref.at[pl.ds(t, 1)], xs_ref.at[pl.ds(dst, 1)], sem)

    def start(t, c):
        for kk in range(TOPK):
            copy(t, kk).start(priority=kk % 2)
        return c

    def wait(t, c):
        for kk in range(TOPK):
            copy(t, kk).wait()
        return c

    lax.fori_loop(0, td, start, 0)
    lax.fori_loop(0, td, wait, 0)


def _dispatch(pend, cnt, dest, h2, n_rows, td):
    n = h2.shape[0]
    gs = pltpu.PrefetchScalarGridSpec(
        num_scalar_prefetch=2, grid=(n // td,),
        in_specs=[pl.BlockSpec((1, 1, td * TOPK), lambda i, *_: (i, 0, 0), memory_space=pltpu.SMEM),
                  pl.BlockSpec((td, D), lambda i, *_: (i, 0))],
        out_specs=pl.BlockSpec(memory_space=pl.ANY),
        scratch_shapes=[pltpu.VMEM((GROUP, D), F32), pltpu.SemaphoreType.DMA(())])
    return pl.pallas_call(
        _dispatch_kernel, grid_spec=gs, out_shape=jax.ShapeDtypeStruct((n_rows, D), F32),
        compiler_params=_cparams("arbitrary"), name="dispatch")(
            pend, cnt, dest.reshape(n // td, 1, td * TOPK), h2)


def _experts_kernel(be_ref, nu_ref, x_ref, wg_ref, wu_ref, wd_ref, o_ref, wg_s, wu_s, wd_s):
    b = pl.program_id(0)

    @pl.when(b < nu_ref[0])
    def _run():
        @pl.when((b == 0) | (be_ref[b] != be_ref[jnp.maximum(b - 1, 0)]))
        def _load():
            wg_s[...] = wg_ref[0].astype(BF16)
            wu_s[...] = wu_ref[0].astype(BF16)
            wd_s[...] = wd_ref[0].astype(BF16)

        d = functools.partial(jnp.dot, preferred_element_type=F32)
        x = x_ref[...].astype(BF16)
        hb = _silu(d(x, wg_s[...])) * d(x, wu_s[...])
        o_ref[...] = d(hb.astype(BF16), wd_s[...])


def _experts(blk_e, n_used, xs, w_gate, w_up, w_down):
    n_rows = xs.shape[0]
    rows = lambda b, be, nu: (jnp.minimum(b, nu[0] - 1), 0)
    gs = pltpu.PrefetchScalarGridSpec(
        num_scalar_prefetch=2, grid=(n_rows // GROUP,),
        in_specs=[pl.BlockSpec((GROUP, D), rows),
                  pl.BlockSpec((1, D, DE), lambda b, be, nu: (be[b], 0, 0)),
                  pl.BlockSpec((1, D, DE), lambda b, be, nu: (be[b], 0, 0)),
                  pl.BlockSpec((1, DE, D), lambda b, be, nu: (be[b], 0, 0))],
        out_specs=pl.BlockSpec((GROUP, D), rows),
        scratch_shapes=[pltpu.VMEM((D, DE), BF16), pltpu.VMEM((D, DE), BF16), pltpu.VMEM((DE, D), BF16)])
    return pl.pallas_call(
        _experts_kernel, grid_spec=gs, out_shape=jax.ShapeDtypeStruct((n_rows, D), F32),
        compiler_params=_cparams("arbitrary"), name="experts")(blk_e, n_used, xs, w_gate, w_up, w_down)


def _combine_kernel(dest_ref, ys_ref, wt_ref, base_ref, gt2_ref, o_ref, buf, sem):
    tc = base_ref.shape[0]

    def copy(t, kk):
        src = dest_ref[0, 0, t * TOPK + kk]
        return _row_copy(ys_ref.at[pl.ds(src, 1)], buf.at[kk, pl.ds(t, 1)], sem)

    def start(t, c):
        for kk in range(TOPK):
            copy(t, kk).start(priority=kk % 2)
        return c

    def wait(t, c):
        for kk in range(TOPK):
            copy(t, kk).wait()
        return c

    lax.fori_loop(0, tc, start, 0)
    lax.fori_loop(0, tc, wait, 0)
    wt = wt_ref[...]
    acc = jnp.zeros((tc, D), F32)
    for kk in range(TOPK):
        acc = acc + wt[:, kk:kk + 1] * buf[kk]
    o_ref[...] = base_ref[...] + gt2_ref[0] * acc


def _combine(dest, ys, wt, base, mod3, per_token, tc, tiles_per_batch):
    n = base.shape[0]
    return pl.pallas_call(
        _combine_kernel, grid=(n // tc,),
        in_specs=[pl.BlockSpec((1, 1, tc * TOPK), lambda i: (i, 0, 0), memory_space=pltpu.SMEM),
                  pl.BlockSpec(memory_space=pl.ANY),
                  pl.BlockSpec((tc, TOPK), lambda i: (i, 0)),
                  pl.BlockSpec((tc, D), lambda i: (i, 0)),
                  _mod_spec(per_token, tc, tiles_per_batch, 5)],
        out_specs=pl.BlockSpec((tc, D), lambda i: (i, 0)),
        out_shape=jax.ShapeDtypeStruct((n, D), F32),
        scratch_shapes=[pltpu.VMEM((TOPK, tc, D), F32), pltpu.SemaphoreType.DMA(())],
        compiler_params=_cparams("arbitrary"), name="combine")(
            dest.reshape(n // tc, 1, tc * TOPK), ys, wt, base, mod3)


def _pick_tile(n, pref):
    t = pref
    while n % t:
        t //= 2
    return t


def kernel(x_prompt, x_sample, cache_k, cache_v, page_table, state_mlstm_C, state_mlstm_n, state_mlstm_m,
           c_prompt, c_sample, w_ada, b_ada, g_norm1, w_in, b_gates, g_q, g_k, g_mh, w_br_m, w_br_a, w_out,
           g_norm2, w_router, b_router, w_gate, w_up, w_down, ws_gate, ws_up, ws_down):
    depth = w_ada.shape[0]
    assert depth == 1
    bsz, t, _ = x_prompt.shape
    db, ts, _ = x_sample.shape
    n_p, n_s = bsz * t, db * ts
    assert t % BLK == 0 and t % MCHUNK == 0 and ts <= 8
    l = 0

    nc = bsz + db
    ncp = -(-nc // 8) * 8
    c_all = jnp.pad(jnp.concatenate([c_prompt, c_sample], axis=0), ((0, ncp - nc), (0, 0)))
    mod = _ada(c_all, w_ada[l], b_ada[l])
    mod_p = mod[:bsz].reshape(bsz, 1, 6 * D)
    tm_s = _pick_tile(n_s, 256)
    mod_s = jnp.repeat(mod[bsz:nc], ts, axis=0).reshape(n_s // tm_s, tm_s, 6 * D)

    wi = w_in[l]
    sl = lambda name: wi[:, _OFF[name][0]:_OFF[name][1]]
    wm = jnp.concatenate([sl("mq"), sl("mk"), sl("mv"), sl("aq"), sl("ak"), sl("av")], axis=1).astype(BF16)
    wg = jnp.pad(jnp.concatenate([sl("mi"), sl("mf")], axis=1), ((0, 0), (0, 128 - 2 * MH))).astype(BF16)
    wl = jnp.concatenate([sl("mo"), sl("ga"), sl("gb")], axis=1).astype(BF16)
    g1 = g_norm1[l].reshape(1, D)
    g2 = g_norm2[l].reshape(1, D)
    gq = jnp.tile(g_q[l], AH).reshape(1, AW)
    gk = jnp.tile(g_k[l], AH).reshape(1, AW)
    gmh = g_mh[l].reshape(1, MW)
    grp = jnp.arange(AW) // AD
    bd = jnp.where(grp[:, None] == grp[None, :], 1.0 / AD, 0.0).astype(BF16)
    merge_w = (g1, g2, gmh, wl, w_br_m[l].astype(BF16), w_br_a[l].astype(BF16), w_out[l].astype(BF16),
               w_router[l], b_router[l].reshape(1, NE), ws_gate[l].astype(BF16), ws_up[l].astype(BF16),
               ws_down[l].astype(BF16))
    slopes = 2.0 ** (-8.0 * jnp.arange(1, AH + 1, dtype=F32) / AH)

    tm_p = _pick_tile(t, 512)
    xp2 = x_prompt.reshape(n_p, D)
    xs2 = x_sample.reshape(n_s, D)
    mq_p, mk_p, mv_p, gt_p, qa_p, ka_p, va_p, kb_p, vt_p = _inproj(
        xp2, mod_p, False, tm_p, t // tm_p, g1, wm, wg, gq, gk, bd, True)
    mq_s, mk_s, mv_s, gt_s, qa_s, ka_s, va_s = _inproj(
        xs2, mod_s, True, tm_s, 1, g1, wm, wg, gq, gk, bd, False)

    zeros = functools.partial(jnp.zeros, dtype=F32)
    gp = gt_p.reshape(bsz, t, 2 * MH)
    hm_p, c_p, nn_p, m_p = _mlstm(
        mq_p.reshape(bsz, t, MW), mk_p.reshape(bsz, t, MW), mv_p.reshape(bsz, t, MW),
        gp.transpose(0, 2, 1), gp, b_gates[l], zeros((bsz, MH, MD, MD)), zeros((bsz, MH, MD)),
        zeros((bsz, MH, MD)), _pick_tile(bsz, 2), t)
    pad_t = lambda a: jnp.pad(a.reshape(db, ts, -1), ((0, 0), (0, MCHUNK - ts), (0, 0)))
    gs_ = pad_t(gt_s)
    m0 = jnp.broadcast_to(state_mlstm_m[l].astype(F32)[:, :, None], (db, MH, MD))
    hm_s, c_sm, nn_s, m_sm = _mlstm(
        pad_t(mq_s), pad_t(mk_s), pad_t(mv_s), gs_.transpose(0, 2, 1), gs_, b_gates[l],
        state_mlstm_C[l].astype(F32), state_mlstm_n[l].astype(F32), m0, _pick_tile(db, 2), ts)
    hm_s = hm_s[:, :ts].reshape(n_s, MW)

    nb = t // BLK
    km = _kmean(ka_p)
    ha_p = _moba_prompt(qa_p.reshape(bsz, t, AW), kb_p.reshape(bsz, nb, BLK, AW), vt_p.reshape(bsz, nb, AW, BLK),
                        km.reshape(bsz, nb, AW), slopes).reshape(n_p, AW)
    qx = qa_s.astype(F32).reshape(db, ts, AH, AD).transpose(0, 2, 1, 3).reshape(db, AH * ts, AD)
    ha_s = _moba_sample_native(page_table, cache_k[l], cache_v[l], qx, ka_s.reshape(db, ts * AH, AD),
                               va_s.reshape(db, ts * AH, AD), ts)
    ha_s = ha_s.reshape(db, AH, ts, AD).transpose(0, 2, 1, 3).reshape(n_s, AW)

    tmm_p = _pick_tile(t, 256)
    base_p, h2_p, idx_p, wt_p, rk_p, cnt1 = _merge(
        xp2, hm_p.reshape(n_p, MW), ha_p, mod_p, False, tmm_p, t // tmm_p, merge_w, zeros((1, NE)))
    base_s, h2_s, idx_s, wt_s, rk_s, cnt2 = _merge(
        xs2, hm_s, ha_s, mod_s, True, tm_s, 1, merge_w, cnt1)

    n_all = n_p + n_s
    cnt = cnt2.reshape(NE).astype(I32)
    padded = (cnt + GROUP - 1) // GROUP * GROUP
    pend = jnp.cumsum(padded)
    pstart = pend - padded
    n_blocks = -(-(n_all * TOPK + NE * (GROUP - 1)) // GROUP)
    n_used = (pend[-1] // GROUP).reshape(1)
    blk = jnp.minimum(jnp.arange(n_blocks, dtype=I32), n_used[0] - 1) * GROUP
    blk_e = jnp.minimum(jnp.sum((pend[None, :] <= blk[:, None]).astype(I32), axis=1), NE - 1)

    h2 = jnp.concatenate([h2_p, h2_s], axis=0)
    dest_p = _dest(idx_p, rk_p, pstart)
    dest_s = _dest(idx_s, rk_s, pstart)
    td = _pick_tile(n_all, 128)
    xs = _dispatch(pend, cnt, jnp.concatenate([dest_p, dest_s], axis=0), h2, n_blocks * GROUP, td)
    ys = _experts(blk_e, n_used, xs, w_gate[l], w_up[l], w_down[l])
    tc_p = _pick_tile(t, 64)
    y_p = _combine(dest_p, ys, wt_p, base_p, mod_p, False, tc_p, t // tc_p)
    tc_s = _pick_tile(tm_s, 64)
    mod_sc = mod_s.reshape(n_s // tc_s, tc_s, 6 * D)
    y_s = _combine(dest_s, ys, wt_s, base_s, mod_sc, True, tc_s, 1)

    st = lambda a: a[None]
    return (y_p.reshape(bsz, t, D), y_s.reshape(db, ts, D),
            st(ka_p.reshape(bsz, t, AH, AD)), st(va_p.reshape(bsz, t, AH, AD)),
            st(c_p), st(nn_p), st(m_p[:, :, 0]),
            st(ka_s.reshape(db, ts, AH, AD)), st(va_s.reshape(db, ts, AH, AD)),
            st(c_sm), st(nn_s), st(m_sm[:, :, 0]))
```

```python
import functools

import jax
import jax.numpy as jnp
from jax import lax
from jax.experimental import pallas as pl
from jax.experimental.pallas import tpu as pltpu

F32, BF16, I32 = jnp.float32, jnp.bfloat16, jnp.int32

D = 1024
MH, MD = 4, 128
AH, AD = 8, 64
MW, AW = MH * MD, AH * AD
MCHUNK = 128
BLK = 256
TOPB = 3
NE, TOPK, DE, DSH = 256, 8, 256, 256
ROUTE_SCALE = 2.5
GROUP = 192
EPS = 1e-6
NEG = -1e30
MASKV = -3.0e38
VMEM_LIMIT = 56 * 1024 * 1024
_OFF = {}
_o = 0
for _n, _w in (("mq", MW), ("mk", MW), ("mv", MW), ("mo", MW), ("mi", MH), ("mf", MH),
               ("aq", AW), ("ak", AW), ("av", AW), ("ga", D), ("gb", D)):
    _OFF[_n] = (_o, _o + _w)
    _o += _w


def _cparams(*sem):
    return pltpu.CompilerParams(dimension_semantics=sem, vmem_limit_bytes=VMEM_LIMIT)


def _mm(a, b):
    return jnp.dot(a.astype(BF16), b.astype(BF16), preferred_element_type=F32)


def _mm_nt(a, b):
    return lax.dot_general(a.astype(BF16), b.astype(BF16), (((1,), (1,)), ((), ())),
                           preferred_element_type=F32)


def _mm_tn(a, b):
    return lax.dot_general(a.astype(BF16), b.astype(BF16), (((0,), (0,)), ((), ())),
                           preferred_element_type=F32)


def _split2(x):
    hi = x.astype(BF16)
    return hi, (x - hi.astype(F32)).astype(BF16)


def _mm_x3(a, b):
    ah, al = _split2(a)
    bh, bl = _split2(b)
    d = functools.partial(jnp.dot, preferred_element_type=F32)
    return d(ah, bh) + d(al, bh) + d(ah, bl)


def _mm_nt_x2(a_bf16, b):
    bh, bl = _split2(b)
    return _mm_nt(a_bf16, bh) + _mm_nt(a_bf16, bl)


def _rms(x, g):
    return x * lax.rsqrt(jnp.mean(x * x, axis=-1, keepdims=True) + EPS) * g


def _silu(x):
    return x * jax.nn.sigmoid(x)


def _logsig(x):
    return jnp.minimum(x, 0.0) - jnp.log1p(jnp.exp(-jnp.abs(x)))


def _ada_kernel(c_ref, w_ref, b_ref, o_ref):
    o_ref[...] = _mm_x3(_silu(c_ref[...]), w_ref[...]) + b_ref[...]


def _ada(c_all, w_ada, b_ada):
    r = c_all.shape[0]
    return pl.pallas_call(
        _ada_kernel, grid=(6,),
        in_specs=[pl.BlockSpec((r, D), lambda j: (0, 0)),
                  pl.BlockSpec((D, D), lambda j: (0, j)),
                  pl.BlockSpec((1, D), lambda j: (0, j))],
        out_specs=pl.BlockSpec((r, D), lambda j: (0, j)),
        out_shape=jax.ShapeDtypeStruct((r, 6 * D), F32),
        compiler_params=_cparams("parallel"), name="ada")(c_all, w_ada, b_ada.reshape(1, 6 * D))


def _mod_spec(per_token, tm, tiles_per_batch, chunk):
    if per_token:
        return pl.BlockSpec((1, tm, D), lambda i, *_: (i, 0, chunk))
    return pl.BlockSpec((1, 1, D), lambda i, *_: (i // tiles_per_batch, 0, chunk))


def _group_ms(x, bd):
    hi, lo = _split2(x * x)
    d = functools.partial(jnp.dot, preferred_element_type=F32)
    return d(hi, bd) + d(lo, bd)


def _inproj_kernel(x_ref, sh_ref, sc_ref, g1_ref, wm_ref, wg_ref, gq_ref, gk_ref, bd_ref,
                   mq_ref, mk_ref, mv_ref, gt_ref, qa_ref, ka_ref, va_ref, *attn_refs):
    h = _rms(x_ref[...], g1_ref[...]) * (1.0 + sc_ref[0]) + sh_ref[0]
    hb = h.astype(BF16)
    d = functools.partial(jnp.dot, preferred_element_type=F32)
    mq_ref[...] = d(hb, wm_ref[:, 0:MW]).astype(BF16)
    mk_ref[...] = (d(hb, wm_ref[:, MW:2 * MW]) * (MD ** -0.5)).astype(BF16)
    mv_ref[...] = d(hb, wm_ref[:, 2 * MW:3 * MW]).astype(BF16)
    o = 3 * MW
    aq = d(hb, wm_ref[:, o:o + AW])
    ak = d(hb, wm_ref[:, o + AW:o + 2 * AW])
    av = d(hb, wm_ref[:, o + 2 * AW:o + 3 * AW])
    bd = bd_ref[...]
    qa = aq * lax.rsqrt(_group_ms(aq, bd) + EPS) * gq_ref[...]
    ka = ak * lax.rsqrt(_group_ms(ak, bd) + EPS) * gk_ref[...]
    qa_ref[...] = qa.astype(BF16)
    ka_ref[...] = ka
    va_ref[...] = av
    gt_ref[...] = d(hb, wg_ref[...])[:, 0:2 * MH]
    if attn_refs:
        kb_ref, vt_ref = attn_refs
        for r in range(vt_ref.shape[0]):
            kb_ref[r] = ka[r * BLK:(r + 1) * BLK, :].astype(BF16)
            vt_ref[r] = av[r * BLK:(r + 1) * BLK, :].T.astype(BF16)


def _inproj(x2, mod3, per_token, tm, tiles_per_batch, g1, wm, wg, gq, gk, bd, attn_layouts):
    n = x2.shape[0]
    row = lambda w: pl.BlockSpec((tm, w), lambda i: (i, 0))
    full = lambda a: pl.BlockSpec(a.shape, lambda i: (0,) * a.ndim)
    sds = jax.ShapeDtypeStruct
    out_specs = [row(MW), row(MW), row(MW), row(2 * MH), row(AW), row(AW), row(AW)]
    out_shape = [sds((n, MW), BF16), sds((n, MW), BF16), sds((n, MW), BF16), sds((n, 2 * MH), F32),
                 sds((n, AW), BF16), sds((n, AW), F32), sds((n, AW), F32)]
    if attn_layouts:
        out_specs += [pl.BlockSpec((tm // BLK, BLK, AW), lambda i: (i, 0, 0)),
                      pl.BlockSpec((tm // BLK, AW, BLK), lambda i: (i, 0, 0))]
        out_shape += [sds((n // BLK, BLK, AW), BF16), sds((n // BLK, AW, BLK), BF16)]
    return pl.pallas_call(
        _inproj_kernel, grid=(n // tm,),
        in_specs=[row(D), _mod_spec(per_token, tm, tiles_per_batch, 0),
                  _mod_spec(per_token, tm, tiles_per_batch, 1),
                  full(g1), full(wm), full(wg), full(gq), full(gk), full(bd)],
        out_specs=out_specs, out_shape=out_shape,
        compiler_params=_cparams("parallel"), name="inproj")(x2, mod3, mod3, g1, wm, wg, gq, gk, bd)


def _mlstm_kernel(q_ref, k_ref, v_ref, gr_ref, gc_ref, bgc_ref, bgr_ref, c0_ref, n0_ref, m0_ref,
                  h_ref, cn_ref, nn_ref, mn_ref, c_s, n_s, m_s, *, bb, t_valid):
    L = MCHUNK
    c = pl.program_id(1)

    @pl.when(c == 0)
    def _init():
        c_s[...] = c0_ref[...]
        n_s[...] = n0_ref[...]
        m_s[...] = m0_ref[...]

    rowi = lax.broadcasted_iota(I32, (L, L), 0)
    coli = lax.broadcasted_iota(I32, (L, L), 1)
    tri = rowi >= coli
    ok_r = (c * L + lax.broadcasted_iota(I32, (1, L), 1)) < t_valid
    ok_c = (c * L + lax.broadcasted_iota(I32, (L, 1), 0)) < t_valid
    for b in range(bb):
        g_r = gr_ref[b] + bgc_ref[...]
        g_c = gc_ref[b] + bgr_ref[...]
        for h in range(MH):
            li_r = jnp.where(ok_r, g_r[h:h + 1, :], NEG)
            lf_r = jnp.where(ok_r, _logsig(g_r[MH + h:MH + h + 1, :]), 0.0)
            li_c = jnp.where(ok_c, g_c[:, h:h + 1], NEG)
            lf_c = jnp.where(ok_c, _logsig(g_c[:, MH + h:MH + h + 1]), 0.0)
            b_c = jnp.sum(jnp.where(tri, lf_r, 0.0), axis=1, keepdims=True)
            b_r = jnp.sum(jnp.where(rowi <= coli, lf_c, 0.0), axis=0, keepdims=True)
            q = q_ref[b, :, h * MD:(h + 1) * MD]
            k = k_ref[b, :, h * MD:(h + 1) * MD]
            v = v_ref[b, :, h * MD:(h + 1) * MD]
            cm = c_s[b, h]
            nv = n_s[b, h:h + 1, :]
            m_prev = m_s[b, h:h + 1, 0:1]
            inter = m_prev + b_c
            intra = jnp.where(tri, li_r + b_c - b_r, NEG)
            m_t = jnp.maximum(inter, jnp.max(intra, axis=1, keepdims=True))
            w_inter = jnp.exp(inter - m_t)
            s = _mm_nt(q, k) * jnp.exp(intra - m_t)
            num = w_inter * _mm_nt(q, cm) + _mm(s, v)
            qn = jnp.sum(q.astype(F32) * nv, axis=1, keepdims=True)
            den = w_inter * qn + jnp.sum(s, axis=1, keepdims=True)
            h_ref[b, :, h * MD:(h + 1) * MD] = num / jnp.maximum(jnp.abs(den), jnp.exp(-m_t))
            m_last = m_t[L - 1:L, :]
            b_last = b_c[L - 1:L, :]
            w_c = jnp.exp(m_prev + b_last - m_last)
            w_s = jnp.exp(li_c + b_last - b_c - m_last)
            c_s[b, h] = w_c * cm + _mm_tn(v.astype(F32) * w_s, k)
            n_s[b, h:h + 1, :] = w_c * nv + jnp.sum(k.astype(F32) * w_s, axis=0, keepdims=True)
            m_s[b, h:h + 1, :] = jnp.broadcast_to(m_last, (1, MD))

    @pl.when(c == pl.num_programs(1) - 1)
    def _fin():
        cn_ref[...] = c_s[...]
        nn_ref[...] = n_s[...]
        mn_ref[...] = m_s[...]


def _mlstm(q, k, v, g_row, g_col, b_gates, c0, n0, m0, bb, t_valid):
    bsz, t, _ = q.shape
    L = MCHUNK
    seq = pl.BlockSpec((bb, L, MW), lambda g, c: (g, c, 0))
    st4 = pl.BlockSpec((bb, MH, MD, MD), lambda g, c: (g, 0, 0, 0))
    st3 = pl.BlockSpec((bb, MH, MD), lambda g, c: (g, 0, 0))
    sds = jax.ShapeDtypeStruct
    return pl.pallas_call(
        functools.partial(_mlstm_kernel, bb=bb, t_valid=t_valid),
        grid=(bsz // bb, t // L),
        in_specs=[seq, seq, seq,
                  pl.BlockSpec((bb, 2 * MH, L), lambda g, c: (g, 0, c)),
                  pl.BlockSpec((bb, L, 2 * MH), lambda g, c: (g, c, 0)),
                  pl.BlockSpec((2 * MH, 1), lambda g, c: (0, 0)),
                  pl.BlockSpec((1, 2 * MH), lambda g, c: (0, 0)),
                  st4, st3, st3],
        out_specs=[seq, st4, st3, st3],
        out_shape=[sds((bsz, t, MW), F32), sds((bsz, MH, MD, MD), F32),
                   sds((bsz, MH, MD), F32), sds((bsz, MH, MD), F32)],
        scratch_shapes=[pltpu.VMEM((bb, MH, MD, MD), F32), pltpu.VMEM((bb, MH, MD), F32),
                        pltpu.VMEM((bb, MH, MD), F32)],
        compiler_params=_cparams("parallel", "arbitrary"), name="mlstm")(
            q, k, v, g_row, g_col, b_gates.reshape(2 * MH, 1), b_gates.reshape(1, 2 * MH), c0, n0, m0)


def _kmean_kernel(k_ref, o_ref):
    for r in range(o_ref.shape[0]):
        o_ref[r:r + 1, :] = jnp.mean(k_ref[r * BLK:(r + 1) * BLK, :], axis=0, keepdims=True)


def _kmean(ka):
    n = ka.shape[0]
    nblk = n // BLK
    r = 8 if nblk % 8 == 0 else nblk
    return pl.pallas_call(
        _kmean_kernel, grid=(nblk // r,),
        in_specs=[pl.BlockSpec((r * BLK, AW), lambda i: (i, 0))],
        out_specs=pl.BlockSpec((r, AW), lambda i: (i, 0)),
        out_shape=jax.ShapeDtypeStruct((nblk, AW), F32),
        compiler_params=_cparams("parallel"), name="kmean")(ka)


def _select_blocks(gate, own, n_sel):
    nb = gate.shape[1]
    col = lax.broadcasted_iota(I32, gate.shape, 1).astype(F32)
    ownf = jnp.asarray(own, F32) if not isinstance(own, int) else float(own)
    g = jnp.where(col < ownf, gate, NEG)
    sel = jnp.zeros(gate.shape, F32)
    for _ in range(n_sel):
        mx = jnp.max(g, axis=1, keepdims=True)
        idx = jnp.min(jnp.where(g == mx, col, float(nb)), axis=1, keepdims=True)
        hit = col == idx
        sel = jnp.where(hit & (idx < ownf), 1.0, sel)
        g = jnp.where(hit, MASKV, g)
    return sel


def _select_blocks_t(gate_t, own, n_sel):
    nb = gate_t.shape[0]
    row = lax.broadcasted_iota(I32, gate_t.shape, 0).astype(F32)
    g = jnp.where(row < own, gate_t, NEG)
    sel = jnp.zeros(gate_t.shape, F32)
    for _ in range(n_sel):
        mx = jnp.max(g, axis=0, keepdims=True)
        idx = jnp.min(jnp.where(g == mx, row, float(nb)), axis=0, keepdims=True)
        hit = row == idx
        sel = jnp.where(hit & (idx < own), 1.0, sel)
        g = jnp.where(hit, MASKV, g)
    return sel


def _moba_p_kernel(slope_ref, q_ref, k_ref, vt_ref, km_ref, o_ref, a_s, s_s, bias_s, qs_s, m_s, l_s, acc_s):
    b = pl.program_id(0)
    i = pl.program_id(1)
    nb = km_ref.shape[1]
    rel_t = lax.broadcasted_iota(I32, (BLK, BLK), 1) - lax.broadcasted_iota(I32, (BLK, BLK), 0)

    @pl.when((b == 0) & (i == 0))
    def _alibi():
        relf = rel_t.astype(F32)
        for h in range(AH):
            a_s[h] = slope_ref[h] * relf

    causal = rel_t >= 0
    own = i.astype(F32)
    for h in range(AH):
        hs = slice(h * AD, (h + 1) * AD)
        q = q_ref[0, :, hs]
        kmh, kml = _split2(km_ref[0, :, hs])
        gate_t = _mm_nt(kmh, q) + _mm_nt(kml, q)
        sel = _select_blocks_t(gate_t, own, min(TOPB, nb))
        bias_s[h] = jnp.where(sel > 0.5, 0.0, MASKV)
        qs = (q.astype(F32) * (AD ** -0.5)).astype(BF16)
        qs_s[h] = qs
        s_s[h] = jnp.where(causal, _mm_nt(k_ref[0, i, :, hs], qs) - a_s[h], MASKV)
    for h in range(AH):
        hs = slice(h * AD, (h + 1) * AD)
        m = jnp.max(s_s[h], axis=0, keepdims=True)
        p = jnp.exp(s_s[h] - m)
        m_s[h] = m
        l_s[h] = jnp.sum(p, axis=0, keepdims=True)
        acc_s[h] = _mm(vt_ref[0, i, hs, :], p)

    def body(j, c):
        dj = ((i - j) * BLK).astype(F32)
        for h in range(AH):
            hs = slice(h * AD, (h + 1) * AD)
            rowterm = bias_s[h, pl.ds(j, 1), :] - slope_ref[h] * dj
            s_s[h] = (_mm_nt(k_ref[0, j, :, hs], qs_s[h]) - a_s[h]) + rowterm
        for h in range(AH):
            hs = slice(h * AD, (h + 1) * AD)
            m = m_s[h]
            m_new = jnp.maximum(m, jnp.max(s_s[h], axis=0, keepdims=True))
            p = jnp.exp(s_s[h] - m_new)
            alpha = jnp.exp(m - m_new)
            l_s[h] = alpha * l_s[h] + jnp.sum(p, axis=0, keepdims=True)
            acc_s[h] = alpha * acc_s[h] + _mm(vt_ref[0, j, hs, :], p)
            m_s[h] = m_new
        return c

    lax.fori_loop(0, i, body, 0)
    out_t = jnp.concatenate([acc_s[h] / l_s[h] for h in range(AH)], axis=0)
    o_ref[0] = out_t.T


def _moba_prompt(q, kb, vt, km, slopes):
    bsz, t, _ = q.shape
    nb = t // BLK
    gs = pltpu.PrefetchScalarGridSpec(
        num_scalar_prefetch=1, grid=(bsz, nb),
        in_specs=[pl.BlockSpec((1, BLK, AW), lambda b, i, s: (b, i, 0)),
                  pl.BlockSpec((1, nb, BLK, AW), lambda b, i, s: (b, 0, 0, 0)),
                  pl.BlockSpec((1, nb, AW, BLK), lambda b, i, s: (b, 0, 0, 0)),
                  pl.BlockSpec((1, nb, AW), lambda b, i, s: (b, 0, 0))],
        out_specs=pl.BlockSpec((1, BLK, AW), lambda b, i, s: (b, i, 0)),
        scratch_shapes=[pltpu.VMEM((AH, BLK, BLK), F32), pltpu.VMEM((AH, BLK, BLK), F32),
                        pltpu.VMEM((AH, nb, BLK), F32),
                        pltpu.VMEM((AH, BLK, AD), BF16), pltpu.VMEM((AH, 1, BLK), F32),
                        pltpu.VMEM((AH, 1, BLK), F32), pltpu.VMEM((AH, AD, BLK), F32)])
    return pl.pallas_call(
        _moba_p_kernel, grid_spec=gs,
        out_shape=jax.ShapeDtypeStruct((bsz, t, AW), F32),
        compiler_params=_cparams("arbitrary", "arbitrary"), name="moba_p")(slopes, q, kb, vt, km)


def _moba_s_kernel(pt_ref, *refs, ppb, page, nbp, past, rq):
    k_pages = refs[0:ppb]
    v_pages = refs[ppb:2 * ppb]
    q_ref, kn_ref, vn_ref, o_ref, k_s, v_s, km_s = refs[2 * ppb:]
    b = pl.program_id(0)
    j = pl.program_id(1)
    lk = (nbp + 1) * BLK

    @pl.when((b == 0) & (j == 0))
    def _zero():
        k_s[past:lk, :] = jnp.zeros((BLK, AW), BF16)
        v_s[past:lk, :] = jnp.zeros((BLK, AW), BF16)
        km_s[...] = jnp.zeros(km_s.shape, F32)

    ksum = jnp.zeros((1, AW), F32)
    for p in range(ppb):
        kp = k_pages[p][0]
        off = pl.multiple_of(j * BLK + p * page, page)
        k_s[pl.ds(off, page), :] = kp
        v_s[pl.ds(off, page), :] = v_pages[p][0]
        ksum = ksum + jnp.sum(kp.astype(F32), axis=0, keepdims=True)
    km_s[pl.ds(j, 1), :] = ksum * (1.0 / BLK)

    @pl.when(j == nbp - 1)
    def _attend():
        k_s[past:past + rq, :] = kn_ref[0].astype(BF16)
        v_s[past:past + rq, :] = vn_ref[0].astype(BF16)
        rows = AH * rq
        q8 = q_ref[0]
        qt = jnp.concatenate([q8] * AH, axis=0)
        rowh = lax.broadcasted_iota(I32, (rows, AW), 0) // rq
        laneh = lax.broadcasted_iota(I32, (rows, AW), 1) // AD
        qe = jnp.where(rowh == laneh, qt, 0.0)
        qe_b = qe.astype(BF16)
        gate = _mm_nt_x2(qe_b, km_s[...])
        sel = _select_blocks(gate, nbp, min(TOPB, nbp + 1))
        qs = (qe * (AD ** -0.5)).astype(BF16)
        hrow = lax.broadcasted_iota(I32, (rows, 1), 0) // rq
        slope = jnp.zeros((rows, 1), F32)
        for hh in range(AH):
            slope = jnp.where(hrow == hh, 2.0 ** (-8.0 * (hh + 1) / AH), slope)
        rq_pos = past + lax.broadcasted_iota(I32, (rows, BLK), 0) % rq
        lane = lax.broadcasted_iota(I32, (rows, BLK), 1)
        segs = []
        m = jnp.full((rows, 1), NEG, F32)
        for cb in range(nbp + 1):
            s = _mm_nt(qs, k_s[cb * BLK:(cb + 1) * BLK, :])
            dist = rq_pos - (lane + cb * BLK)
            s = s - slope * dist.astype(F32)
            ok = dist >= 0
            if cb < nbp:
                ok = ok & (sel[:, cb:cb + 1] > 0.5)
            s = jnp.where(ok, s, MASKV)
            segs.append(s)
            m = jnp.maximum(m, jnp.max(s, axis=1, keepdims=True))
        l = jnp.zeros((rows, 1), F32)
        acc = jnp.zeros((rows, AW), F32)
        for cb in range(nbp + 1):
            p = jnp.exp(segs[cb] - m)
            l = l + jnp.sum(p, axis=1, keepdims=True)
            acc = acc + _mm(p, v_s[cb * BLK:(cb + 1) * BLK, :])
        acc = acc / l
        out = jnp.zeros((rq, AW), F32)
        lh = lax.broadcasted_iota(I32, (rq, AW), 1) // AD
        for hh in range(AH):
            out = out + jnp.where(lh == hh, acc[hh * rq:(hh + 1) * rq, :], 0.0)
        o_ref[0] = out


def _moba_sample(page_table, cache_k, cache_v, q, k_new, v_new):
    db, n_pages = page_table.shape
    page = cache_k.shape[1]
    rq = q.shape[1]
    past = n_pages * page
    assert past % BLK == 0 and BLK % page == 0
    ppb = BLK // page
    nbp = past // BLK

    def page_spec(p):
        return pl.BlockSpec((1, page, AW), lambda b, j, pt: (pt[b * n_pages + j * ppb + p], 0, 0))

    tok = pl.BlockSpec((1, rq, AW), lambda b, j, pt: (b, 0, 0))
    gs = pltpu.PrefetchScalarGridSpec(
        num_scalar_prefetch=1, grid=(db, nbp),
        in_specs=[page_spec(p) for p in range(ppb)] * 2 + [tok, tok, tok],
        out_specs=tok,
        scratch_shapes=[pltpu.VMEM(((nbp + 1) * BLK, AW), BF16), pltpu.VMEM(((nbp + 1) * BLK, AW), BF16),
                        pltpu.VMEM((max(8, -(-(nbp + 1) // 8) * 8), AW), F32)])
    return pl.pallas_call(
        functools.partial(_moba_s_kernel, ppb=ppb, page=page, nbp=nbp, past=past, rq=rq),
        grid_spec=gs, out_shape=jax.ShapeDtypeStruct((db, rq, AW), F32),
        compiler_params=_cparams("arbitrary", "arbitrary"), name="moba_s")(
            page_table.reshape(-1), *([cache_k] * ppb), *([cache_v] * ppb), q, k_new, v_new)


def _moba_sn_kernel(pt_ref, *refs, ppb, page, nbp, past, ts):
    k_pages = refs[0:ppb]
    v_pages = refs[ppb:2 * ppb]
    q_ref, kn_ref, vn_ref, o_ref, s_s, v_s, km_s = refs[2 * ppb:]
    j = pl.program_id(1)
    rows = AH * ts
    pl_rows = page * AH
    scale = AD ** -0.5
    qx = q_ref[0]
    qs = (qx * scale).astype(BF16)

    ksum = jnp.zeros((AH, AD), F32)
    for p in range(ppb):
        kp = k_pages[p][0]
        pg = j * ppb + p
        s_s[pg] = _mm_nt(qs, kp.reshape(pl_rows, AD))
        v_s[pg] = v_pages[p][0].reshape(pl_rows, AD).astype(BF16)
        ksum = ksum + jnp.sum(kp, axis=0)
    km_s[j] = ksum * (1.0 / BLK)

    @pl.when(j == nbp - 1)
    def _attend():
        rh = lax.broadcasted_iota(I32, (rows, 1), 0) // ts
        rr = lax.broadcasted_iota(I32, (rows, 1), 0) % ts
        slope = jnp.zeros((rows, 1), F32)
        for hh in range(AH):
            slope = jnp.where(rh == hh, 2.0 ** (-8.0 * (hh + 1) / AH), slope)
        ncol = nbp * AH
        gate = _mm_nt_x2(qx.astype(BF16), km_s[...].reshape(ncol, AD))
        gcol = lax.broadcasted_iota(I32, (rows, ncol), 1)
        colf = gcol.astype(F32)
        g = jnp.where(gcol % AH == rh, gate, MASKV)
        picks = []
        for _ in range(min(TOPB, nbp)):
            mx = jnp.max(g, axis=1, keepdims=True)
            idx = jnp.min(jnp.where(g == mx, colf, float(ncol)), axis=1, keepdims=True)
            picks.append(jnp.floor(idx * (1.0 / AH)))
            g = jnp.where(colf == idx, MASKV, g)
        lane = lax.broadcasted_iota(I32, (rows, pl_rows), 1)
        head_ok = lane % AH == rh
        kpos = (lane // AH).astype(F32)
        qpos = (past + rr).astype(F32)
        lane_n = lax.broadcasted_iota(I32, (rows, ts * AH), 1)
        dist_n = rr - lane_n // AH
        s_new = _mm_nt(qs, kn_ref[0]) - slope * dist_n.astype(F32)
        s_new = jnp.where((lane_n % AH == rh) & (dist_n >= 0), s_new, MASKV)
        m = jnp.max(s_new, axis=1, keepdims=True)
        n_pg = nbp * ppb
        for pg in range(n_pg):
            blk = float(pg // ppb)
            chosen = picks[0] == blk
            for c in picks[1:]:
                chosen = chosen | (c == blk)
            s = s_s[pg] - slope * (qpos - (kpos + float(pg * page)))
            s = jnp.where(head_ok & chosen, s, MASKV)
            s_s[pg] = s
            m = jnp.maximum(m, jnp.max(s, axis=1, keepdims=True))
        p_new = jnp.exp(s_new - m)
        l = jnp.sum(p_new, axis=1, keepdims=True)
        acc = _mm(p_new, vn_ref[0])
        for pg in range(n_pg):
            p = jnp.exp(s_s[pg] - m)
            l = l + jnp.sum(p, axis=1, keepdims=True)
            acc = acc + _mm(p, v_s[pg])
        o_ref[0] = acc / l


def _moba_sample_native(page_table, cache_k, cache_v, qx, k_new, v_new, ts):
    db, n_pages = page_table.shape
    page = cache_k.shape[1]
    past = n_pages * page
    assert past % BLK == 0 and BLK % page == 0
    ppb = BLK // page
    nbp = past // BLK
    rows = AH * ts

    def page_spec(p):
        return pl.BlockSpec((1, page, AH, AD), lambda b, j, pt: (pt[b * n_pages + j * ppb + p], 0, 0, 0))

    tok = pl.BlockSpec((1, rows, AD), lambda b, j, pt: (b, 0, 0))
    gs = pltpu.PrefetchScalarGridSpec(
        num_scalar_prefetch=1, grid=(db, nbp),
        in_specs=[page_spec(p) for p in range(ppb)] * 2 + [tok, tok, tok],
        out_specs=tok,
        scratch_shapes=[pltpu.VMEM((n_pages, rows, page * AH), F32), pltpu.VMEM((n_pages, page * AH, AD), BF16),
                        pltpu.VMEM((nbp, AH, AD), F32)])
    return pl.pallas_call(
        functools.partial(_moba_sn_kernel, ppb=ppb, page=page, nbp=nbp, past=past, ts=ts),
        grid_spec=gs, out_shape=jax.ShapeDtypeStruct((db, rows, AD), F32),
        compiler_params=_cparams("arbitrary", "arbitrary"), name="moba_s")(
            page_table.reshape(-1), *([cache_k] * ppb), *([cache_v] * ppb), qx, k_new, v_new)


def _moba_st_kernel(pt_ref, *refs, ppb, page, nbp, past, ts, tn):
    k_pages = refs[0:ppb]
    v_pages = refs[ppb:2 * ppb]
    qbd_ref, kn_ref, vn_ref, o_ref, s_s, v_s, g_s = refs[2 * ppb:]
    j = pl.program_id(1)
    rows = AH * ts
    qsf = qbd_ref[0] * (AD ** -0.5)
    qs = qsf.astype(BF16)

    gsum = jnp.zeros((rows, page), F32)
    for p in range(ppb):
        pg = j * ppb + p
        s = _mm(qs, k_pages[p][0].reshape(AW, page))
        s_s[pg] = s
        gsum = gsum + s
        v_s[pg] = v_pages[p][0].reshape(AW, page).astype(BF16)
    g_s[j] = gsum

    @pl.when(j == nbp - 1)
    def _attend():
        rh = lax.broadcasted_iota(I32, (rows, 1), 0) // ts
        rr = lax.broadcasted_iota(I32, (rows, 1), 0) % ts
        slope = jnp.zeros((rows, 1), F32)
        for hh in range(AH):
            slope = jnp.where(rh == hh, 2.0 ** (-8.0 * (hh + 1) / AH), slope)
        g = [jnp.sum(g_s[jb], axis=1, keepdims=True) for jb in range(nbp)]
        picks = []
        for _ in range(min(TOPB, nbp)):
            mx = g[0]
            for jb in range(1, nbp):
                mx = jnp.maximum(mx, g[jb])
            idx = jnp.full((rows, 1), float(nbp), F32)
            for jb in reversed(range(nbp)):
                idx = jnp.where(g[jb] == mx, float(jb), idx)
            picks.append(idx)
            g = [jnp.where(idx == float(jb), MASKV, g[jb]) for jb in range(nbp)]
        lane = lax.broadcasted_iota(I32, (rows, page), 1).astype(F32)
        qpos = (past + rr).astype(F32)
        lane_n = lax.broadcasted_iota(I32, (rows, tn), 1)
        dist_n = rr - lane_n
        s_new = lax.dot_general(qsf, kn_ref[0], (((1,), (1,)), ((), ())), preferred_element_type=F32)
        s_new = jnp.where(dist_n >= 0, s_new - slope * dist_n.astype(F32), MASKV)
        m = jnp.max(s_new, axis=1, keepdims=True)
        n_pg = nbp * ppb
        for pg in range(n_pg):
            blk = float(pg // ppb)
            chosen = picks[0] == blk
            for c in picks[1:]:
                chosen = chosen | (c == blk)
            s = s_s[pg] - slope * (qpos - (lane + float(pg * page)))
            s = jnp.where(chosen, s, MASKV)
            s_s[pg] = s
            m = jnp.maximum(m, jnp.max(s, axis=1, keepdims=True))
        p_new = jnp.exp(s_new - m)
        l = jnp.sum(p_new, axis=1, keepdims=True)
        acc = jnp.dot(p_new, vn_ref[0], preferred_element_type=F32)
        for pg in range(n_pg):
            p = jnp.exp(s_s[pg] - m)
            l = l + jnp.sum(p, axis=1, keepdims=True)
            acc = acc + _mm_nt(p, v_s[pg])
        acc = acc / l
        lh = lax.broadcasted_iota(I32, (ts, AW), 1) // AD
        out = jnp.zeros((ts, AW), F32)
        for hh in range(AH):
            out = out + jnp.where(lh == hh, acc[hh * ts:(hh + 1) * ts, :], 0.0)
        o_ref[0] = out


def _moba_sample_t(page_table, cache_kt, cache_vt, qbd, k_new, v_new, ts):
    db, n_pages = page_table.shape
    page = cache_kt.shape[3]
    past = n_pages * page
    assert past % BLK == 0 and BLK % page == 0
    ppb = BLK // page
    nbp = past // BLK
    rows = AH * ts
    tn = k_new.shape[1]

    def page_spec(p):
        return pl.BlockSpec((1, AH, AD, page), lambda b, j, pt: (pt[b * n_pages + j * ppb + p], 0, 0, 0))

    gs = pltpu.PrefetchScalarGridSpec(
        num_scalar_prefetch=1, grid=(db, nbp),
        in_specs=[page_spec(p) for p in range(ppb)] * 2 + [
            pl.BlockSpec((1, rows, AW), lambda b, j, pt: (b, 0, 0)),
            pl.BlockSpec((1, tn, AW), lambda b, j, pt: (b, 0, 0)),
            pl.BlockSpec((1, tn, AW), lambda b, j, pt: (b, 0, 0))],
        out_specs=pl.BlockSpec((1, ts, AW), lambda b, j, pt: (b, 0, 0)),
        scratch_shapes=[pltpu.VMEM((n_pages, rows, page), F32), pltpu.VMEM((n_pages, AW, page), BF16),
                        pltpu.VMEM((nbp, rows, page), F32)])
    return pl.pallas_call(
        functools.partial(_moba_st_kernel, ppb=ppb, page=page, nbp=nbp, past=past, ts=ts, tn=tn),
        grid_spec=gs, out_shape=jax.ShapeDtypeStruct((db, ts, AW), F32),
        compiler_params=_cparams("arbitrary", "arbitrary"), name="moba_s")(
            page_table.reshape(-1), *([cache_kt] * ppb), *([cache_vt] * ppb), qbd, k_new, v_new)


def _merge_kernel(x_ref, hm_ref, ha_ref, sh1_ref, sc1_ref, gt1_ref, sh2_ref, sc2_ref, gt2_ref,
                  g1_ref, g2_ref, gmh_ref, wl_ref, wbm_ref, wba_ref, wo_ref, wr_ref, br_ref,
                  wsg_ref, wsu_ref, wsd_ref, cin_ref,
                  base_ref, h2_ref, idx_ref, wt_ref, rank_ref, cout_ref, cnt_s):
    i = pl.program_id(0)
    tm = x_ref.shape[0]

    @pl.when(i == 0)
    def _init():
        cnt_s[...] = cin_ref[...]

    x = x_ref[...]
    h1 = (_rms(x, g1_ref[...]) * (1.0 + sc1_ref[0]) + sh1_ref[0]).astype(BF16)
    d = functools.partial(jnp.dot, preferred_element_type=F32)
    mo = d(h1, wl_ref[:, 0:MW])
    ga = d(h1, wl_ref[:, MW:MW + D])
    gb = d(h1, wl_ref[:, MW + D:MW + 2 * D])
    hm = hm_ref[...]
    parts = []
    for h in range(MH):
        xh = hm[:, h * MD:(h + 1) * MD]
        parts.append(_rms(xh, gmh_ref[:, h * MD:(h + 1) * MD]))
    hmn = jnp.concatenate(parts, axis=1) * jax.nn.sigmoid(mo)
    merged = jax.nn.sigmoid(ga) * _mm(hmn, wbm_ref[...]) + jax.nn.sigmoid(gb) * _mm(ha_ref[...], wba_ref[...])
    x1 = x + gt1_ref[0] * _mm(merged, wo_ref[...])
    h2 = _rms(x1, g2_ref[...]) * (1.0 + sc2_ref[0]) + sh2_ref[0]
    h2_ref[...] = h2
    h2b = h2.astype(BF16)
    shared = _mm(_silu(d(h2b, wsg_ref[...])) * d(h2b, wsu_ref[...]), wsd_ref[...])
    base_ref[...] = x1 + gt2_ref[0] * shared
    scores = jax.nn.sigmoid(_mm_x3(h2, wr_ref[...]))
    col = lax.broadcasted_iota(I32, (tm, NE), 1).astype(F32)
    g = scores + br_ref[...]
    idxs, wts = [], []
    onehot = jnp.zeros((tm, NE), F32)
    for _ in range(TOPK):
        mx = jnp.max(g, axis=1, keepdims=True)
        idx = jnp.min(jnp.where(g == mx, col, float(NE)), axis=1, keepdims=True)
        hit = col == idx
        idxs.append(idx)
        wts.append(jnp.sum(jnp.where(hit, scores, 0.0), axis=1, keepdims=True))
        onehot = jnp.where(hit, 1.0, onehot)
        g = jnp.where(hit, MASKV, g)
    wsum = wts[0]
    for w in wts[1:]:
        wsum = wsum + w
    ri = lax.broadcasted_iota(I32, (tm, tm), 0)
    ci = lax.broadcasted_iota(I32, (tm, tm), 1)
    before = jnp.where(ci < ri, 1.0, 0.0).astype(BF16)
    pref = cnt_s[...] + d(before, onehot.astype(BF16))
    lane8 = lax.broadcasted_iota(I32, (tm, TOPK), 1)
    idx_o = jnp.zeros((tm, TOPK), F32)
    wt_o = jnp.zeros((tm, TOPK), F32)
    rk_o = jnp.zeros((tm, TOPK), F32)
    for kk in range(TOPK):
        rk = jnp.sum(jnp.where(col == idxs[kk], pref, 0.0), axis=1, keepdims=True)
        idx_o = jnp.where(lane8 == kk, idxs[kk], idx_o)
        wt_o = jnp.where(lane8 == kk, wts[kk] / wsum * ROUTE_SCALE, wt_o)
        rk_o = jnp.where(lane8 == kk, rk, rk_o)
    idx_ref[...] = idx_o.astype(I32)
    wt_ref[...] = wt_o
    rank_ref[...] = rk_o.astype(I32)
    cnt_s[...] = cnt_s[...] + jnp.sum(onehot, axis=0, keepdims=True)
    cout_ref[...] = cnt_s[...]


def _merge(x2, hm, ha, mod3, per_token, tm, tiles_per_batch, weights, cnt_in):
    n = x2.shape[0]
    row = lambda w: pl.BlockSpec((tm, w), lambda i: (i, 0))
    full = lambda a: pl.BlockSpec(a.shape, lambda i: (0,) * a.ndim)
    mods = [_mod_spec(per_token, tm, tiles_per_batch, c) for c in range(6)]
    sds = jax.ShapeDtypeStruct
    return pl.pallas_call(
        _merge_kernel, grid=(n // tm,),
        in_specs=[row(D), row(MW), row(AW)] + mods + [full(w) for w in weights] + [full(cnt_in)],
        out_specs=[row(D), row(D), row(TOPK), row(TOPK), row(TOPK), full(cnt_in)],
        out_shape=[sds((n, D), F32), sds((n, D), F32), sds((n, TOPK), I32), sds((n, TOPK), F32),
                   sds((n, TOPK), I32), sds((1, NE), F32)],
        scratch_shapes=[pltpu.VMEM((1, NE), F32)],
        compiler_params=_cparams("arbitrary"), name="merge")(
            x2, hm, ha, *([mod3] * 6), *weights, cnt_in)


def _dest_kernel(idx_ref, rank_ref, pstart_ref, o_ref):
    tm = idx_ref.shape[0]
    col = lax.broadcasted_iota(I32, (tm, NE), 1)
    lane8 = lax.broadcasted_iota(I32, (tm, TOPK), 1)
    idx = idx_ref[...]
    out = rank_ref[...]
    for kk in range(TOPK):
        start = jnp.sum(jnp.where(col == idx[:, kk:kk + 1], pstart_ref[...], 0.0), axis=1, keepdims=True)
        out = out + jnp.where(lane8 == kk, start.astype(I32), 0)
    o_ref[...] = out


def _dest(idx, rank, pstart):
    n = idx.shape[0]
    tm = _pick_tile(n, 512)
    row = pl.BlockSpec((tm, TOPK), lambda i: (i, 0))
    return pl.pallas_call(
        _dest_kernel, grid=(n // tm,),
        in_specs=[row, row, pl.BlockSpec((1, NE), lambda i: (0, 0))],
        out_specs=row, out_shape=jax.ShapeDtypeStruct((n, TOPK), I32),
        compiler_params=_cparams("parallel"), name="dest")(idx, rank, pstart.astype(F32).reshape(1, NE))


def _row_copy(src, dst, sem):
    return pltpu.make_async_copy(src, dst, sem)


def _dispatch_kernel(pend_ref, cnt_ref, dest_ref, h_ref, xs_ref, zbuf, ring, zsem, sem):
    step = pl.program_id(0)
    last = pl.num_programs(0) - 1
    td = h_ref.shape[0]
    slot = step % 2

    @pl.when(step == 0)
    def _zero_tails():
        zbuf[...] = jnp.zeros(zbuf.shape, F32)

        def tail(e):
            return _row_copy(zbuf, xs_ref.at[pl.ds(pl.multiple_of(pend_ref[e] - GROUP, 8), GROUP)], zsem)

        def start(e, c):
            @pl.when(cnt_ref[e] > 0)
            def _():
                tail(e).start()
            return c

        def wait(e, c):
            @pl.when(cnt_ref[e] > 0)
            def _():
                tail(e).wait()
            return c

        lax.fori_loop(0, NE, start, 0)
        lax.fori_loop(0, NE, wait, 0)

    ring[slot] = h_ref[...]

    def start(t, c):
        for kk in range(TOPK):
            dst = dest_ref[0, 0, t * TOPK + kk]
            _row_copy(ring.at[slot, pl.ds(t, 1)], xs_ref.at[pl.ds(dst, 1)], sem.at[slot]).start(priority=kk % 2)
        return c

    def drain(s):
        def wait(t, c):
            for kk in range(TOPK):
                _row_copy(ring.at[s, pl.ds(t, 1)], xs_ref.at[pl.ds(0, 1)], sem.at[s]).wait()
            return c
        lax.fori_loop(0, td, wait, 0)

    lax.fori_loop(0, td, start, 0)

    @pl.when(step > 0)
    def _prev():
        drain(1 - slot)

    @pl.when(step == last)
    def _own():
        drain(slot)


def _dispatch(pend, cnt, dest, h2, n_rows, td):
    n = h2.shape[0]
    gs = pltpu.PrefetchScalarGridSpec(
        num_scalar_prefetch=2, grid=(n // td,),
        in_specs=[pl.BlockSpec((1, 1, td * TOPK), lambda i, *_: (i, 0, 0), memory_space=pltpu.SMEM),
                  pl.BlockSpec((td, D), lambda i, *_: (i, 0))],
        out_specs=pl.BlockSpec(memory_space=pl.ANY),
        scratch_shapes=[pltpu.VMEM((GROUP, D), F32), pltpu.VMEM((2, td, D), F32),
                        pltpu.SemaphoreType.DMA(()), pltpu.SemaphoreType.DMA((2,))])
    return pl.pallas_call(
        _dispatch_kernel, grid_spec=gs, out_shape=jax.ShapeDtypeStruct((n_rows, D), F32),
        compiler_params=_cparams("arbitrary"), name="dispatch")(
            pend, cnt, dest.reshape(n // td, 1, td * TOPK), h2)


def _experts_kernel(be_ref, nu_ref, x_ref, wg_ref, wu_ref, wd_ref, o_ref, wg_s, wu_s, wd_s):
    b = pl.program_id(0)

    @pl.when(b < nu_ref[0])
    def _run():
        @pl.when((b == 0) | (be_ref[b] != be_ref[jnp.maximum(b - 1, 0)]))
        def _load():
            wg_s[...] = wg_ref[0].astype(BF16)
            wu_s[...] = wu_ref[0].astype(BF16)
            wd_s[...] = wd_ref[0].astype(BF16)

        d = functools.partial(jnp.dot, preferred_element_type=F32)
        x = x_ref[...].astype(BF16)
        hb = _silu(d(x, wg_s[...])) * d(x, wu_s[...])
        o_ref[...] = d(hb.astype(BF16), wd_s[...])


def _experts(blk_e, n_used, xs, w_gate, w_up, w_down):
    n_rows = xs.shape[0]
    rows = lambda b, be, nu: (jnp.minimum(b, nu[0] - 1), 0)
    gs = pltpu.PrefetchScalarGridSpec(
        num_scalar_prefetch=2, grid=(n_rows // GROUP,),
        in_specs=[pl.BlockSpec((GROUP, D), rows),
                  pl.BlockSpec((1, D, DE), lambda b, be, nu: (be[b], 0, 0)),
                  pl.BlockSpec((1, D, DE), lambda b, be, nu: (be[b], 0, 0)),
                  pl.BlockSpec((1, DE, D), lambda b, be, nu: (be[b], 0, 0))],
        out_specs=pl.BlockSpec((GROUP, D), rows),
        scratch_shapes=[pltpu.VMEM((D, DE), BF16), pltpu.VMEM((D, DE), BF16), pltpu.VMEM((DE, D), BF16)])
    return pl.pallas_call(
        _experts_kernel, grid_spec=gs, out_shape=jax.ShapeDtypeStruct((n_rows, D), F32),
        compiler_params=_cparams("arbitrary"), name="experts")(blk_e, n_used, xs, w_gate, w_up, w_down)


def _combine_kernel(dest_ref, dnext_ref, ys_ref, wt_ref, base_ref, gt2_ref, o_ref, buf, sem):
    step = pl.program_id(0)
    last = pl.num_programs(0) - 1
    tc = base_ref.shape[0]
    slot = step % 2

    def issue(dref, s):
        def start(t, c):
            for kk in range(TOPK):
                src = dref[0, 0, t * TOPK + kk]
                _row_copy(ys_ref.at[pl.ds(src, 1)], buf.at[s, kk, pl.ds(t, 1)], sem.at[s]).start(priority=kk % 2)
            return c
        lax.fori_loop(0, tc, start, 0)

    @pl.when(step == 0)
    def _first():
        issue(dest_ref, slot)

    @pl.when(step < last)
    def _next():
        issue(dnext_ref, 1 - slot)

    def wait(t, c):
        for kk in range(TOPK):
            _row_copy(ys_ref.at[pl.ds(0, 1)], buf.at[slot, kk, pl.ds(t, 1)], sem.at[slot]).wait()
        return c

    lax.fori_loop(0, tc, wait, 0)
    wt = wt_ref[...]
    acc = jnp.zeros((tc, D), F32)
    for kk in range(TOPK):
        acc = acc + wt[:, kk:kk + 1] * buf[slot, kk]
    o_ref[...] = base_ref[...] + gt2_ref[0] * acc


def _combine(dest, ys, wt, base, mod3, per_token, tc, tiles_per_batch):
    n = base.shape[0]
    nt = n // tc
    return pl.pallas_call(
        _combine_kernel, grid=(nt,),
        in_specs=[pl.BlockSpec((1, 1, tc * TOPK), lambda i: (i, 0, 0), memory_space=pltpu.SMEM),
                  pl.BlockSpec((1, 1, tc * TOPK), lambda i: (jnp.minimum(i + 1, nt - 1), 0, 0),
                               memory_space=pltpu.SMEM),
                  pl.BlockSpec(memory_space=pl.ANY),
                  pl.BlockSpec((tc, TOPK), lambda i: (i, 0)),
                  pl.BlockSpec((tc, D), lambda i: (i, 0)),
                  _mod_spec(per_token, tc, tiles_per_batch, 5)],
        out_specs=pl.BlockSpec((tc, D), lambda i: (i, 0)),
        out_shape=jax.ShapeDtypeStruct((n, D), F32),
        scratch_shapes=[pltpu.VMEM((2, TOPK, tc, D), F32), pltpu.SemaphoreType.DMA((2,))],
        compiler_params=_cparams("arbitrary"), name="combine")(
            dest.reshape(nt, 1, tc * TOPK), dest.reshape(nt, 1, tc * TOPK), ys, wt, base, mod3)


def _pick_tile(n, pref):
    t = pref
    while n % t:
        t //= 2
    return t


def kernel(x_prompt, x_sample, cache_k, cache_v, page_table, state_mlstm_C, state_mlstm_n, state_mlstm_m,
           c_prompt, c_sample, w_ada, b_ada, g_norm1, w_in, b_gates, g_q, g_k, g_mh, w_br_m, w_br_a, w_out,
           g_norm2, w_router, b_router, w_gate, w_up, w_down, ws_gate, ws_up, ws_down):
    depth = w_ada.shape[0]
    assert depth == 1
    bsz, t, _ = x_prompt.shape
    db, ts, _ = x_sample.shape
    n_p, n_s = bsz * t, db * ts
    assert t % BLK == 0 and t % MCHUNK == 0 and ts <= 8
    l = 0

    nc = bsz + db
    ncp = -(-nc // 8) * 8
    c_all = jnp.pad(jnp.concatenate([c_prompt, c_sample], axis=0), ((0, ncp - nc), (0, 0)))
    mod = _ada(c_all, w_ada[l], b_ada[l])
    mod_p = mod[:bsz].reshape(bsz, 1, 6 * D)
    tm_s = _pick_tile(n_s, 256)
    mod_s = jnp.repeat(mod[bsz:nc], ts, axis=0).reshape(n_s // tm_s, tm_s, 6 * D)

    wi = w_in[l]
    sl = lambda name: wi[:, _OFF[name][0]:_OFF[name][1]]
    wm = jnp.concatenate([sl("mq"), sl("mk"), sl("mv"), sl("aq"), sl("ak"), sl("av")], axis=1).astype(BF16)
    wg = jnp.pad(jnp.concatenate([sl("mi"), sl("mf")], axis=1), ((0, 0), (0, 128 - 2 * MH))).astype(BF16)
    wl = jnp.concatenate([sl("mo"), sl("ga"), sl("gb")], axis=1).astype(BF16)
    g1 = g_norm1[l].reshape(1, D)
    g2 = g_norm2[l].reshape(1, D)
    gq = jnp.tile(g_q[l], AH).reshape(1, AW)
    gk = jnp.tile(g_k[l], AH).reshape(1, AW)
    gmh = g_mh[l].reshape(1, MW)
    grp = jnp.arange(AW) // AD
    bd = jnp.where(grp[:, None] == grp[None, :], 1.0 / AD, 0.0).astype(BF16)
    merge_w = (g1, g2, gmh, wl, w_br_m[l].astype(BF16), w_br_a[l].astype(BF16), w_out[l].astype(BF16),
               w_router[l], b_router[l].reshape(1, NE), ws_gate[l].astype(BF16), ws_up[l].astype(BF16),
               ws_down[l].astype(BF16))
    slopes = 2.0 ** (-8.0 * jnp.arange(1, AH + 1, dtype=F32) / AH)

    tm_p = _pick_tile(t, 512)
    xp2 = x_prompt.reshape(n_p, D)
    xs2 = x_sample.reshape(n_s, D)
    mq_p, mk_p, mv_p, gt_p, qa_p, ka_p, va_p, kb_p, vt_p = _inproj(
        xp2, mod_p, False, tm_p, t // tm_p, g1, wm, wg, gq, gk, bd, True)
    mq_s, mk_s, mv_s, gt_s, qa_s, ka_s, va_s = _inproj(
        xs2, mod_s, True, tm_s, 1, g1, wm, wg, gq, gk, bd, False)

    zeros = functools.partial(jnp.zeros, dtype=F32)
    gp = gt_p.reshape(bsz, t, 2 * MH)
    hm_p, c_p, nn_p, m_p = _mlstm(
        mq_p.reshape(bsz, t, MW), mk_p.reshape(bsz, t, MW), mv_p.reshape(bsz, t, MW),
        gp.transpose(0, 2, 1), gp, b_gates[l], zeros((bsz, MH, MD, MD)), zeros((bsz, MH, MD)),
        zeros((bsz, MH, MD)), _pick_tile(bsz, 2), t)
    pad_t = lambda a: jnp.pad(a.reshape(db, ts, -1), ((0, 0), (0, MCHUNK - ts), (0, 0)))
    gs_ = pad_t(gt_s)
    m0 = jnp.broadcast_to(state_mlstm_m[l].astype(F32)[:, :, None], (db, MH, MD))
    hm_s, c_sm, nn_s, m_sm = _mlstm(
        pad_t(mq_s), pad_t(mk_s), pad_t(mv_s), gs_.transpose(0, 2, 1), gs_, b_gates[l],
        state_mlstm_C[l].astype(F32), state_mlstm_n[l].astype(F32), m0, _pick_tile(db, 2), ts)
    hm_s = hm_s[:, :ts].reshape(n_s, MW)

    nb = t // BLK
    km = _kmean(ka_p)
    ha_p = _moba_prompt(qa_p.reshape(bsz, t, AW), kb_p.reshape(bsz, nb, BLK, AW), vt_p.reshape(bsz, nb, AW, BLK),
                        km.reshape(bsz, nb, AW), slopes).reshape(n_p, AW)
    qh = qa_s.astype(F32).reshape(db, ts, AH, AD).transpose(0, 2, 1, 3)
    qbd = (qh[:, :, :, None, :] * jnp.eye(AH, dtype=F32)[None, :, None, :, None]).reshape(db, AH * ts, AW)
    tn = 8
    pad_n = lambda a: jnp.pad(a.reshape(db, ts, AW), ((0, 0), (0, tn - ts), (0, 0)))
    to_t = lambda c: jnp.transpose(c[l], (0, 2, 3, 1))
    ha_s = _moba_sample_t(page_table, to_t(cache_k), to_t(cache_v), qbd, pad_n(ka_s), pad_n(va_s), ts)
    ha_s = ha_s.reshape(n_s, AW)

    tmm_p = _pick_tile(t, 256)
    base_p, h2_p, idx_p, wt_p, rk_p, cnt1 = _merge(
        xp2, hm_p.reshape(n_p, MW), ha_p, mod_p, False, tmm_p, t // tmm_p, merge_w, zeros((1, NE)))
    base_s, h2_s, idx_s, wt_s, rk_s, cnt2 = _merge(
        xs2, hm_s, ha_s, mod_s, True, tm_s, 1, merge_w, cnt1)

    n_all = n_p + n_s
    cnt = cnt2.reshape(NE).astype(I32)
    padded = (cnt + GROUP - 1) // GROUP * GROUP
    pend = jnp.cumsum(padded)
    pstart = pend - padded
    n_blocks = -(-(n_all * TOPK + NE * (GROUP - 1)) // GROUP)
    n_used = (pend[-1] // GROUP).reshape(1)
    blk = jnp.minimum(jnp.arange(n_blocks, dtype=I32), n_used[0] - 1) * GROUP
    blk_e = jnp.minimum(jnp.sum((pend[None, :] <= blk[:, None]).astype(I32), axis=1), NE - 1)

    h2 = jnp.concatenate([h2_p, h2_s], axis=0)
    dest_p = _dest(idx_p, rk_p, pstart)
    dest_s = _dest(idx_s, rk_s, pstart)
    td = _pick_tile(n_all, 128)
    xs = _dispatch(pend, cnt, jnp.concatenate([dest_p, dest_s], axis=0), h2, n_blocks * GROUP, td)
    ys = _experts(blk_e, n_used, xs, w_gate[l], w_up[l], w_down[l])
    tc_p = _pick_tile(t, 64)
    y_p = _combine(dest_p, ys, wt_p, base_p, mod_p, False, tc_p, t // tc_p)
    tc_s = _pick_tile(tm_s, 64)
    mod_sc = mod_s.reshape(n_s // tc_s, tc_s, 6 * D)
    y_s = _combine(dest_s, ys, wt_s, base_s, mod_sc, True, tc_s, 1)

    st = lambda a: a[None]
    return (y_p.reshape(bsz, t, D), y_s.reshape(db, ts, D),
            st(ka_p.reshape(bsz, t, AH, AD)), st(va_p.reshape(bsz, t, AH, AD)),
            st(c_p), st(nn_p), st(m_p[:, :, 0]),
            st(ka_s.reshape(db, ts, AH, AD)), st(va_s.reshape(db, ts, AH, AD)),
            st(c_sm), st(nn_s), st(m_sm[:, :, 0]))
```

```python
import functools

import jax
import jax.numpy as jnp
from jax import lax
from jax.experimental import pallas as pl
from jax.experimental.pallas import tpu as pltpu

F32, BF16, I32 = jnp.float32, jnp.bfloat16, jnp.int32

D = 1024
MH, MD = 4, 128
AH, AD = 8, 64
MW, AW = MH * MD, AH * AD
MCHUNK = 128
BLK = 256
TOPB = 3
NE, TOPK, DE, DSH = 256, 8, 256, 256
ROUTE_SCALE = 2.5
GROUP = 256
EPS = 1e-6
NEG = -1e30
MASKV = -3.0e38
VMEM_LIMIT = 56 * 1024 * 1024
_OFF = {}
_o = 0
for _n, _w in (("mq", MW), ("mk", MW), ("mv", MW), ("mo", MW), ("mi", MH), ("mf", MH),
               ("aq", AW), ("ak", AW), ("av", AW), ("ga", D), ("gb", D)):
    _OFF[_n] = (_o, _o + _w)
    _o += _w


def _cparams(*sem):
    return pltpu.CompilerParams(dimension_semantics=sem, vmem_limit_bytes=VMEM_LIMIT)


def _mm(a, b):
    return jnp.dot(a.astype(BF16), b.astype(BF16), preferred_element_type=F32)


def _mm_nt(a, b):
    return lax.dot_general(a.astype(BF16), b.astype(BF16), (((1,), (1,)), ((), ())),
                           preferred_element_type=F32)


def _mm_tn(a, b):
    return lax.dot_general(a.astype(BF16), b.astype(BF16), (((0,), (0,)), ((), ())),
                           preferred_element_type=F32)


def _split2(x):
    hi = x.astype(BF16)
    return hi, (x - hi.astype(F32)).astype(BF16)


def _mm_x3(a, b):
    ah, al = _split2(a)
    bh, bl = _split2(b)
    d = functools.partial(jnp.dot, preferred_element_type=F32)
    return d(ah, bh) + d(al, bh) + d(ah, bl)


def _mm_nt_x2(a_bf16, b):
    bh, bl = _split2(b)
    return _mm_nt(a_bf16, bh) + _mm_nt(a_bf16, bl)


def _rms(x, g):
    return x * lax.rsqrt(jnp.mean(x * x, axis=-1, keepdims=True) + EPS) * g


def _silu(x):
    return x * jax.nn.sigmoid(x)


def _logsig(x):
    return jnp.minimum(x, 0.0) - jnp.log1p(jnp.exp(-jnp.abs(x)))


def _ada_kernel(c_ref, w_ref, b_ref, o_ref):
    o_ref[...] = _mm_x3(_silu(c_ref[...]), w_ref[...]) + b_ref[...]


def _ada(c_all, w_ada, b_ada):
    r = c_all.shape[0]
    return pl.pallas_call(
        _ada_kernel, grid=(6,),
        in_specs=[pl.BlockSpec((r, D), lambda j: (0, 0)),
                  pl.BlockSpec((D, D), lambda j: (0, j)),
                  pl.BlockSpec((1, D), lambda j: (0, j))],
        out_specs=pl.BlockSpec((r, D), lambda j: (0, j)),
        out_shape=jax.ShapeDtypeStruct((r, 6 * D), F32),
        compiler_params=_cparams("parallel"), name="ada")(c_all, w_ada, b_ada.reshape(1, 6 * D))


def _mod_spec(per_token, tm, tiles_per_batch, chunk):
    if per_token:
        return pl.BlockSpec((1, tm, D), lambda i, *_: (i, 0, chunk))
    return pl.BlockSpec((1, 1, D), lambda i, *_: (i // tiles_per_batch, 0, chunk))


def _group_ms(x, bd):
    hi, lo = _split2(x * x)
    d = functools.partial(jnp.dot, preferred_element_type=F32)
    return d(hi, bd) + d(lo, bd)


def _inproj_kernel(x_ref, sh_ref, sc_ref, g1_ref, wm_ref, wg_ref, gq_ref, gk_ref, bd_ref,
                   mq_ref, mk_ref, mv_ref, gt_ref, qa_ref, ka_ref, va_ref, *attn_refs):
    h = _rms(x_ref[...], g1_ref[...]) * (1.0 + sc_ref[0]) + sh_ref[0]
    hb = h.astype(BF16)
    d = functools.partial(jnp.dot, preferred_element_type=F32)
    mq_ref[...] = d(hb, wm_ref[:, 0:MW]).astype(BF16)
    mk_ref[...] = (d(hb, wm_ref[:, MW:2 * MW]) * (MD ** -0.5)).astype(BF16)
    mv_ref[...] = d(hb, wm_ref[:, 2 * MW:3 * MW]).astype(BF16)
    o = 3 * MW
    aq = d(hb, wm_ref[:, o:o + AW])
    ak = d(hb, wm_ref[:, o + AW:o + 2 * AW])
    av = d(hb, wm_ref[:, o + 2 * AW:o + 3 * AW])
    bd = bd_ref[...]
    qa = aq * lax.rsqrt(_group_ms(aq, bd) + EPS) * gq_ref[...]
    ka = ak * lax.rsqrt(_group_ms(ak, bd) + EPS) * gk_ref[...]
    qa_ref[...] = qa.astype(BF16)
    ka_ref[...] = ka
    va_ref[...] = av
    gt_ref[...] = d(hb, wg_ref[...])[:, 0:2 * MH]
    if attn_refs:
        kb_ref, vt_ref = attn_refs
        for r in range(vt_ref.shape[0]):
            kb_ref[r] = ka[r * BLK:(r + 1) * BLK, :].astype(BF16)
            vt_ref[r] = av[r * BLK:(r + 1) * BLK, :].T.astype(BF16)


def _inproj(x2, mod3, per_token, tm, tiles_per_batch, g1, wm, wg, gq, gk, bd, attn_layouts):
    n = x2.shape[0]
    row = lambda w: pl.BlockSpec((tm, w), lambda i: (i, 0))
    full = lambda a: pl.BlockSpec(a.shape, lambda i: (0,) * a.ndim)
    sds = jax.ShapeDtypeStruct
    out_specs = [row(MW), row(MW), row(MW), row(2 * MH), row(AW), row(AW), row(AW)]
    out_shape = [sds((n, MW), BF16), sds((n, MW), BF16), sds((n, MW), BF16), sds((n, 2 * MH), F32),
                 sds((n, AW), BF16), sds((n, AW), F32), sds((n, AW), F32)]
    if attn_layouts:
        out_specs += [pl.BlockSpec((tm // BLK, BLK, AW), lambda i: (i, 0, 0)),
                      pl.BlockSpec((tm // BLK, AW, BLK), lambda i: (i, 0, 0))]
        out_shape += [sds((n // BLK, BLK, AW), BF16), sds((n // BLK, AW, BLK), BF16)]
    return pl.pallas_call(
        _inproj_kernel, grid=(n // tm,),
        in_specs=[row(D), _mod_spec(per_token, tm, tiles_per_batch, 0),
                  _mod_spec(per_token, tm, tiles_per_batch, 1),
                  full(g1), full(wm), full(wg), full(gq), full(gk), full(bd)],
        out_specs=out_specs, out_shape=out_shape,
        compiler_params=_cparams("parallel"), name="inproj")(x2, mod3, mod3, g1, wm, wg, gq, gk, bd)


def _mlstm_kernel(q_ref, k_ref, v_ref, gr_ref, gc_ref, bgc_ref, bgr_ref, c0_ref, n0_ref, m0_ref,
                  h_ref, cn_ref, nn_ref, mn_ref, c_s, n_s, m_s, *, bb, t_valid, L):
    c = pl.program_id(1)

    @pl.when(c == 0)
    def _init():
        c_s[...] = c0_ref[...]
        n_s[...] = n0_ref[...]
        m_s[...] = m0_ref[...]

    rowi = lax.broadcasted_iota(I32, (L, L), 0)
    coli = lax.broadcasted_iota(I32, (L, L), 1)
    tri = rowi >= coli
    ok_r = (c * L + lax.broadcasted_iota(I32, (1, L), 1)) < t_valid
    ok_c = (c * L + lax.broadcasted_iota(I32, (L, 1), 0)) < t_valid
    for b in range(bb):
        g_r = gr_ref[b] + bgc_ref[...]
        g_c = gc_ref[b] + bgr_ref[...]
        for h in range(MH):
            li_r = jnp.where(ok_r, g_r[h:h + 1, :], NEG)
            lf_r = jnp.where(ok_r, _logsig(g_r[MH + h:MH + h + 1, :]), 0.0)
            li_c = jnp.where(ok_c, g_c[:, h:h + 1], NEG)
            lf_c = jnp.where(ok_c, _logsig(g_c[:, MH + h:MH + h + 1]), 0.0)
            b_c = jnp.sum(jnp.where(tri, lf_r, 0.0), axis=1, keepdims=True)
            b_r = jnp.sum(jnp.where(rowi <= coli, lf_c, 0.0), axis=0, keepdims=True)
            q = q_ref[b, :, h * MD:(h + 1) * MD]
            k = k_ref[b, :, h * MD:(h + 1) * MD]
            v = v_ref[b, :, h * MD:(h + 1) * MD]
            cm = c_s[b, h]
            nv = n_s[b, h:h + 1, :]
            m_prev = m_s[b, h:h + 1, 0:1]
            inter = m_prev + b_c
            intra = jnp.where(tri, li_r + b_c - b_r, NEG)
            m_t = jnp.maximum(inter, jnp.max(intra, axis=1, keepdims=True))
            w_inter = jnp.exp(inter - m_t)
            s = _mm_nt(q, k) * jnp.exp(intra - m_t)
            num = w_inter * _mm_nt(q, cm) + _mm(s, v)
            qn = jnp.sum(q.astype(F32) * nv, axis=1, keepdims=True)
            den = w_inter * qn + jnp.sum(s, axis=1, keepdims=True)
            h_ref[b, :, h * MD:(h + 1) * MD] = num / jnp.maximum(jnp.abs(den), jnp.exp(-m_t))
            m_last = m_t[L - 1:L, :]
            b_last = b_c[L - 1:L, :]
            w_c = jnp.exp(m_prev + b_last - m_last)
            w_s = jnp.exp(li_c + b_last - b_c - m_last)
            c_s[b, h] = w_c * cm + _mm_tn(v.astype(F32) * w_s, k)
            n_s[b, h:h + 1, :] = w_c * nv + jnp.sum(k.astype(F32) * w_s, axis=0, keepdims=True)
            m_s[b, h:h + 1, :] = jnp.broadcast_to(m_last, (1, MD))

    @pl.when(c == pl.num_programs(1) - 1)
    def _fin():
        cn_ref[...] = c_s[...]
        nn_ref[...] = n_s[...]
        mn_ref[...] = m_s[...]


def _mlstm(q, k, v, g_row, g_col, b_gates, c0, n0, m0, bb, t_valid, L):
    bsz, t, _ = q.shape
    seq = pl.BlockSpec((bb, L, MW), lambda g, c: (g, c, 0))
    st4 = pl.BlockSpec((bb, MH, MD, MD), lambda g, c: (g, 0, 0, 0))
    st3 = pl.BlockSpec((bb, MH, MD), lambda g, c: (g, 0, 0))
    sds = jax.ShapeDtypeStruct
    return pl.pallas_call(
        functools.partial(_mlstm_kernel, bb=bb, t_valid=t_valid, L=L),
        grid=(bsz // bb, t // L),
        in_specs=[seq, seq, seq,
                  pl.BlockSpec((bb, 2 * MH, L), lambda g, c: (g, 0, c)),
                  pl.BlockSpec((bb, L, 2 * MH), lambda g, c: (g, c, 0)),
                  pl.BlockSpec((2 * MH, 1), lambda g, c: (0, 0)),
                  pl.BlockSpec((1, 2 * MH), lambda g, c: (0, 0)),
                  st4, st3, st3],
        out_specs=[seq, st4, st3, st3],
        out_shape=[sds((bsz, t, MW), F32), sds((bsz, MH, MD, MD), F32),
                   sds((bsz, MH, MD), F32), sds((bsz, MH, MD), F32)],
        scratch_shapes=[pltpu.VMEM((bb, MH, MD, MD), F32), pltpu.VMEM((bb, MH, MD), F32),
                        pltpu.VMEM((bb, MH, MD), F32)],
        compiler_params=_cparams("parallel", "arbitrary"), name="mlstm")(
            q, k, v, g_row, g_col, b_gates.reshape(2 * MH, 1), b_gates.reshape(1, 2 * MH), c0, n0, m0)


def _kmean_kernel(k_ref, o_ref):
    for r in range(o_ref.shape[0]):
        o_ref[r:r + 1, :] = jnp.mean(k_ref[r * BLK:(r + 1) * BLK, :], axis=0, keepdims=True)


def _kmean(ka):
    n = ka.shape[0]
    nblk = n // BLK
    r = 8 if nblk % 8 == 0 else nblk
    return pl.pallas_call(
        _kmean_kernel, grid=(nblk // r,),
        in_specs=[pl.BlockSpec((r * BLK, AW), lambda i: (i, 0))],
        out_specs=pl.BlockSpec((r, AW), lambda i: (i, 0)),
        out_shape=jax.ShapeDtypeStruct((nblk, AW), F32),
        compiler_params=_cparams("parallel"), name="kmean")(ka)


def _select_blocks(gate, own, n_sel):
    nb = gate.shape[1]
    col = lax.broadcasted_iota(I32, gate.shape, 1).astype(F32)
    ownf = jnp.asarray(own, F32) if not isinstance(own, int) else float(own)
    g = jnp.where(col < ownf, gate, NEG)
    sel = jnp.zeros(gate.shape, F32)
    for _ in range(n_sel):
        mx = jnp.max(g, axis=1, keepdims=True)
        idx = jnp.min(jnp.where(g == mx, col, float(nb)), axis=1, keepdims=True)
        hit = col == idx
        sel = jnp.where(hit & (idx < ownf), 1.0, sel)
        g = jnp.where(hit, MASKV, g)
    return sel


def _select_blocks_t(gate_t, own, n_sel):
    nb = gate_t.shape[0]
    row = lax.broadcasted_iota(I32, gate_t.shape, 0).astype(F32)
    g = jnp.where(row < own, gate_t, NEG)
    sel = jnp.zeros(gate_t.shape, F32)
    for _ in range(n_sel):
        mx = jnp.max(g, axis=0, keepdims=True)
        idx = jnp.min(jnp.where(g == mx, row, float(nb)), axis=0, keepdims=True)
        hit = row == idx
        sel = jnp.where(hit & (idx < own), 1.0, sel)
        g = jnp.where(hit, MASKV, g)
    return sel


def _moba_p_kernel(slope_ref, q_ref, k_ref, vt_ref, km_ref, o_ref, a_s, s_s, bias_s, qs_s, m_s, l_s, acc_s):
    b = pl.program_id(0)
    i = pl.program_id(1)
    nb = km_ref.shape[1]
    rel_t = lax.broadcasted_iota(I32, (BLK, BLK), 1) - lax.broadcasted_iota(I32, (BLK, BLK), 0)

    @pl.when((b == 0) & (i == 0))
    def _alibi():
        relf = rel_t.astype(F32)
        for h in range(AH):
            a_s[h] = slope_ref[h] * relf

    causal = rel_t >= 0
    own = i.astype(F32)
    for h in range(AH):
        hs = slice(h * AD, (h + 1) * AD)
        q = q_ref[0, :, hs]
        kmh, kml = _split2(km_ref[0, :, hs])
        gate_t = _mm_nt(kmh, q) + _mm_nt(kml, q)
        sel = _select_blocks_t(gate_t, own, min(TOPB, nb))
        bias_s[h] = jnp.where(sel > 0.5, 0.0, MASKV)
        qs = (q.astype(F32) * (AD ** -0.5)).astype(BF16)
        qs_s[h] = qs
        s_s[h] = jnp.where(causal, _mm_nt(k_ref[0, i, :, hs], qs) - a_s[h], MASKV)
    for h in range(AH):
        hs = slice(h * AD, (h + 1) * AD)
        m = jnp.max(s_s[h], axis=0, keepdims=True)
        p = jnp.exp(s_s[h] - m)
        m_s[h] = m
        l_s[h] = jnp.sum(p, axis=0, keepdims=True)
        acc_s[h] = _mm(vt_ref[0, i, hs, :], p)

    def body(j, c):
        dj = ((i - j) * BLK).astype(F32)
        for h in range(AH):
            hs = slice(h * AD, (h + 1) * AD)
            rowterm = bias_s[h, pl.ds(j, 1), :] - slope_ref[h] * dj
            s_s[h] = (_mm_nt(k_ref[0, j, :, hs], qs_s[h]) - a_s[h]) + rowterm
        for h in range(AH):
            hs = slice(h * AD, (h + 1) * AD)
            m = m_s[h]
            m_new = jnp.maximum(m, jnp.max(s_s[h], axis=0, keepdims=True))
            p = jnp.exp(s_s[h] - m_new)
            alpha = jnp.exp(m - m_new)
            l_s[h] = alpha * l_s[h] + jnp.sum(p, axis=0, keepdims=True)
            acc_s[h] = alpha * acc_s[h] + _mm(vt_ref[0, j, hs, :], p)
            m_s[h] = m_new
        return c

    lax.fori_loop(0, i, body, 0)
    out_t = jnp.concatenate([acc_s[h] / l_s[h] for h in range(AH)], axis=0)
    o_ref[0] = out_t.T


def _moba_prompt(q, kb, vt, km, slopes):
    bsz, t, _ = q.shape
    nb = t // BLK
    gs = pltpu.PrefetchScalarGridSpec(
        num_scalar_prefetch=1, grid=(bsz, nb),
        in_specs=[pl.BlockSpec((1, BLK, AW), lambda b, i, s: (b, i, 0)),
                  pl.BlockSpec((1, nb, BLK, AW), lambda b, i, s: (b, 0, 0, 0)),
                  pl.BlockSpec((1, nb, AW, BLK), lambda b, i, s: (b, 0, 0, 0)),
                  pl.BlockSpec((1, nb, AW), lambda b, i, s: (b, 0, 0))],
        out_specs=pl.BlockSpec((1, BLK, AW), lambda b, i, s: (b, i, 0)),
        scratch_shapes=[pltpu.VMEM((AH, BLK, BLK), F32), pltpu.VMEM((AH, BLK, BLK), F32),
                        pltpu.VMEM((AH, nb, BLK), F32),
                        pltpu.VMEM((AH, BLK, AD), BF16), pltpu.VMEM((AH, 1, BLK), F32),
                        pltpu.VMEM((AH, 1, BLK), F32), pltpu.VMEM((AH, AD, BLK), F32)])
    return pl.pallas_call(
        _moba_p_kernel, grid_spec=gs,
        out_shape=jax.ShapeDtypeStruct((bsz, t, AW), F32),
        compiler_params=_cparams("arbitrary", "arbitrary"), name="moba_p")(slopes, q, kb, vt, km)


def _moba_s_kernel(pt_ref, *refs, ppb, page, nbp, past, rq):
    k_pages = refs[0:ppb]
    v_pages = refs[ppb:2 * ppb]
    q_ref, kn_ref, vn_ref, o_ref, k_s, v_s, km_s = refs[2 * ppb:]
    b = pl.program_id(0)
    j = pl.program_id(1)
    lk = (nbp + 1) * BLK

    @pl.when((b == 0) & (j == 0))
    def _zero():
        k_s[past:lk, :] = jnp.zeros((BLK, AW), BF16)
        v_s[past:lk, :] = jnp.zeros((BLK, AW), BF16)
        km_s[...] = jnp.zeros(km_s.shape, F32)

    ksum = jnp.zeros((1, AW), F32)
    for p in range(ppb):
        kp = k_pages[p][0]
        off = pl.multiple_of(j * BLK + p * page, page)
        k_s[pl.ds(off, page), :] = kp
        v_s[pl.ds(off, page), :] = v_pages[p][0]
        ksum = ksum + jnp.sum(kp.astype(F32), axis=0, keepdims=True)
    km_s[pl.ds(j, 1), :] = ksum * (1.0 / BLK)

    @pl.when(j == nbp - 1)
    def _attend():
        k_s[past:past + rq, :] = kn_ref[0].astype(BF16)
        v_s[past:past + rq, :] = vn_ref[0].astype(BF16)
        rows = AH * rq
        q8 = q_ref[0]
        qt = jnp.concatenate([q8] * AH, axis=0)
        rowh = lax.broadcasted_iota(I32, (rows, AW), 0) // rq
        laneh = lax.broadcasted_iota(I32, (rows, AW), 1) // AD
        qe = jnp.where(rowh == laneh, qt, 0.0)
        qe_b = qe.astype(BF16)
        gate = _mm_nt_x2(qe_b, km_s[...])
        sel = _select_blocks(gate, nbp, min(TOPB, nbp + 1))
        qs = (qe * (AD ** -0.5)).astype(BF16)
        hrow = lax.broadcasted_iota(I32, (rows, 1), 0) // rq
        slope = jnp.zeros((rows, 1), F32)
        for hh in range(AH):
            slope = jnp.where(hrow == hh, 2.0 ** (-8.0 * (hh + 1) / AH), slope)
        rq_pos = past + lax.broadcasted_iota(I32, (rows, BLK), 0) % rq
        lane = lax.broadcasted_iota(I32, (rows, BLK), 1)
        segs = []
        m = jnp.full((rows, 1), NEG, F32)
        for cb in range(nbp + 1):
            s = _mm_nt(qs, k_s[cb * BLK:(cb + 1) * BLK, :])
            dist = rq_pos - (lane + cb * BLK)
            s = s - slope * dist.astype(F32)
            ok = dist >= 0
            if cb < nbp:
                ok = ok & (sel[:, cb:cb + 1] > 0.5)
            s = jnp.where(ok, s, MASKV)
            segs.append(s)
            m = jnp.maximum(m, jnp.max(s, axis=1, keepdims=True))
        l = jnp.zeros((rows, 1), F32)
        acc = jnp.zeros((rows, AW), F32)
        for cb in range(nbp + 1):
            p = jnp.exp(segs[cb] - m)
            l = l + jnp.sum(p, axis=1, keepdims=True)
            acc = acc + _mm(p, v_s[cb * BLK:(cb + 1) * BLK, :])
        acc = acc / l
        out = jnp.zeros((rq, AW), F32)
        lh = lax.broadcasted_iota(I32, (rq, AW), 1) // AD
        for hh in range(AH):
            out = out + jnp.where(lh == hh, acc[hh * rq:(hh + 1) * rq, :], 0.0)
        o_ref[0] = out


def _moba_sample(page_table, cache_k, cache_v, q, k_new, v_new):
    db, n_pages = page_table.shape
    page = cache_k.shape[1]
    rq = q.shape[1]
    past = n_pages * page
    assert past % BLK == 0 and BLK % page == 0
    ppb = BLK // page
    nbp = past // BLK

    def page_spec(p):
        return pl.BlockSpec((1, page, AW), lambda b, j, pt: (pt[b * n_pages + j * ppb + p], 0, 0))

    tok = pl.BlockSpec((1, rq, AW), lambda b, j, pt: (b, 0, 0))
    gs = pltpu.PrefetchScalarGridSpec(
        num_scalar_prefetch=1, grid=(db, nbp),
        in_specs=[page_spec(p) for p in range(ppb)] * 2 + [tok, tok, tok],
        out_specs=tok,
        scratch_shapes=[pltpu.VMEM(((nbp + 1) * BLK, AW), BF16), pltpu.VMEM(((nbp + 1) * BLK, AW), BF16),
                        pltpu.VMEM((max(8, -(-(nbp + 1) // 8) * 8), AW), F32)])
    return pl.pallas_call(
        functools.partial(_moba_s_kernel, ppb=ppb, page=page, nbp=nbp, past=past, rq=rq),
        grid_spec=gs, out_shape=jax.ShapeDtypeStruct((db, rq, AW), F32),
        compiler_params=_cparams("arbitrary", "arbitrary"), name="moba_s")(
            page_table.reshape(-1), *([cache_k] * ppb), *([cache_v] * ppb), q, k_new, v_new)


def _moba_sn_kernel(pt_ref, *refs, ppb, page, nbp, past, ts):
    k_pages = refs[0:ppb]
    v_pages = refs[ppb:2 * ppb]
    q_ref, kn_ref, vn_ref, o_ref, s_s, v_s, km_s = refs[2 * ppb:]
    j = pl.program_id(1)
    rows = AH * ts
    pl_rows = page * AH
    scale = AD ** -0.5
    qx = q_ref[0]
    qs = (qx * scale).astype(BF16)

    ksum = jnp.zeros((AH, AD), F32)
    for p in range(ppb):
        kp = k_pages[p][0]
        pg = j * ppb + p
        s_s[pg] = _mm_nt(qs, kp.reshape(pl_rows, AD))
        v_s[pg] = v_pages[p][0].reshape(pl_rows, AD).astype(BF16)
        ksum = ksum + jnp.sum(kp, axis=0)
    km_s[j] = ksum * (1.0 / BLK)

    @pl.when(j == nbp - 1)
    def _attend():
        rh = lax.broadcasted_iota(I32, (rows, 1), 0) // ts
        rr = lax.broadcasted_iota(I32, (rows, 1), 0) % ts
        slope = jnp.zeros((rows, 1), F32)
        for hh in range(AH):
            slope = jnp.where(rh == hh, 2.0 ** (-8.0 * (hh + 1) / AH), slope)
        ncol = nbp * AH
        gate = _mm_nt_x2(qx.astype(BF16), km_s[...].reshape(ncol, AD))
        gcol = lax.broadcasted_iota(I32, (rows, ncol), 1)
        colf = gcol.astype(F32)
        g = jnp.where(gcol % AH == rh, gate, MASKV)
        picks = []
        for _ in range(min(TOPB, nbp)):
            mx = jnp.max(g, axis=1, keepdims=True)
            idx = jnp.min(jnp.where(g == mx, colf, float(ncol)), axis=1, keepdims=True)
            picks.append(jnp.floor(idx * (1.0 / AH)))
            g = jnp.where(colf == idx, MASKV, g)
        lane = lax.broadcasted_iota(I32, (rows, pl_rows), 1)
        head_ok = lane % AH == rh
        kpos = (lane // AH).astype(F32)
        qpos = (past + rr).astype(F32)
        lane_n = lax.broadcasted_iota(I32, (rows, ts * AH), 1)
        dist_n = rr - lane_n // AH
        s_new = _mm_nt(qs, kn_ref[0]) - slope * dist_n.astype(F32)
        s_new = jnp.where((lane_n % AH == rh) & (dist_n >= 0), s_new, MASKV)
        m = jnp.max(s_new, axis=1, keepdims=True)
        n_pg = nbp * ppb
        for pg in range(n_pg):
            blk = float(pg // ppb)
            chosen = picks[0] == blk
            for c in picks[1:]:
                chosen = chosen | (c == blk)
            s = s_s[pg] - slope * (qpos - (kpos + float(pg * page)))
            s = jnp.where(head_ok & chosen, s, MASKV)
            s_s[pg] = s
            m = jnp.maximum(m, jnp.max(s, axis=1, keepdims=True))
        p_new = jnp.exp(s_new - m)
        l = jnp.sum(p_new, axis=1, keepdims=True)
        acc = _mm(p_new, vn_ref[0])
        for pg in range(n_pg):
            p = jnp.exp(s_s[pg] - m)
            l = l + jnp.sum(p, axis=1, keepdims=True)
            acc = acc + _mm(p, v_s[pg])
        o_ref[0] = acc / l


def _moba_sample_native(page_table, cache_k, cache_v, qx, k_new, v_new, ts):
    db, n_pages = page_table.shape
    page = cache_k.shape[1]
    past = n_pages * page
    assert past % BLK == 0 and BLK % page == 0
    ppb = BLK // page
    nbp = past // BLK
    rows = AH * ts

    def page_spec(p):
        return pl.BlockSpec((1, page, AH, AD), lambda b, j, pt: (pt[b * n_pages + j * ppb + p], 0, 0, 0))

    tok = pl.BlockSpec((1, rows, AD), lambda b, j, pt: (b, 0, 0))
    gs = pltpu.PrefetchScalarGridSpec(
        num_scalar_prefetch=1, grid=(db, nbp),
        in_specs=[page_spec(p) for p in range(ppb)] * 2 + [tok, tok, tok],
        out_specs=tok,
        scratch_shapes=[pltpu.VMEM((n_pages, rows, page * AH), F32), pltpu.VMEM((n_pages, page * AH, AD), BF16),
                        pltpu.VMEM((nbp, AH, AD), F32)])
    return pl.pallas_call(
        functools.partial(_moba_sn_kernel, ppb=ppb, page=page, nbp=nbp, past=past, ts=ts),
        grid_spec=gs, out_shape=jax.ShapeDtypeStruct((db, rows, AD), F32),
        compiler_params=_cparams("arbitrary", "arbitrary"), name="moba_s")(
            page_table.reshape(-1), *([cache_k] * ppb), *([cache_v] * ppb), qx, k_new, v_new)


def _moba_st_kernel(pt_ref, *refs, ppb, page, nbp, past, ts, tn):
    k_pages = refs[0:ppb]
    v_pages = refs[ppb:2 * ppb]
    qbd_ref, kn_ref, vn_ref, o_ref, s_s, v_s, g_s = refs[2 * ppb:]
    j = pl.program_id(1)
    rows = AH * ts
    qsf = qbd_ref[0] * (AD ** -0.5)
    qs = qsf.astype(BF16)

    gsum = jnp.zeros((rows, page), F32)
    for p in range(ppb):
        pg = j * ppb + p
        s = _mm(qs, k_pages[p][0].reshape(AW, page))
        s_s[pg] = s
        gsum = gsum + s
        v_s[pg] = v_pages[p][0].reshape(AW, page).astype(BF16)
    g_s[j] = gsum

    @pl.when(j == nbp - 1)
    def _attend():
        rh = lax.broadcasted_iota(I32, (rows, 1), 0) // ts
        rr = lax.broadcasted_iota(I32, (rows, 1), 0) % ts
        slope = jnp.zeros((rows, 1), F32)
        for hh in range(AH):
            slope = jnp.where(rh == hh, 2.0 ** (-8.0 * (hh + 1) / AH), slope)
        g = [jnp.sum(g_s[jb], axis=1, keepdims=True) for jb in range(nbp)]
        picks = []
        for _ in range(min(TOPB, nbp)):
            mx = g[0]
            for jb in range(1, nbp):
                mx = jnp.maximum(mx, g[jb])
            idx = jnp.full((rows, 1), float(nbp), F32)
            for jb in reversed(range(nbp)):
                idx = jnp.where(g[jb] == mx, float(jb), idx)
            picks.append(idx)
            g = [jnp.where(idx == float(jb), MASKV, g[jb]) for jb in range(nbp)]
        lane = lax.broadcasted_iota(I32, (rows, page), 1).astype(F32)
        qpos = (past + rr).astype(F32)
        lane_n = lax.broadcasted_iota(I32, (rows, tn), 1)
        dist_n = rr - lane_n
        s_new = lax.dot_general(qsf, kn_ref[0], (((1,), (1,)), ((), ())), preferred_element_type=F32)
        s_new = jnp.where(dist_n >= 0, s_new - slope * dist_n.astype(F32), MASKV)
        m = jnp.max(s_new, axis=1, keepdims=True)
        n_pg = nbp * ppb
        for pg in range(n_pg):
            blk = float(pg // ppb)
            chosen = picks[0] == blk
            for c in picks[1:]:
                chosen = chosen | (c == blk)
            s = s_s[pg] - slope * (qpos - (lane + float(pg * page)))
            s = jnp.where(chosen, s, MASKV)
            s_s[pg] = s
            m = jnp.maximum(m, jnp.max(s, axis=1, keepdims=True))
        p_new = jnp.exp(s_new - m)
        l = jnp.sum(p_new, axis=1, keepdims=True)
        acc = jnp.dot(p_new, vn_ref[0], preferred_element_type=F32)
        for pg in range(n_pg):
            p = jnp.exp(s_s[pg] - m)
            l = l + jnp.sum(p, axis=1, keepdims=True)
            acc = acc + _mm_nt(p, v_s[pg])
        acc = acc / l
        lh = lax.broadcasted_iota(I32, (ts, AW), 1) // AD
        out = jnp.zeros((ts, AW), F32)
        for hh in range(AH):
            out = out + jnp.where(lh == hh, acc[hh * ts:(hh + 1) * ts, :], 0.0)
        o_ref[0] = out


def _moba_sample_t(page_table, cache_kt, cache_vt, qbd, k_new, v_new, ts):
    db, n_pages = page_table.shape
    page = cache_kt.shape[3]
    past = n_pages * page
    assert past % BLK == 0 and BLK % page == 0
    ppb = BLK // page
    nbp = past // BLK
    rows = AH * ts
    tn = k_new.shape[1]

    def page_spec(p):
        return pl.BlockSpec((1, AH, AD, page), lambda b, j, pt: (pt[b * n_pages + j * ppb + p], 0, 0, 0))

    gs = pltpu.PrefetchScalarGridSpec(
        num_scalar_prefetch=1, grid=(db, nbp),
        in_specs=[page_spec(p) for p in range(ppb)] * 2 + [
            pl.BlockSpec((1, rows, AW), lambda b, j, pt: (b, 0, 0)),
            pl.BlockSpec((1, tn, AW), lambda b, j, pt: (b, 0, 0)),
            pl.BlockSpec((1, tn, AW), lambda b, j, pt: (b, 0, 0))],
        out_specs=pl.BlockSpec((1, ts, AW), lambda b, j, pt: (b, 0, 0)),
        scratch_shapes=[pltpu.VMEM((n_pages, rows, page), F32), pltpu.VMEM((n_pages, AW, page), BF16),
                        pltpu.VMEM((nbp, rows, page), F32)])
    return pl.pallas_call(
        functools.partial(_moba_st_kernel, ppb=ppb, page=page, nbp=nbp, past=past, ts=ts, tn=tn),
        grid_spec=gs, out_shape=jax.ShapeDtypeStruct((db, ts, AW), F32),
        compiler_params=_cparams("arbitrary", "arbitrary"), name="moba_s")(
            page_table.reshape(-1), *([cache_kt] * ppb), *([cache_vt] * ppb), qbd, k_new, v_new)


def _moba_sm_kernel(pt_ref, qbd_ref, kn_ref, vn_ref, ck_ref, cv_ref, o_ref, kbuf, vbuf, s_s, sem,
                    *, n_pages, page, nbp, past, ts, tn):
    b = pl.program_id(0)
    slot = b % 2
    ppb = n_pages // nbp
    rows = AH * ts

    def page_copies(seq, sl, pg):
        pid = pt_ref[seq * n_pages + pg]
        return (pltpu.make_async_copy(ck_ref.at[pid], kbuf.at[sl, pg], sem.at[sl, 0]),
                pltpu.make_async_copy(cv_ref.at[pid], vbuf.at[sl, pg], sem.at[sl, 1]))

    def fetch(seq, sl):
        for pg in range(n_pages):
            for cp in page_copies(seq, sl, pg):
                cp.start()

    @pl.when(b == 0)
    def _first():
        fetch(b, slot)

    @pl.when(b + 1 < pl.num_programs(0))
    def _next():
        fetch(b + 1, 1 - slot)

    for pg in range(n_pages):
        for cp in page_copies(b, slot, pg):
            cp.wait()

    qsf = qbd_ref[0] * (AD ** -0.5)
    qs = qsf.astype(BF16)
    g = []
    for jb in range(nbp):
        gsum = jnp.zeros((rows, page), F32)
        for p in range(ppb):
            pg = jb * ppb + p
            s = _mm(qs, kbuf[slot, pg].reshape(AW, page))
            s_s[pg] = s
            gsum = gsum + s
        g.append(jnp.sum(gsum, axis=1, keepdims=True))
    rh = lax.broadcasted_iota(I32, (rows, 1), 0) // ts
    rr = lax.broadcasted_iota(I32, (rows, 1), 0) % ts
    slope = jnp.zeros((rows, 1), F32)
    for hh in range(AH):
        slope = jnp.where(rh == hh, 2.0 ** (-8.0 * (hh + 1) / AH), slope)
    picks = []
    for _ in range(min(TOPB, nbp)):
        mx = g[0]
        for jb in range(1, nbp):
            mx = jnp.maximum(mx, g[jb])
        idx = jnp.full((rows, 1), float(nbp), F32)
        for jb in reversed(range(nbp)):
            idx = jnp.where(g[jb] == mx, float(jb), idx)
        picks.append(idx)
        g = [jnp.where(idx == float(jb), MASKV, g[jb]) for jb in range(nbp)]
    lane = lax.broadcasted_iota(I32, (rows, page), 1).astype(F32)
    qpos = (past + rr).astype(F32)
    dist_n = rr - lax.broadcasted_iota(I32, (rows, tn), 1)
    s_new = lax.dot_general(qsf, kn_ref[0], (((1,), (1,)), ((), ())), preferred_element_type=F32)
    s_new = jnp.where(dist_n >= 0, s_new - slope * dist_n.astype(F32), MASKV)
    m = jnp.max(s_new, axis=1, keepdims=True)
    for pg in range(n_pages):
        blk = float(pg // ppb)
        chosen = picks[0] == blk
        for c in picks[1:]:
            chosen = chosen | (c == blk)
        s = jnp.where(chosen, s_s[pg] - slope * (qpos - (lane + float(pg * page))), MASKV)
        s_s[pg] = s
        m = jnp.maximum(m, jnp.max(s, axis=1, keepdims=True))
    p_new = jnp.exp(s_new - m)
    l = jnp.sum(p_new, axis=1, keepdims=True)
    acc = jnp.dot(p_new, vn_ref[0], preferred_element_type=F32)
    for pg in range(n_pages):
        p = jnp.exp(s_s[pg] - m)
        l = l + jnp.sum(p, axis=1, keepdims=True)
        acc = acc + _mm_nt(p, vbuf[slot, pg].reshape(AW, page))
    acc = acc / l
    lh = lax.broadcasted_iota(I32, (ts, AW), 1) // AD
    out = jnp.zeros((ts, AW), F32)
    for hh in range(AH):
        out = out + jnp.where(lh == hh, acc[hh * ts:(hh + 1) * ts, :], 0.0)
    o_ref[0] = out


def _moba_sample_m(page_table, cache_kt, cache_vt, qbd, k_new, v_new, ts):
    db, n_pages = page_table.shape
    page = cache_kt.shape[3]
    past = n_pages * page
    assert past % BLK == 0 and BLK % page == 0
    nbp = past // BLK
    rows = AH * ts
    tn = k_new.shape[1]
    gs = pltpu.PrefetchScalarGridSpec(
        num_scalar_prefetch=1, grid=(db,),
        in_specs=[pl.BlockSpec((1, rows, AW), lambda b, pt: (b, 0, 0)),
                  pl.BlockSpec((1, tn, AW), lambda b, pt: (b, 0, 0)),
                  pl.BlockSpec((1, tn, AW), lambda b, pt: (b, 0, 0)),
                  pl.BlockSpec(memory_space=pl.ANY), pl.BlockSpec(memory_space=pl.ANY)],
        out_specs=pl.BlockSpec((1, ts, AW), lambda b, pt: (b, 0, 0)),
        scratch_shapes=[pltpu.VMEM((2, n_pages, AH, AD, page), F32), pltpu.VMEM((2, n_pages, AH, AD, page), F32),
                        pltpu.VMEM((n_pages, rows, page), F32), pltpu.SemaphoreType.DMA((2, 2))])
    return pl.pallas_call(
        functools.partial(_moba_sm_kernel, n_pages=n_pages, page=page, nbp=nbp, past=past, ts=ts, tn=tn),
        grid_spec=gs, out_shape=jax.ShapeDtypeStruct((db, ts, AW), F32),
        compiler_params=_cparams("arbitrary"), name="moba_s")(
            page_table.reshape(-1), qbd, k_new, v_new, cache_kt, cache_vt)


def _merge_kernel(x_ref, hm_ref, ha_ref, sh1_ref, sc1_ref, gt1_ref, sh2_ref, sc2_ref, gt2_ref,
                  g1_ref, g2_ref, gmh_ref, wl_ref, wbm_ref, wba_ref, wo_ref, wr_ref, br_ref,
                  wsg_ref, wsu_ref, wsd_ref, cin_ref,
                  base_ref, h2_ref, idx_ref, wt_ref, rank_ref, cout_ref, cnt_s):
    i = pl.program_id(0)
    tm = x_ref.shape[0]

    @pl.when(i == 0)
    def _init():
        cnt_s[...] = cin_ref[...]

    x = x_ref[...]
    h1 = (_rms(x, g1_ref[...]) * (1.0 + sc1_ref[0]) + sh1_ref[0]).astype(BF16)
    d = functools.partial(jnp.dot, preferred_element_type=F32)
    mo = d(h1, wl_ref[:, 0:MW])
    ga = d(h1, wl_ref[:, MW:MW + D])
    gb = d(h1, wl_ref[:, MW + D:MW + 2 * D])
    hm = hm_ref[...]
    parts = []
    for h in range(MH):
        xh = hm[:, h * MD:(h + 1) * MD]
        parts.append(_rms(xh, gmh_ref[:, h * MD:(h + 1) * MD]))
    hmn = jnp.concatenate(parts, axis=1) * jax.nn.sigmoid(mo)
    merged = jax.nn.sigmoid(ga) * _mm(hmn, wbm_ref[...]) + jax.nn.sigmoid(gb) * _mm(ha_ref[...], wba_ref[...])
    x1 = x + gt1_ref[0] * _mm(merged, wo_ref[...])
    h2 = _rms(x1, g2_ref[...]) * (1.0 + sc2_ref[0]) + sh2_ref[0]
    h2_ref[...] = h2
    h2b = h2.astype(BF16)
    shared = _mm(_silu(d(h2b, wsg_ref[...])) * d(h2b, wsu_ref[...]), wsd_ref[...])
    base_ref[...] = x1 + gt2_ref[0] * shared
    scores = jax.nn.sigmoid(_mm_x3(h2, wr_ref[...]))
    col = lax.broadcasted_iota(I32, (tm, NE), 1).astype(F32)
    g = scores + br_ref[...]
    idxs, wts = [], []
    onehot = jnp.zeros((tm, NE), F32)
    for _ in range(TOPK):
        mx = jnp.max(g, axis=1, keepdims=True)
        idx = jnp.min(jnp.where(g == mx, col, float(NE)), axis=1, keepdims=True)
        hit = col == idx
        idxs.append(idx)
        wts.append(jnp.sum(jnp.where(hit, scores, 0.0), axis=1, keepdims=True))
        onehot = jnp.where(hit, 1.0, onehot)
        g = jnp.where(hit, MASKV, g)
    wsum = wts[0]
    for w in wts[1:]:
        wsum = wsum + w
    ri = lax.broadcasted_iota(I32, (tm, tm), 0)
    ci = lax.broadcasted_iota(I32, (tm, tm), 1)
    before = jnp.where(ci < ri, 1.0, 0.0).astype(BF16)
    pref = cnt_s[...] + d(before, onehot.astype(BF16))
    lane8 = lax.broadcasted_iota(I32, (tm, TOPK), 1)
    idx_o = jnp.zeros((tm, TOPK), F32)
    wt_o = jnp.zeros((tm, TOPK), F32)
    rk_o = jnp.zeros((tm, TOPK), F32)
    for kk in range(TOPK):
        rk = jnp.sum(jnp.where(col == idxs[kk], pref, 0.0), axis=1, keepdims=True)
        idx_o = jnp.where(lane8 == kk, idxs[kk], idx_o)
        wt_o = jnp.where(lane8 == kk, wts[kk] / wsum * ROUTE_SCALE, wt_o)
        rk_o = jnp.where(lane8 == kk, rk, rk_o)
    idx_ref[...] = idx_o.astype(I32)
    wt_ref[...] = wt_o
    rank_ref[...] = rk_o.astype(I32)
    cnt_s[...] = cnt_s[...] + jnp.sum(onehot, axis=0, keepdims=True)
    cout_ref[...] = cnt_s[...]


def _merge(x2, hm, ha, mod3, per_token, tm, tiles_per_batch, weights, cnt_in):
    n = x2.shape[0]
    row = lambda w: pl.BlockSpec((tm, w), lambda i: (i, 0))
    full = lambda a: pl.BlockSpec(a.shape, lambda i: (0,) * a.ndim)
    mods = [_mod_spec(per_token, tm, tiles_per_batch, c) for c in range(6)]
    sds = jax.ShapeDtypeStruct
    return pl.pallas_call(
        _merge_kernel, grid=(n // tm,),
        in_specs=[row(D), row(MW), row(AW)] + mods + [full(w) for w in weights] + [full(cnt_in)],
        out_specs=[row(D), row(D), row(TOPK), row(TOPK), row(TOPK), full(cnt_in)],
        out_shape=[sds((n, D), F32), sds((n, D), F32), sds((n, TOPK), I32), sds((n, TOPK), F32),
                   sds((n, TOPK), I32), sds((1, NE), F32)],
        scratch_shapes=[pltpu.VMEM((1, NE), F32)],
        compiler_params=_cparams("arbitrary"), name="merge")(
            x2, hm, ha, *([mod3] * 6), *weights, cnt_in)


def _dest_kernel(idx_ref, rank_ref, pstart_ref, o_ref):
    tm = idx_ref.shape[0]
    col = lax.broadcasted_iota(I32, (tm, NE), 1)
    lane8 = lax.broadcasted_iota(I32, (tm, TOPK), 1)
    idx = idx_ref[...]
    out = rank_ref[...]
    for kk in range(TOPK):
        start = jnp.sum(jnp.where(col == idx[:, kk:kk + 1], pstart_ref[...], 0.0), axis=1, keepdims=True)
        out = out + jnp.where(lane8 == kk, start.astype(I32), 0)
    o_ref[...] = out


def _dest(idx, rank, pstart):
    n = idx.shape[0]
    tm = _pick_tile(n, 512)
    row = pl.BlockSpec((tm, TOPK), lambda i: (i, 0))
    return pl.pallas_call(
        _dest_kernel, grid=(n // tm,),
        in_specs=[row, row, pl.BlockSpec((1, NE), lambda i: (0, 0))],
        out_specs=row, out_shape=jax.ShapeDtypeStruct((n, TOPK), I32),
        compiler_params=_cparams("parallel"), name="dest")(idx, rank, pstart.astype(F32).reshape(1, NE))


def _row_copy(src, dst, sem):
    return pltpu.make_async_copy(src, dst, sem)


def _dispatch_kernel(pend_ref, cnt_ref, dest_ref, h_ref, xs_ref, zbuf, ring, zsem, sem):
    step = pl.program_id(0)
    last = pl.num_programs(0) - 1
    td = h_ref.shape[0]
    slot = step % 2

    @pl.when(step == 0)
    def _zero_tails():
        zbuf[...] = jnp.zeros(zbuf.shape, F32)

        def tail(e):
            return _row_copy(zbuf, xs_ref.at[pl.ds(pl.multiple_of(pend_ref[e] - GROUP, 8), GROUP)], zsem)

        def start(e, c):
            @pl.when(cnt_ref[e] > 0)
            def _():
                tail(e).start()
            return c

        def wait(e, c):
            @pl.when(cnt_ref[e] > 0)
            def _():
                tail(e).wait()
            return c

        lax.fori_loop(0, NE, start, 0)
        lax.fori_loop(0, NE, wait, 0)

    ring[slot] = h_ref[...]

    def start(t, c):
        for kk in range(TOPK):
            dst = dest_ref[0, 0, t * TOPK + kk]
            _row_copy(ring.at[slot, pl.ds(t, 1)], xs_ref.at[pl.ds(dst, 1)], sem.at[slot]).start(priority=kk % 2)
        return c

    def drain(s):
        def wait(t, c):
            for kk in range(TOPK):
                _row_copy(ring.at[s, pl.ds(t, 1)], xs_ref.at[pl.ds(0, 1)], sem.at[s]).wait()
            return c
        lax.fori_loop(0, td, wait, 0)

    lax.fori_loop(0, td, start, 0)

    @pl.when(step > 0)
    def _prev():
        drain(1 - slot)

    @pl.when(step == last)
    def _own():
        drain(slot)


def _dispatch(pend, cnt, dest, h2, n_rows, td):
    n = h2.shape[0]
    gs = pltpu.PrefetchScalarGridSpec(
        num_scalar_prefetch=2, grid=(n // td,),
        in_specs=[pl.BlockSpec((1, 1, td * TOPK), lambda i, *_: (i, 0, 0), memory_space=pltpu.SMEM),
                  pl.BlockSpec((td, D), lambda i, *_: (i, 0))],
        out_specs=pl.BlockSpec(memory_space=pl.ANY),
        scratch_shapes=[pltpu.VMEM((GROUP, D), F32), pltpu.VMEM((2, td, D), F32),
                        pltpu.SemaphoreType.DMA(()), pltpu.SemaphoreType.DMA((2,))])
    return pl.pallas_call(
        _dispatch_kernel, grid_spec=gs, out_shape=jax.ShapeDtypeStruct((n_rows, D), F32),
        compiler_params=_cparams("arbitrary"), name="dispatch")(
            pend, cnt, dest.reshape(n // td, 1, td * TOPK), h2)


def _experts_kernel(be_ref, nu_ref, x_ref, wg_ref, wu_ref, wd_ref, o_ref, wg_s, wu_s, wd_s):
    b = pl.program_id(0)

    @pl.when(b < nu_ref[0])
    def _run():
        @pl.when((b == 0) | (be_ref[b] != be_ref[jnp.maximum(b - 1, 0)]))
        def _load():
            wg_s[...] = wg_ref[0].astype(BF16)
            wu_s[...] = wu_ref[0].astype(BF16)
            wd_s[...] = wd_ref[0].astype(BF16)

        d = functools.partial(jnp.dot, preferred_element_type=F32)
        x = x_ref[...].astype(BF16)
        hb = _silu(d(x, wg_s[...])) * d(x, wu_s[...])
        o_ref[...] = d(hb.astype(BF16), wd_s[...])


def _experts(blk_e, n_used, xs, w_gate, w_up, w_down):
    n_rows = xs.shape[0]
    rows = lambda b, be, nu: (jnp.minimum(b, nu[0] - 1), 0)
    gs = pltpu.PrefetchScalarGridSpec(
        num_scalar_prefetch=2, grid=(n_rows // GROUP,),
        in_specs=[pl.BlockSpec((GROUP, D), rows),
                  pl.BlockSpec((1, D, DE), lambda b, be, nu: (be[b], 0, 0)),
                  pl.BlockSpec((1, D, DE), lambda b, be, nu: (be[b], 0, 0)),
                  pl.BlockSpec((1, DE, D), lambda b, be, nu: (be[b], 0, 0))],
        out_specs=pl.BlockSpec((GROUP, D), rows),
        scratch_shapes=[pltpu.VMEM((D, DE), BF16), pltpu.VMEM((D, DE), BF16), pltpu.VMEM((DE, D), BF16)])
    return pl.pallas_call(
        _experts_kernel, grid_spec=gs, out_shape=jax.ShapeDtypeStruct((n_rows, D), F32),
        compiler_params=_cparams("arbitrary"), name="experts")(blk_e, n_used, xs, w_gate, w_up, w_down)


def _combine_kernel(dest_ref, dnext_ref, ys_ref, wt_ref, base_ref, gt2_ref, o_ref, buf, sem):
    step = pl.program_id(0)
    last = pl.num_programs(0) - 1
    tc = base_ref.shape[0]
    slot = step % 2

    def issue(dref, s):
        def start(t, c):
            for kk in range(TOPK):
                src = dref[0, 0, t * TOPK + kk]
                _row_copy(ys_ref.at[pl.ds(src, 1)], buf.at[s, kk, pl.ds(t, 1)], sem.at[s]).start(priority=kk % 2)
            return c
        lax.fori_loop(0, tc, start, 0)

    @pl.when(step == 0)
    def _first():
        issue(dest_ref, slot)

    @pl.when(step < last)
    def _next():
        issue(dnext_ref, 1 - slot)

    def wait(t, c):
        for kk in range(TOPK):
            _row_copy(ys_ref.at[pl.ds(0, 1)], buf.at[slot, kk, pl.ds(t, 1)], sem.at[slot]).wait()
        return c

    lax.fori_loop(0, tc, wait, 0)
    wt = wt_ref[...]
    acc = jnp.zeros((tc, D), F32)
    for kk in range(TOPK):
        acc = acc + wt[:, kk:kk + 1] * buf[slot, kk]
    o_ref[...] = base_ref[...] + gt2_ref[0] * acc


def _combine(dest, ys, wt, base, mod3, per_token, tc, tiles_per_batch):
    n = base.shape[0]
    nt = n // tc
    return pl.pallas_call(
        _combine_kernel, grid=(nt,),
        in_specs=[pl.BlockSpec((1, 1, tc * TOPK), lambda i: (i, 0, 0), memory_space=pltpu.SMEM),
                  pl.BlockSpec((1, 1, tc * TOPK), lambda i: (jnp.minimum(i + 1, nt - 1), 0, 0),
                               memory_space=pltpu.SMEM),
                  pl.BlockSpec(memory_space=pl.ANY),
                  pl.BlockSpec((tc, TOPK), lambda i: (i, 0)),
                  pl.BlockSpec((tc, D), lambda i: (i, 0)),
                  _mod_spec(per_token, tc, tiles_per_batch, 5)],
        out_specs=pl.BlockSpec((tc, D), lambda i: (i, 0)),
        out_shape=jax.ShapeDtypeStruct((n, D), F32),
        scratch_shapes=[pltpu.VMEM((2, TOPK, tc, D), F32), pltpu.SemaphoreType.DMA((2,))],
        compiler_params=_cparams("arbitrary"), name="combine")(
            dest.reshape(nt, 1, tc * TOPK), dest.reshape(nt, 1, tc * TOPK), ys, wt, base, mod3)


def _pick_tile(n, pref):
    t = pref
    while n % t:
        t //= 2
    return t


def kernel(x_prompt, x_sample, cache_k, cache_v, page_table, state_mlstm_C, state_mlstm_n, state_mlstm_m,
           c_prompt, c_sample, w_ada, b_ada, g_norm1, w_in, b_gates, g_q, g_k, g_mh, w_br_m, w_br_a, w_out,
           g_norm2, w_router, b_router, w_gate, w_up, w_down, ws_gate, ws_up, ws_down):
    depth = w_ada.shape[0]
    assert depth == 1
    bsz, t, _ = x_prompt.shape
    db, ts, _ = x_sample.shape
    n_p, n_s = bsz * t, db * ts
    assert t % BLK == 0 and t % MCHUNK == 0 and ts <= 8
    l = 0

    nc = bsz + db
    ncp = -(-nc // 8) * 8
    c_all = jnp.pad(jnp.concatenate([c_prompt, c_sample], axis=0), ((0, ncp - nc), (0, 0)))
    mod = _ada(c_all, w_ada[l], b_ada[l])
    mod_p = mod[:bsz].reshape(bsz, 1, 6 * D)
    tm_s = _pick_tile(n_s, 256)
    mod_s = jnp.repeat(mod[bsz:nc], ts, axis=0).reshape(n_s // tm_s, tm_s, 6 * D)

    wi = w_in[l]
    sl = lambda name: wi[:, _OFF[name][0]:_OFF[name][1]]
    wm = jnp.concatenate([sl("mq"), sl("mk"), sl("mv"), sl("aq"), sl("ak"), sl("av")], axis=1).astype(BF16)
    wg = jnp.pad(jnp.concatenate([sl("mi"), sl("mf")], axis=1), ((0, 0), (0, 128 - 2 * MH))).astype(BF16)
    wl = jnp.concatenate([sl("mo"), sl("ga"), sl("gb")], axis=1).astype(BF16)
    g1 = g_norm1[l].reshape(1, D)
    g2 = g_norm2[l].reshape(1, D)
    gq = jnp.tile(g_q[l], AH).reshape(1, AW)
    gk = jnp.tile(g_k[l], AH).reshape(1, AW)
    gmh = g_mh[l].reshape(1, MW)
    grp = jnp.arange(AW) // AD
    bd = jnp.where(grp[:, None] == grp[None, :], 1.0 / AD, 0.0).astype(BF16)
    merge_w = (g1, g2, gmh, wl, w_br_m[l].astype(BF16), w_br_a[l].astype(BF16), w_out[l].astype(BF16),
               w_router[l], b_router[l].reshape(1, NE), ws_gate[l].astype(BF16), ws_up[l].astype(BF16),
               ws_down[l].astype(BF16))
    slopes = 2.0 ** (-8.0 * jnp.arange(1, AH + 1, dtype=F32) / AH)

    tm_p = _pick_tile(t, 512)
    xp2 = x_prompt.reshape(n_p, D)
    xs2 = x_sample.reshape(n_s, D)
    mq_p, mk_p, mv_p, gt_p, qa_p, ka_p, va_p, kb_p, vt_p = _inproj(
        xp2, mod_p, False, tm_p, t // tm_p, g1, wm, wg, gq, gk, bd, True)
    mq_s, mk_s, mv_s, gt_s, qa_s, ka_s, va_s = _inproj(
        xs2, mod_s, True, tm_s, 1, g1, wm, wg, gq, gk, bd, False)

    zeros = functools.partial(jnp.zeros, dtype=F32)
    gp = gt_p.reshape(bsz, t, 2 * MH)
    hm_p, c_p, nn_p, m_p = _mlstm(
        mq_p.reshape(bsz, t, MW), mk_p.reshape(bsz, t, MW), mv_p.reshape(bsz, t, MW),
        gp.transpose(0, 2, 1), gp, b_gates[l], zeros((bsz, MH, MD, MD)), zeros((bsz, MH, MD)),
        zeros((bsz, MH, MD)), _pick_tile(bsz, 2), t, MCHUNK)
    ls = 16
    assert ts <= ls
    pad_t = lambda a: jnp.pad(a.reshape(db, ts, -1), ((0, 0), (0, ls - ts), (0, 0)))
    gs_ = pad_t(gt_s)
    m0 = jnp.broadcast_to(state_mlstm_m[l].astype(F32)[:, :, None], (db, MH, MD))
    hm_s, c_sm, nn_s, m_sm = _mlstm(
        pad_t(mq_s), pad_t(mk_s), pad_t(mv_s), gs_.transpose(0, 2, 1), gs_, b_gates[l],
        state_mlstm_C[l].astype(F32), state_mlstm_n[l].astype(F32), m0, _pick_tile(db, 8), ts, ls)
    hm_s = hm_s[:, :ts].reshape(n_s, MW)

    nb = t // BLK
    km = _kmean(ka_p)
    ha_p = _moba_prompt(qa_p.reshape(bsz, t, AW), kb_p.reshape(bsz, nb, BLK, AW), vt_p.reshape(bsz, nb, AW, BLK),
                        km.reshape(bsz, nb, AW), slopes).reshape(n_p, AW)
    qh = qa_s.astype(F32).reshape(db, ts, AH, AD).transpose(0, 2, 1, 3)
    qbd = (qh[:, :, :, None, :] * jnp.eye(AH, dtype=F32)[None, :, None, :, None]).reshape(db, AH * ts, AW)
    tn = 8
    pad_n = lambda a: jnp.pad(a.reshape(db, ts, AW), ((0, 0), (0, tn - ts), (0, 0)))
    to_t = lambda c: jnp.transpose(c[l], (0, 2, 3, 1))
    ha_s = _moba_sample_m(page_table, to_t(cache_k), to_t(cache_v), qbd, pad_n(ka_s), pad_n(va_s), ts)
    ha_s = ha_s.reshape(n_s, AW)

    tmm_p = _pick_tile(t, 256)
    base_p, h2_p, idx_p, wt_p, rk_p, cnt1 = _merge(
        xp2, hm_p.reshape(n_p, MW), ha_p, mod_p, False, tmm_p, t // tmm_p, merge_w, zeros((1, NE)))
    base_s, h2_s, idx_s, wt_s, rk_s, cnt2 = _merge(
        xs2, hm_s, ha_s, mod_s, True, tm_s, 1, merge_w, cnt1)

    n_all = n_p + n_s
    cnt = cnt2.reshape(NE).astype(I32)
    padded = (cnt + GROUP - 1) // GROUP * GROUP
    pend = jnp.cumsum(padded)
    pstart = pend - padded
    n_blocks = -(-(n_all * TOPK + NE * (GROUP - 1)) // GROUP)
    n_used = (pend[-1] // GROUP).reshape(1)
    blk = jnp.minimum(jnp.arange(n_blocks, dtype=I32), n_used[0] - 1) * GROUP
    blk_e = jnp.minimum(jnp.sum((pend[None, :] <= blk[:, None]).astype(I32), axis=1), NE - 1)

    h2 = jnp.concatenate([h2_p, h2_s], axis=0)
    dest_p = _dest(idx_p, rk_p, pstart)
    dest_s = _dest(idx_s, rk_s, pstart)
    td = _pick_tile(n_all, 128)
    xs = _dispatch(pend, cnt, jnp.concatenate([dest_p, dest_s], axis=0), h2, n_blocks * GROUP, td)
    ys = _experts(blk_e, n_used, xs, w_gate[l], w_up[l], w_down[l])
    tc_p = _pick_tile(t, 64)
    y_p = _combine(dest_p, ys, wt_p, base_p, mod_p, False, tc_p, t // tc_p)
    tc_s = _pick_tile(tm_s, 64)
    mod_sc = mod_s.reshape(n_s // tc_s, tc_s, 6 * D)
    y_s = _combine(dest_s, ys, wt_s, base_s, mod_sc, True, tc_s, 1)

    st = lambda a: a[None]
    return (y_p.reshape(bsz, t, D), y_s.reshape(db, ts, D),
            st(ka_p.reshape(bsz, t, AH, AD)), st(va_p.reshape(bsz, t, AH, AD)),
            st(c_p), st(nn_p), st(m_p[:, :, 0]),
            st(ka_s.reshape(db, ts, AH, AD)), st(va_s.reshape(db, ts, AH, AD)),
            st(c_sm), st(nn_s), st(m_sm[:, :, 0]))
```

```python
import functools

import jax
import jax.numpy as jnp
from jax import lax
from jax.experimental import pallas as pl
from jax.experimental.pallas import tpu as pltpu

F32, BF16, I32 = jnp.float32, jnp.bfloat16, jnp.int32

D = 1024
MH, MD = 4, 128
AH, AD = 8, 64
MW, AW = MH * MD, AH * AD
MCHUNK = 128
BLK = 256
TOPB = 3
NE, TOPK, DE, DSH = 256, 8, 256, 256
ROUTE_SCALE = 2.5
GROUP = 256
EPS = 1e-6
NEG = -1e30
MASKV = -3.0e38
VMEM_LIMIT = 56 * 1024 * 1024
_OFF = {}
_o = 0
for _n, _w in (("mq", MW), ("mk", MW), ("mv", MW), ("mo", MW), ("mi", MH), ("mf", MH),
               ("aq", AW), ("ak", AW), ("av", AW), ("ga", D), ("gb", D)):
    _OFF[_n] = (_o, _o + _w)
    _o += _w


def _cparams(*sem):
    return pltpu.CompilerParams(dimension_semantics=sem, vmem_limit_bytes=VMEM_LIMIT)


def _mm(a, b):
    return jnp.dot(a.astype(BF16), b.astype(BF16), preferred_element_type=F32)


def _mm_nt(a, b):
    return lax.dot_general(a.astype(BF16), b.astype(BF16), (((1,), (1,)), ((), ())),
                           preferred_element_type=F32)


def _mm_tn(a, b):
    return lax.dot_general(a.astype(BF16), b.astype(BF16), (((0,), (0,)), ((), ())),
                           preferred_element_type=F32)


def _split2(x):
    hi = x.astype(BF16)
    return hi, (x - hi.astype(F32)).astype(BF16)


def _mm_x3(a, b):
    ah, al = _split2(a)
    bh, bl = _split2(b)
    d = functools.partial(jnp.dot, preferred_element_type=F32)
    return d(ah, bh) + d(al, bh) + d(ah, bl)


def _mm_nt_x2(a_bf16, b):
    bh, bl = _split2(b)
    return _mm_nt(a_bf16, bh) + _mm_nt(a_bf16, bl)


def _rms(x, g):
    return x * lax.rsqrt(jnp.mean(x * x, axis=-1, keepdims=True) + EPS) * g


def _silu(x):
    return x * jax.nn.sigmoid(x)


def _logsig(x):
    return jnp.minimum(x, 0.0) - jnp.log1p(jnp.exp(-jnp.abs(x)))


LANES = 128
RT = D // LANES


def _to_row_tiles(ref, x, idx=()):
    rows = x.shape[0]
    for s in range(RT):
        ref[idx + (pl.ds(s, rows, stride=RT), slice(None))] = x[:, s * LANES:(s + 1) * LANES]


def _from_row_tiles(ref, rows, idx=()):
    return jnp.concatenate([ref[idx + (pl.ds(s, rows, stride=RT), slice(None))] for s in range(RT)], axis=1)


def _ada_kernel(c_ref, w_ref, b_ref, o_ref):
    o_ref[...] = _mm_x3(_silu(c_ref[...]), w_ref[...]) + b_ref[...]


def _ada(c_all, w_ada, b_ada):
    r = c_all.shape[0]
    return pl.pallas_call(
        _ada_kernel, grid=(6,),
        in_specs=[pl.BlockSpec((r, D), lambda j: (0, 0)),
                  pl.BlockSpec((D, D), lambda j: (0, j)),
                  pl.BlockSpec((1, D), lambda j: (0, j))],
        out_specs=pl.BlockSpec((r, D), lambda j: (0, j)),
        out_shape=jax.ShapeDtypeStruct((r, 6 * D), F32),
        compiler_params=_cparams("parallel"), name="ada")(c_all, w_ada, b_ada.reshape(1, 6 * D))


def _mod_spec(per_token, tm, tiles_per_batch, chunk):
    if per_token:
        return pl.BlockSpec((1, tm, D), lambda i, *_: (i, 0, chunk))
    return pl.BlockSpec((1, 1, D), lambda i, *_: (i // tiles_per_batch, 0, chunk))


def _group_ms(x, bd):
    hi, lo = _split2(x * x)
    d = functools.partial(jnp.dot, preferred_element_type=F32)
    return d(hi, bd) + d(lo, bd)


def _inproj_kernel(x_ref, sh_ref, sc_ref, g1_ref, wm_ref, wg_ref, gq_ref, gk_ref, bd_ref,
                   mq_ref, mk_ref, mv_ref, gt_ref, qa_ref, ka_ref, va_ref, *attn_refs):
    h = _rms(x_ref[...], g1_ref[...]) * (1.0 + sc_ref[0]) + sh_ref[0]
    hb = h.astype(BF16)
    d = functools.partial(jnp.dot, preferred_element_type=F32)
    mq_ref[...] = d(hb, wm_ref[:, 0:MW]).astype(BF16)
    mk_ref[...] = (d(hb, wm_ref[:, MW:2 * MW]) * (MD ** -0.5)).astype(BF16)
    mv_ref[...] = d(hb, wm_ref[:, 2 * MW:3 * MW]).astype(BF16)
    o = 3 * MW
    aq = d(hb, wm_ref[:, o:o + AW])
    ak = d(hb, wm_ref[:, o + AW:o + 2 * AW])
    av = d(hb, wm_ref[:, o + 2 * AW:o + 3 * AW])
    bd = bd_ref[...]
    qa = aq * lax.rsqrt(_group_ms(aq, bd) + EPS) * gq_ref[...]
    ka = ak * lax.rsqrt(_group_ms(ak, bd) + EPS) * gk_ref[...]
    qa_ref[...] = qa.astype(BF16)
    ka_ref[...] = ka
    va_ref[...] = av
    gt_ref[...] = d(hb, wg_ref[...])[:, 0:2 * MH]
    if attn_refs:
        kb_ref, vt_ref = attn_refs
        for r in range(vt_ref.shape[0]):
            kb_ref[r] = ka[r * BLK:(r + 1) * BLK, :].astype(BF16)
            vt_ref[r] = av[r * BLK:(r + 1) * BLK, :].T.astype(BF16)


def _inproj(x2, mod3, per_token, tm, tiles_per_batch, g1, wm, wg, gq, gk, bd, attn_layouts):
    n = x2.shape[0]
    row = lambda w: pl.BlockSpec((tm, w), lambda i: (i, 0))
    full = lambda a: pl.BlockSpec(a.shape, lambda i: (0,) * a.ndim)
    sds = jax.ShapeDtypeStruct
    out_specs = [row(MW), row(MW), row(MW), row(2 * MH), row(AW), row(AW), row(AW)]
    out_shape = [sds((n, MW), BF16), sds((n, MW), BF16), sds((n, MW), BF16), sds((n, 2 * MH), F32),
                 sds((n, AW), BF16), sds((n, AW), F32), sds((n, AW), F32)]
    if attn_layouts:
        out_specs += [pl.BlockSpec((tm // BLK, BLK, AW), lambda i: (i, 0, 0)),
                      pl.BlockSpec((tm // BLK, AW, BLK), lambda i: (i, 0, 0))]
        out_shape += [sds((n // BLK, BLK, AW), BF16), sds((n // BLK, AW, BLK), BF16)]
    return pl.pallas_call(
        _inproj_kernel, grid=(n // tm,),
        in_specs=[row(D), _mod_spec(per_token, tm, tiles_per_batch, 0),
                  _mod_spec(per_token, tm, tiles_per_batch, 1),
                  full(g1), full(wm), full(wg), full(gq), full(gk), full(bd)],
        out_specs=out_specs, out_shape=out_shape,
        compiler_params=_cparams("parallel"), name="inproj")(x2, mod3, mod3, g1, wm, wg, gq, gk, bd)


def _mlstm_kernel(q_ref, k_ref, v_ref, gr_ref, gc_ref, bgc_ref, bgr_ref, c0_ref, n0_ref, m0_ref,
                  h_ref, cn_ref, nn_ref, mn_ref, c_s, n_s, m_s, *, bb, t_valid, L):
    c = pl.program_id(1)

    @pl.when(c == 0)
    def _init():
        c_s[...] = c0_ref[...]
        n_s[...] = n0_ref[...]
        m_s[...] = m0_ref[...]

    rowi = lax.broadcasted_iota(I32, (L, L), 0)
    coli = lax.broadcasted_iota(I32, (L, L), 1)
    tri = rowi >= coli
    ok_r = (c * L + lax.broadcasted_iota(I32, (1, L), 1)) < t_valid
    ok_c = (c * L + lax.broadcasted_iota(I32, (L, 1), 0)) < t_valid
    for b in range(bb):
        g_r = gr_ref[b] + bgc_ref[...]
        g_c = gc_ref[b] + bgr_ref[...]
        for h in range(MH):
            li_r = jnp.where(ok_r, g_r[h:h + 1, :], NEG)
            lf_r = jnp.where(ok_r, _logsig(g_r[MH + h:MH + h + 1, :]), 0.0)
            li_c = jnp.where(ok_c, g_c[:, h:h + 1], NEG)
            lf_c = jnp.where(ok_c, _logsig(g_c[:, MH + h:MH + h + 1]), 0.0)
            b_c = jnp.sum(jnp.where(tri, lf_r, 0.0), axis=1, keepdims=True)
            b_r = jnp.sum(jnp.where(rowi <= coli, lf_c, 0.0), axis=0, keepdims=True)
            q = q_ref[b, :, h * MD:(h + 1) * MD]
            k = k_ref[b, :, h * MD:(h + 1) * MD]
            v = v_ref[b, :, h * MD:(h + 1) * MD]
            cm = c_s[b, h]
            nv = n_s[b, h:h + 1, :]
            m_prev = m_s[b, h:h + 1, 0:1]
            inter = m_prev + b_c
            intra = jnp.where(tri, li_r + b_c - b_r, NEG)
            m_t = jnp.maximum(inter, jnp.max(intra, axis=1, keepdims=True))
            w_inter = jnp.exp(inter - m_t)
            s = _mm_nt(q, k) * jnp.exp(intra - m_t)
            num = w_inter * _mm_nt(q, cm) + _mm(s, v)
            qn = jnp.sum(q.astype(F32) * nv, axis=1, keepdims=True)
            den = w_inter * qn + jnp.sum(s, axis=1, keepdims=True)
            h_ref[b, :, h * MD:(h + 1) * MD] = num / jnp.maximum(jnp.abs(den), jnp.exp(-m_t))
            m_last = m_t[L - 1:L, :]
            b_last = b_c[L - 1:L, :]
            w_c = jnp.exp(m_prev + b_last - m_last)
            w_s = jnp.exp(li_c + b_last - b_c - m_last)
            c_s[b, h] = w_c * cm + _mm_tn(v.astype(F32) * w_s, k)
            n_s[b, h:h + 1, :] = w_c * nv + jnp.sum(k.astype(F32) * w_s, axis=0, keepdims=True)
            m_s[b, h:h + 1, :] = jnp.broadcast_to(m_last, (1, MD))

    @pl.when(c == pl.num_programs(1) - 1)
    def _fin():
        cn_ref[...] = c_s[...]
        nn_ref[...] = n_s[...]
        mn_ref[...] = m_s[...]


def _mlstm(q, k, v, g_row, g_col, b_gates, c0, n0, m0, bb, t_valid, L):
    bsz, t, _ = q.shape
    seq = pl.BlockSpec((bb, L, MW), lambda g, c: (g, c, 0))
    st4 = pl.BlockSpec((bb, MH, MD, MD), lambda g, c: (g, 0, 0, 0))
    st3 = pl.BlockSpec((bb, MH, MD), lambda g, c: (g, 0, 0))
    sds = jax.ShapeDtypeStruct
    return pl.pallas_call(
        functools.partial(_mlstm_kernel, bb=bb, t_valid=t_valid, L=L),
        grid=(bsz // bb, t // L),
        in_specs=[seq, seq, seq,
                  pl.BlockSpec((bb, 2 * MH, L), lambda g, c: (g, 0, c)),
                  pl.BlockSpec((bb, L, 2 * MH), lambda g, c: (g, c, 0)),
                  pl.BlockSpec((2 * MH, 1), lambda g, c: (0, 0)),
                  pl.BlockSpec((1, 2 * MH), lambda g, c: (0, 0)),
                  st4, st3, st3],
        out_specs=[seq, st4, st3, st3],
        out_shape=[sds((bsz, t, MW), F32), sds((bsz, MH, MD, MD), F32),
                   sds((bsz, MH, MD), F32), sds((bsz, MH, MD), F32)],
        scratch_shapes=[pltpu.VMEM((bb, MH, MD, MD), F32), pltpu.VMEM((bb, MH, MD), F32),
                        pltpu.VMEM((bb, MH, MD), F32)],
        compiler_params=_cparams("parallel", "arbitrary"), name="mlstm")(
            q, k, v, g_row, g_col, b_gates.reshape(2 * MH, 1), b_gates.reshape(1, 2 * MH), c0, n0, m0)


def _kmean_kernel(k_ref, o_ref):
    for r in range(o_ref.shape[0]):
        o_ref[r:r + 1, :] = jnp.mean(k_ref[r * BLK:(r + 1) * BLK, :], axis=0, keepdims=True)


def _kmean(ka):
    n = ka.shape[0]
    nblk = n // BLK
    r = 8 if nblk % 8 == 0 else nblk
    return pl.pallas_call(
        _kmean_kernel, grid=(nblk // r,),
        in_specs=[pl.BlockSpec((r * BLK, AW), lambda i: (i, 0))],
        out_specs=pl.BlockSpec((r, AW), lambda i: (i, 0)),
        out_shape=jax.ShapeDtypeStruct((nblk, AW), F32),
        compiler_params=_cparams("parallel"), name="kmean")(ka)


def _select_blocks(gate, own, n_sel):
    nb = gate.shape[1]
    col = lax.broadcasted_iota(I32, gate.shape, 1).astype(F32)
    ownf = jnp.asarray(own, F32) if not isinstance(own, int) else float(own)
    g = jnp.where(col < ownf, gate, NEG)
    sel = jnp.zeros(gate.shape, F32)
    for _ in range(n_sel):
        mx = jnp.max(g, axis=1, keepdims=True)
        idx = jnp.min(jnp.where(g == mx, col, float(nb)), axis=1, keepdims=True)
        hit = col == idx
        sel = jnp.where(hit & (idx < ownf), 1.0, sel)
        g = jnp.where(hit, MASKV, g)
    return sel


def _select_blocks_t(gate_t, own, n_sel):
    nb = gate_t.shape[0]
    row = lax.broadcasted_iota(I32, gate_t.shape, 0).astype(F32)
    g = jnp.where(row < own, gate_t, NEG)
    sel = jnp.zeros(gate_t.shape, F32)
    for _ in range(n_sel):
        mx = jnp.max(g, axis=0, keepdims=True)
        idx = jnp.min(jnp.where(g == mx, row, float(nb)), axis=0, keepdims=True)
        hit = row == idx
        sel = jnp.where(hit & (idx < own), 1.0, sel)
        g = jnp.where(hit, MASKV, g)
    return sel


def _moba_p_kernel(slope_ref, q_ref, k_ref, vt_ref, km_ref, o_ref, a_s, s_s, bias_s, qs_s, m_s, l_s, acc_s):
    b = pl.program_id(0)
    i = pl.program_id(1)
    nb = km_ref.shape[1]
    rel_t = lax.broadcasted_iota(I32, (BLK, BLK), 1) - lax.broadcasted_iota(I32, (BLK, BLK), 0)

    @pl.when((b == 0) & (i == 0))
    def _alibi():
        relf = rel_t.astype(F32)
        for h in range(AH):
            a_s[h] = slope_ref[h] * relf

    causal = rel_t >= 0
    own = i.astype(F32)
    for h in range(AH):
        hs = slice(h * AD, (h + 1) * AD)
        q = q_ref[0, :, hs]
        kmh, kml = _split2(km_ref[0, :, hs])
        gate_t = _mm_nt(kmh, q) + _mm_nt(kml, q)
        sel = _select_blocks_t(gate_t, own, min(TOPB, nb))
        bias_s[h] = jnp.where(sel > 0.5, 0.0, MASKV)
        qs = (q.astype(F32) * (AD ** -0.5)).astype(BF16)
        qs_s[h] = qs
        s_s[h] = jnp.where(causal, _mm_nt(k_ref[0, i, :, hs], qs) - a_s[h], MASKV)
    for h in range(AH):
        hs = slice(h * AD, (h + 1) * AD)
        m = jnp.max(s_s[h], axis=0, keepdims=True)
        p = jnp.exp(s_s[h] - m)
        m_s[h] = m
        l_s[h] = jnp.sum(p, axis=0, keepdims=True)
        acc_s[h] = _mm(vt_ref[0, i, hs, :], p)

    def body(j, c):
        dj = ((i - j) * BLK).astype(F32)
        for h in range(AH):
            hs = slice(h * AD, (h + 1) * AD)
            rowterm = bias_s[h, pl.ds(j, 1), :] - slope_ref[h] * dj
            s_s[h] = (_mm_nt(k_ref[0, j, :, hs], qs_s[h]) - a_s[h]) + rowterm
        for h in range(AH):
            hs = slice(h * AD, (h + 1) * AD)
            m = m_s[h]
            m_new = jnp.maximum(m, jnp.max(s_s[h], axis=0, keepdims=True))
            p = jnp.exp(s_s[h] - m_new)
            alpha = jnp.exp(m - m_new)
            l_s[h] = alpha * l_s[h] + jnp.sum(p, axis=0, keepdims=True)
            acc_s[h] = alpha * acc_s[h] + _mm(vt_ref[0, j, hs, :], p)
            m_s[h] = m_new
        return c

    lax.fori_loop(0, i, body, 0)
    out_t = jnp.concatenate([acc_s[h] / l_s[h] for h in range(AH)], axis=0)
    o_ref[0] = out_t.T


def _moba_prompt(q, kb, vt, km, slopes):
    bsz, t, _ = q.shape
    nb = t // BLK
    gs = pltpu.PrefetchScalarGridSpec(
        num_scalar_prefetch=1, grid=(bsz, nb),
        in_specs=[pl.BlockSpec((1, BLK, AW), lambda b, i, s: (b, i, 0)),
                  pl.BlockSpec((1, nb, BLK, AW), lambda b, i, s: (b, 0, 0, 0)),
                  pl.BlockSpec((1, nb, AW, BLK), lambda b, i, s: (b, 0, 0, 0)),
                  pl.BlockSpec((1, nb, AW), lambda b, i, s: (b, 0, 0))],
        out_specs=pl.BlockSpec((1, BLK, AW), lambda b, i, s: (b, i, 0)),
        scratch_shapes=[pltpu.VMEM((AH, BLK, BLK), F32), pltpu.VMEM((AH, BLK, BLK), F32),
                        pltpu.VMEM((AH, nb, BLK), F32),
                        pltpu.VMEM((AH, BLK, AD), BF16), pltpu.VMEM((AH, 1, BLK), F32),
                        pltpu.VMEM((AH, 1, BLK), F32), pltpu.VMEM((AH, AD, BLK), F32)])
    return pl.pallas_call(
        _moba_p_kernel, grid_spec=gs,
        out_shape=jax.ShapeDtypeStruct((bsz, t, AW), F32),
        compiler_params=_cparams("arbitrary", "arbitrary"), name="moba_p")(slopes, q, kb, vt, km)


def _moba_s_kernel(pt_ref, *refs, ppb, page, nbp, past, rq):
    k_pages = refs[0:ppb]
    v_pages = refs[ppb:2 * ppb]
    q_ref, kn_ref, vn_ref, o_ref, k_s, v_s, km_s = refs[2 * ppb:]
    b = pl.program_id(0)
    j = pl.program_id(1)
    lk = (nbp + 1) * BLK

    @pl.when((b == 0) & (j == 0))
    def _zero():
        k_s[past:lk, :] = jnp.zeros((BLK, AW), BF16)
        v_s[past:lk, :] = jnp.zeros((BLK, AW), BF16)
        km_s[...] = jnp.zeros(km_s.shape, F32)

    ksum = jnp.zeros((1, AW), F32)
    for p in range(ppb):
        kp = k_pages[p][0]
        off = pl.multiple_of(j * BLK + p * page, page)
        k_s[pl.ds(off, page), :] = kp
        v_s[pl.ds(off, page), :] = v_pages[p][0]
        ksum = ksum + jnp.sum(kp.astype(F32), axis=0, keepdims=True)
    km_s[pl.ds(j, 1), :] = ksum * (1.0 / BLK)

    @pl.when(j == nbp - 1)
    def _attend():
        k_s[past:past + rq, :] = kn_ref[0].astype(BF16)
        v_s[past:past + rq, :] = vn_ref[0].astype(BF16)
        rows = AH * rq
        q8 = q_ref[0]
        qt = jnp.concatenate([q8] * AH, axis=0)
        rowh = lax.broadcasted_iota(I32, (rows, AW), 0) // rq
        laneh = lax.broadcasted_iota(I32, (rows, AW), 1) // AD
        qe = jnp.where(rowh == laneh, qt, 0.0)
        qe_b = qe.astype(BF16)
        gate = _mm_nt_x2(qe_b, km_s[...])
        sel = _select_blocks(gate, nbp, min(TOPB, nbp + 1))
        qs = (qe * (AD ** -0.5)).astype(BF16)
        hrow = lax.broadcasted_iota(I32, (rows, 1), 0) // rq
        slope = jnp.zeros((rows, 1), F32)
        for hh in range(AH):
            slope = jnp.where(hrow == hh, 2.0 ** (-8.0 * (hh + 1) / AH), slope)
        rq_pos = past + lax.broadcasted_iota(I32, (rows, BLK), 0) % rq
        lane = lax.broadcasted_iota(I32, (rows, BLK), 1)
        segs = []
        m = jnp.full((rows, 1), NEG, F32)
        for cb in range(nbp + 1):
            s = _mm_nt(qs, k_s[cb * BLK:(cb + 1) * BLK, :])
            dist = rq_pos - (lane + cb * BLK)
            s = s - slope * dist.astype(F32)
            ok = dist >= 0
            if cb < nbp:
                ok = ok & (sel[:, cb:cb + 1] > 0.5)
            s = jnp.where(ok, s, MASKV)
            segs.append(s)
            m = jnp.maximum(m, jnp.max(s, axis=1, keepdims=True))
        l = jnp.zeros((rows, 1), F32)
        acc = jnp.zeros((rows, AW), F32)
        for cb in range(nbp + 1):
            p = jnp.exp(segs[cb] - m)
            l = l + jnp.sum(p, axis=1, keepdims=True)
            acc = acc + _mm(p, v_s[cb * BLK:(cb + 1) * BLK, :])
        acc = acc / l
        out = jnp.zeros((rq, AW), F32)
        lh = lax.broadcasted_iota(I32, (rq, AW), 1) // AD
        for hh in range(AH):
            out = out + jnp.where(lh == hh, acc[hh * rq:(hh + 1) * rq, :], 0.0)
        o_ref[0] = out


def _moba_sample(page_table, cache_k, cache_v, q, k_new, v_new):
    db, n_pages = page_table.shape
    page = cache_k.shape[1]
    rq = q.shape[1]
    past = n_pages * page
    assert past % BLK == 0 and BLK % page == 0
    ppb = BLK // page
    nbp = past // BLK

    def page_spec(p):
        return pl.BlockSpec((1, page, AW), lambda b, j, pt: (pt[b * n_pages + j * ppb + p], 0, 0))

    tok = pl.BlockSpec((1, rq, AW), lambda b, j, pt: (b, 0, 0))
    gs = pltpu.PrefetchScalarGridSpec(
        num_scalar_prefetch=1, grid=(db, nbp),
        in_specs=[page_spec(p) for p in range(ppb)] * 2 + [tok, tok, tok],
        out_specs=tok,
        scratch_shapes=[pltpu.VMEM(((nbp + 1) * BLK, AW), BF16), pltpu.VMEM(((nbp + 1) * BLK, AW), BF16),
                        pltpu.VMEM((max(8, -(-(nbp + 1) // 8) * 8), AW), F32)])
    return pl.pallas_call(
        functools.partial(_moba_s_kernel, ppb=ppb, page=page, nbp=nbp, past=past, rq=rq),
        grid_spec=gs, out_shape=jax.ShapeDtypeStruct((db, rq, AW), F32),
        compiler_params=_cparams("arbitrary", "arbitrary"), name="moba_s")(
            page_table.reshape(-1), *([cache_k] * ppb), *([cache_v] * ppb), q, k_new, v_new)


def _moba_sn_kernel(pt_ref, *refs, ppb, page, nbp, past, ts):
    k_pages = refs[0:ppb]
    v_pages = refs[ppb:2 * ppb]
    q_ref, kn_ref, vn_ref, o_ref, s_s, v_s, km_s = refs[2 * ppb:]
    j = pl.program_id(1)
    rows = AH * ts
    pl_rows = page * AH
    scale = AD ** -0.5
    qx = q_ref[0]
    qs = (qx * scale).astype(BF16)

    ksum = jnp.zeros((AH, AD), F32)
    for p in range(ppb):
        kp = k_pages[p][0]
        pg = j * ppb + p
        s_s[pg] = _mm_nt(qs, kp.reshape(pl_rows, AD))
        v_s[pg] = v_pages[p][0].reshape(pl_rows, AD).astype(BF16)
        ksum = ksum + jnp.sum(kp, axis=0)
    km_s[j] = ksum * (1.0 / BLK)

    @pl.when(j == nbp - 1)
    def _attend():
        rh = lax.broadcasted_iota(I32, (rows, 1), 0) // ts
        rr = lax.broadcasted_iota(I32, (rows, 1), 0) % ts
        slope = jnp.zeros((rows, 1), F32)
        for hh in range(AH):
            slope = jnp.where(rh == hh, 2.0 ** (-8.0 * (hh + 1) / AH), slope)
        ncol = nbp * AH
        gate = _mm_nt_x2(qx.astype(BF16), km_s[...].reshape(ncol, AD))
        gcol = lax.broadcasted_iota(I32, (rows, ncol), 1)
        colf = gcol.astype(F32)
        g = jnp.where(gcol % AH == rh, gate, MASKV)
        picks = []
        for _ in range(min(TOPB, nbp)):
            mx = jnp.max(g, axis=1, keepdims=True)
            idx = jnp.min(jnp.where(g == mx, colf, float(ncol)), axis=1, keepdims=True)
            picks.append(jnp.floor(idx * (1.0 / AH)))
            g = jnp.where(colf == idx, MASKV, g)
        lane = lax.broadcasted_iota(I32, (rows, pl_rows), 1)
        head_ok = lane % AH == rh
        kpos = (lane // AH).astype(F32)
        qpos = (past + rr).astype(F32)
        lane_n = lax.broadcasted_iota(I32, (rows, ts * AH), 1)
        dist_n = rr - lane_n // AH
        s_new = _mm_nt(qs, kn_ref[0]) - slope * dist_n.astype(F32)
        s_new = jnp.where((lane_n % AH == rh) & (dist_n >= 0), s_new, MASKV)
        m = jnp.max(s_new, axis=1, keepdims=True)
        n_pg = nbp * ppb
        for pg in range(n_pg):
            blk = float(pg // ppb)
            chosen = picks[0] == blk
            for c in picks[1:]:
                chosen = chosen | (c == blk)
            s = s_s[pg] - slope * (qpos - (kpos + float(pg * page)))
            s = jnp.where(head_ok & chosen, s, MASKV)
            s_s[pg] = s
            m = jnp.maximum(m, jnp.max(s, axis=1, keepdims=True))
        p_new = jnp.exp(s_new - m)
        l = jnp.sum(p_new, axis=1, keepdims=True)
        acc = _mm(p_new, vn_ref[0])
        for pg in range(n_pg):
            p = jnp.exp(s_s[pg] - m)
            l = l + jnp.sum(p, axis=1, keepdims=True)
            acc = acc + _mm(p, v_s[pg])
        o_ref[0] = acc / l


def _moba_sample_native(page_table, cache_k, cache_v, qx, k_new, v_new, ts):
    db, n_pages = page_table.shape
    page = cache_k.shape[1]
    past = n_pages * page
    assert past % BLK == 0 and BLK % page == 0
    ppb = BLK // page
    nbp = past // BLK
    rows = AH * ts

    def page_spec(p):
        return pl.BlockSpec((1, page, AH, AD), lambda b, j, pt: (pt[b * n_pages + j * ppb + p], 0, 0, 0))

    tok = pl.BlockSpec((1, rows, AD), lambda b, j, pt: (b, 0, 0))
    gs = pltpu.PrefetchScalarGridSpec(
        num_scalar_prefetch=1, grid=(db, nbp),
        in_specs=[page_spec(p) for p in range(ppb)] * 2 + [tok, tok, tok],
        out_specs=tok,
        scratch_shapes=[pltpu.VMEM((n_pages, rows, page * AH), F32), pltpu.VMEM((n_pages, page * AH, AD), BF16),
                        pltpu.VMEM((nbp, AH, AD), F32)])
    return pl.pallas_call(
        functools.partial(_moba_sn_kernel, ppb=ppb, page=page, nbp=nbp, past=past, ts=ts),
        grid_spec=gs, out_shape=jax.ShapeDtypeStruct((db, rows, AD), F32),
        compiler_params=_cparams("arbitrary", "arbitrary"), name="moba_s")(
            page_table.reshape(-1), *([cache_k] * ppb), *([cache_v] * ppb), qx, k_new, v_new)


def _moba_st_kernel(pt_ref, *refs, ppb, page, nbp, past, ts, tn):
    k_pages = refs[0:ppb]
    v_pages = refs[ppb:2 * ppb]
    qbd_ref, kn_ref, vn_ref, o_ref, s_s, v_s, g_s = refs[2 * ppb:]
    j = pl.program_id(1)
    rows = AH * ts
    qsf = qbd_ref[0] * (AD ** -0.5)
    qs = qsf.astype(BF16)

    gsum = jnp.zeros((rows, page), F32)
    for p in range(ppb):
        pg = j * ppb + p
        s = _mm(qs, k_pages[p][0].reshape(AW, page))
        s_s[pg] = s
        gsum = gsum + s
        v_s[pg] = v_pages[p][0].reshape(AW, page).astype(BF16)
    g_s[j] = gsum

    @pl.when(j == nbp - 1)
    def _attend():
        rh = lax.broadcasted_iota(I32, (rows, 1), 0) // ts
        rr = lax.broadcasted_iota(I32, (rows, 1), 0) % ts
        slope = jnp.zeros((rows, 1), F32)
        for hh in range(AH):
            slope = jnp.where(rh == hh, 2.0 ** (-8.0 * (hh + 1) / AH), slope)
        g = [jnp.sum(g_s[jb], axis=1, keepdims=True) for jb in range(nbp)]
        picks = []
        for _ in range(min(TOPB, nbp)):
            mx = g[0]
            for jb in range(1, nbp):
                mx = jnp.maximum(mx, g[jb])
            idx = jnp.full((rows, 1), float(nbp), F32)
            for jb in reversed(range(nbp)):
                idx = jnp.where(g[jb] == mx, float(jb), idx)
            picks.append(idx)
            g = [jnp.where(idx == float(jb), MASKV, g[jb]) for jb in range(nbp)]
        lane = lax.broadcasted_iota(I32, (rows, page), 1).astype(F32)
        qpos = (past + rr).astype(F32)
        lane_n = lax.broadcasted_iota(I32, (rows, tn), 1)
        dist_n = rr - lane_n
        s_new = lax.dot_general(qsf, kn_ref[0], (((1,), (1,)), ((), ())), preferred_element_type=F32)
        s_new = jnp.where(dist_n >= 0, s_new - slope * dist_n.astype(F32), MASKV)
        m = jnp.max(s_new, axis=1, keepdims=True)
        n_pg = nbp * ppb
        for pg in range(n_pg):
            blk = float(pg // ppb)
            chosen = picks[0] == blk
            for c in picks[1:]:
                chosen = chosen | (c == blk)
            s = s_s[pg] - slope * (qpos - (lane + float(pg * page)))
            s = jnp.where(chosen, s, MASKV)
            s_s[pg] = s
            m = jnp.maximum(m, jnp.max(s, axis=1, keepdims=True))
        p_new = jnp.exp(s_new - m)
        l = jnp.sum(p_new, axis=1, keepdims=True)
        acc = jnp.dot(p_new, vn_ref[0], preferred_element_type=F32)
        for pg in range(n_pg):
            p = jnp.exp(s_s[pg] - m)
            l = l + jnp.sum(p, axis=1, keepdims=True)
            acc = acc + _mm_nt(p, v_s[pg])
        acc = acc / l
        lh = lax.broadcasted_iota(I32, (ts, AW), 1) // AD
        out = jnp.zeros((ts, AW), F32)
        for hh in range(AH):
            out = out + jnp.where(lh == hh, acc[hh * ts:(hh + 1) * ts, :], 0.0)
        o_ref[0] = out


def _moba_sample_t(page_table, cache_kt, cache_vt, qbd, k_new, v_new, ts):
    db, n_pages = page_table.shape
    page = cache_kt.shape[3]
    past = n_pages * page
    assert past % BLK == 0 and BLK % page == 0
    ppb = BLK // page
    nbp = past // BLK
    rows = AH * ts
    tn = k_new.shape[1]

    def page_spec(p):
        return pl.BlockSpec((1, AH, AD, page), lambda b, j, pt: (pt[b * n_pages + j * ppb + p], 0, 0, 0))

    gs = pltpu.PrefetchScalarGridSpec(
        num_scalar_prefetch=1, grid=(db, nbp),
        in_specs=[page_spec(p) for p in range(ppb)] * 2 + [
            pl.BlockSpec((1, rows, AW), lambda b, j, pt: (b, 0, 0)),
            pl.BlockSpec((1, tn, AW), lambda b, j, pt: (b, 0, 0)),
            pl.BlockSpec((1, tn, AW), lambda b, j, pt: (b, 0, 0))],
        out_specs=pl.BlockSpec((1, ts, AW), lambda b, j, pt: (b, 0, 0)),
        scratch_shapes=[pltpu.VMEM((n_pages, rows, page), F32), pltpu.VMEM((n_pages, AW, page), BF16),
                        pltpu.VMEM((nbp, rows, page), F32)])
    return pl.pallas_call(
        functools.partial(_moba_st_kernel, ppb=ppb, page=page, nbp=nbp, past=past, ts=ts, tn=tn),
        grid_spec=gs, out_shape=jax.ShapeDtypeStruct((db, ts, AW), F32),
        compiler_params=_cparams("arbitrary", "arbitrary"), name="moba_s")(
            page_table.reshape(-1), *([cache_kt] * ppb), *([cache_vt] * ppb), qbd, k_new, v_new)


def _moba_sm_kernel(pt_ref, qbd_ref, kn_ref, vn_ref, ck_ref, cv_ref, o_ref, kbuf, vbuf, s_s, sem,
                    *, n_pages, page, nbp, past, ts, tn):
    b = pl.program_id(0)
    slot = b % 2
    ppb = n_pages // nbp
    rows = AH * ts

    def page_copies(seq, sl, pg):
        pid = pt_ref[seq * n_pages + pg]
        return (pltpu.make_async_copy(ck_ref.at[pid], kbuf.at[sl, pg], sem.at[sl, 0]),
                pltpu.make_async_copy(cv_ref.at[pid], vbuf.at[sl, pg], sem.at[sl, 1]))

    def fetch(seq, sl):
        for pg in range(n_pages):
            for cp in page_copies(seq, sl, pg):
                cp.start()

    @pl.when(b == 0)
    def _first():
        fetch(b, slot)

    @pl.when(b + 1 < pl.num_programs(0))
    def _next():
        fetch(b + 1, 1 - slot)

    for pg in range(n_pages):
        for cp in page_copies(b, slot, pg):
            cp.wait()

    qsf = qbd_ref[0] * (AD ** -0.5)
    qs = qsf.astype(BF16)
    g = []
    for jb in range(nbp):
        gsum = jnp.zeros((rows, page), F32)
        for p in range(ppb):
            pg = jb * ppb + p
            s = _mm(qs, kbuf[slot, pg].reshape(AW, page))
            s_s[pg] = s
            gsum = gsum + s
        g.append(jnp.sum(gsum, axis=1, keepdims=True))
    rh = lax.broadcasted_iota(I32, (rows, 1), 0) // ts
    rr = lax.broadcasted_iota(I32, (rows, 1), 0) % ts
    slope = jnp.zeros((rows, 1), F32)
    for hh in range(AH):
        slope = jnp.where(rh == hh, 2.0 ** (-8.0 * (hh + 1) / AH), slope)
    picks = []
    for _ in range(min(TOPB, nbp)):
        mx = g[0]
        for jb in range(1, nbp):
            mx = jnp.maximum(mx, g[jb])
        idx = jnp.full((rows, 1), float(nbp), F32)
        for jb in reversed(range(nbp)):
            idx = jnp.where(g[jb] == mx, float(jb), idx)
        picks.append(idx)
        g = [jnp.where(idx == float(jb), MASKV, g[jb]) for jb in range(nbp)]
    lane = lax.broadcasted_iota(I32, (rows, page), 1).astype(F32)
    qpos = (past + rr).astype(F32)
    dist_n = rr - lax.broadcasted_iota(I32, (rows, tn), 1)
    s_new = lax.dot_general(qsf, kn_ref[0], (((1,), (1,)), ((), ())), preferred_element_type=F32)
    s_new = jnp.where(dist_n >= 0, s_new - slope * dist_n.astype(F32), MASKV)
    m = jnp.max(s_new, axis=1, keepdims=True)
    for pg in range(n_pages):
        blk = float(pg // ppb)
        chosen = picks[0] == blk
        for c in picks[1:]:
            chosen = chosen | (c == blk)
        s = jnp.where(chosen, s_s[pg] - slope * (qpos - (lane + float(pg * page))), MASKV)
        s_s[pg] = s
        m = jnp.maximum(m, jnp.max(s, axis=1, keepdims=True))
    p_new = jnp.exp(s_new - m)
    l = jnp.sum(p_new, axis=1, keepdims=True)
    acc = jnp.dot(p_new, vn_ref[0], preferred_element_type=F32)
    for pg in range(n_pages):
        p = jnp.exp(s_s[pg] - m)
        l = l + jnp.sum(p, axis=1, keepdims=True)
        acc = acc + _mm_nt(p, vbuf[slot, pg].reshape(AW, page))
    acc = acc / l
    lh = lax.broadcasted_iota(I32, (ts, AW), 1) // AD
    out = jnp.zeros((ts, AW), F32)
    for hh in range(AH):
        out = out + jnp.where(lh == hh, acc[hh * ts:(hh + 1) * ts, :], 0.0)
    o_ref[0] = out


def _moba_sample_m(page_table, cache_kt, cache_vt, qbd, k_new, v_new, ts):
    db, n_pages = page_table.shape
    page = cache_kt.shape[3]
    past = n_pages * page
    assert past % BLK == 0 and BLK % page == 0
    nbp = past // BLK
    rows = AH * ts
    tn = k_new.shape[1]
    gs = pltpu.PrefetchScalarGridSpec(
        num_scalar_prefetch=1, grid=(db,),
        in_specs=[pl.BlockSpec((1, rows, AW), lambda b, pt: (b, 0, 0)),
                  pl.BlockSpec((1, tn, AW), lambda b, pt: (b, 0, 0)),
                  pl.BlockSpec((1, tn, AW), lambda b, pt: (b, 0, 0)),
                  pl.BlockSpec(memory_space=pl.ANY), pl.BlockSpec(memory_space=pl.ANY)],
        out_specs=pl.BlockSpec((1, ts, AW), lambda b, pt: (b, 0, 0)),
        scratch_shapes=[pltpu.VMEM((2, n_pages, AH, AD, page), F32), pltpu.VMEM((2, n_pages, AH, AD, page), F32),
                        pltpu.VMEM((n_pages, rows, page), F32), pltpu.SemaphoreType.DMA((2, 2))])
    return pl.pallas_call(
        functools.partial(_moba_sm_kernel, n_pages=n_pages, page=page, nbp=nbp, past=past, ts=ts, tn=tn),
        grid_spec=gs, out_shape=jax.ShapeDtypeStruct((db, ts, AW), F32),
        compiler_params=_cparams("arbitrary"), name="moba_s")(
            page_table.reshape(-1), qbd, k_new, v_new, cache_kt, cache_vt)


def _merge_kernel(x_ref, hm_ref, ha_ref, sh1_ref, sc1_ref, gt1_ref, sh2_ref, sc2_ref, gt2_ref,
                  g1_ref, g2_ref, gmh_ref, wl_ref, wbm_ref, wba_ref, wo_ref, wr_ref, br_ref,
                  wsg_ref, wsu_ref, wsd_ref, cin_ref,
                  base_ref, h2_ref, idx_ref, wt_ref, rank_ref, cout_ref, cnt_s):
    i = pl.program_id(0)
    tm = x_ref.shape[0]

    @pl.when(i == 0)
    def _init():
        cnt_s[...] = cin_ref[...]

    x = x_ref[...]
    h1 = (_rms(x, g1_ref[...]) * (1.0 + sc1_ref[0]) + sh1_ref[0]).astype(BF16)
    d = functools.partial(jnp.dot, preferred_element_type=F32)
    mo = d(h1, wl_ref[:, 0:MW])
    ga = d(h1, wl_ref[:, MW:MW + D])
    gb = d(h1, wl_ref[:, MW + D:MW + 2 * D])
    hm = hm_ref[...]
    parts = []
    for h in range(MH):
        xh = hm[:, h * MD:(h + 1) * MD]
        parts.append(_rms(xh, gmh_ref[:, h * MD:(h + 1) * MD]))
    hmn = jnp.concatenate(parts, axis=1) * jax.nn.sigmoid(mo)
    merged = jax.nn.sigmoid(ga) * _mm(hmn, wbm_ref[...]) + jax.nn.sigmoid(gb) * _mm(ha_ref[...], wba_ref[...])
    x1 = x + gt1_ref[0] * _mm(merged, wo_ref[...])
    h2 = _rms(x1, g2_ref[...]) * (1.0 + sc2_ref[0]) + sh2_ref[0]
    _to_row_tiles(h2_ref, h2)
    h2b = h2.astype(BF16)
    shared = _mm(_silu(d(h2b, wsg_ref[...])) * d(h2b, wsu_ref[...]), wsd_ref[...])
    base_ref[...] = x1 + gt2_ref[0] * shared
    scores = jax.nn.sigmoid(_mm_x3(h2, wr_ref[...]))
    col = lax.broadcasted_iota(I32, (tm, NE), 1).astype(F32)
    g = scores + br_ref[...]
    idxs, wts = [], []
    onehot = jnp.zeros((tm, NE), F32)
    for _ in range(TOPK):
        mx = jnp.max(g, axis=1, keepdims=True)
        idx = jnp.min(jnp.where(g == mx, col, float(NE)), axis=1, keepdims=True)
        hit = col == idx
        idxs.append(idx)
        wts.append(jnp.sum(jnp.where(hit, scores, 0.0), axis=1, keepdims=True))
        onehot = jnp.where(hit, 1.0, onehot)
        g = jnp.where(hit, MASKV, g)
    wsum = wts[0]
    for w in wts[1:]:
        wsum = wsum + w
    ri = lax.broadcasted_iota(I32, (tm, tm), 0)
    ci = lax.broadcasted_iota(I32, (tm, tm), 1)
    before = jnp.where(ci < ri, 1.0, 0.0).astype(BF16)
    pref = cnt_s[...] + d(before, onehot.astype(BF16))
    lane8 = lax.broadcasted_iota(I32, (tm, TOPK), 1)
    idx_o = jnp.zeros((tm, TOPK), F32)
    wt_o = jnp.zeros((tm, TOPK), F32)
    rk_o = jnp.zeros((tm, TOPK), F32)
    for kk in range(TOPK):
        rk = jnp.sum(jnp.where(col == idxs[kk], pref, 0.0), axis=1, keepdims=True)
        idx_o = jnp.where(lane8 == kk, idxs[kk], idx_o)
        wt_o = jnp.where(lane8 == kk, wts[kk] / wsum * ROUTE_SCALE, wt_o)
        rk_o = jnp.where(lane8 == kk, rk, rk_o)
    idx_ref[...] = idx_o.astype(I32)
    wt_ref[...] = wt_o
    rank_ref[...] = rk_o.astype(I32)
    cnt_s[...] = cnt_s[...] + jnp.sum(onehot, axis=0, keepdims=True)
    cout_ref[...] = cnt_s[...]


def _merge(x2, hm, ha, mod3, per_token, tm, tiles_per_batch, weights, cnt_in):
    n = x2.shape[0]
    row = lambda w: pl.BlockSpec((tm, w), lambda i: (i, 0))
    full = lambda a: pl.BlockSpec(a.shape, lambda i: (0,) * a.ndim)
    mods = [_mod_spec(per_token, tm, tiles_per_batch, c) for c in range(6)]
    sds = jax.ShapeDtypeStruct
    return pl.pallas_call(
        _merge_kernel, grid=(n // tm,),
        in_specs=[row(D), row(MW), row(AW)] + mods + [full(w) for w in weights] + [full(cnt_in)],
        out_specs=[row(D), pl.BlockSpec((tm * RT, LANES), lambda i: (i, 0)), row(TOPK), row(TOPK), row(TOPK),
                   full(cnt_in)],
        out_shape=[sds((n, D), F32), sds((n * RT, LANES), F32), sds((n, TOPK), I32), sds((n, TOPK), F32),
                   sds((n, TOPK), I32), sds((1, NE), F32)],
        scratch_shapes=[pltpu.VMEM((1, NE), F32)],
        compiler_params=_cparams("arbitrary"), name="merge")(
            x2, hm, ha, *([mod3] * 6), *weights, cnt_in)


def _dest_kernel(idx_ref, rank_ref, pstart_ref, o_ref):
    tm = idx_ref.shape[0]
    col = lax.broadcasted_iota(I32, (tm, NE), 1)
    lane8 = lax.broadcasted_iota(I32, (tm, TOPK), 1)
    idx = idx_ref[...]
    out = rank_ref[...]
    for kk in range(TOPK):
        start = jnp.sum(jnp.where(col == idx[:, kk:kk + 1], pstart_ref[...], 0.0), axis=1, keepdims=True)
        out = out + jnp.where(lane8 == kk, start.astype(I32), 0)
    o_ref[...] = out


def _dest(idx, rank, pstart):
    n = idx.shape[0]
    tm = _pick_tile(n, 512)
    row = pl.BlockSpec((tm, TOPK), lambda i: (i, 0))
    return pl.pallas_call(
        _dest_kernel, grid=(n // tm,),
        in_specs=[row, row, pl.BlockSpec((1, NE), lambda i: (0, 0))],
        out_specs=row, out_shape=jax.ShapeDtypeStruct((n, TOPK), I32),
        compiler_params=_cparams("parallel"), name="dest")(idx, rank, pstart.astype(F32).reshape(1, NE))


def _row_copy(src, dst, sem):
    return pltpu.make_async_copy(src, dst, sem)


def _dispatch_kernel(pend_ref, cnt_ref, dest_ref, h_ref, xs_ref, zbuf, ring, zsem, sem):
    step = pl.program_id(0)
    last = pl.num_programs(0) - 1
    td = h_ref.shape[0] // RT
    slot = step % 2

    @pl.when(step == 0)
    def _zero_tails():
        zbuf[...] = jnp.zeros(zbuf.shape, F32)

        def tail(e):
            first = pl.multiple_of((pend_ref[e] - GROUP) * RT, RT)
            return _row_copy(zbuf, xs_ref.at[pl.ds(first, GROUP * RT)], zsem)

        def start(e, c):
            @pl.when(cnt_ref[e] > 0)
            def _():
                tail(e).start()
            return c

        def wait(e, c):
            @pl.when(cnt_ref[e] > 0)
            def _():
                tail(e).wait()
            return c

        lax.fori_loop(0, NE, start, 0)
        lax.fori_loop(0, NE, wait, 0)

    ring[slot] = h_ref[...]

    def start(t, c):
        src = ring.at[slot, pl.ds(pl.multiple_of(t * RT, RT), RT)]
        for kk in range(TOPK):
            dst = pl.multiple_of(dest_ref[0, 0, t * TOPK + kk] * RT, RT)
            _row_copy(src, xs_ref.at[pl.ds(dst, RT)], sem.at[slot]).start(priority=kk % 2)
        return c

    def drain(s):
        def wait(t, c):
            for kk in range(TOPK):
                _row_copy(ring.at[s, pl.ds(0, RT)], xs_ref.at[pl.ds(0, RT)], sem.at[s]).wait()
            return c
        lax.fori_loop(0, td, wait, 0)

    lax.fori_loop(0, td, start, 0)

    @pl.when(step > 0)
    def _prev():
        drain(1 - slot)

    @pl.when(step == last)
    def _own():
        drain(slot)


def _dispatch(pend, cnt, dest, h2t, n_rows, td):
    n = h2t.shape[0] // RT
    gs = pltpu.PrefetchScalarGridSpec(
        num_scalar_prefetch=2, grid=(n // td,),
        in_specs=[pl.BlockSpec((1, 1, td * TOPK), lambda i, *_: (i, 0, 0), memory_space=pltpu.SMEM),
                  pl.BlockSpec((td * RT, LANES), lambda i, *_: (i, 0))],
        out_specs=pl.BlockSpec(memory_space=pl.ANY),
        scratch_shapes=[pltpu.VMEM((GROUP * RT, LANES), F32), pltpu.VMEM((2, td * RT, LANES), F32),
                        pltpu.SemaphoreType.DMA(()), pltpu.SemaphoreType.DMA((2,))])
    return pl.pallas_call(
        _dispatch_kernel, grid_spec=gs, out_shape=jax.ShapeDtypeStruct((n_rows * RT, LANES), F32),
        compiler_params=_cparams("arbitrary"), name="dispatch")(
            pend, cnt, dest.reshape(n // td, 1, td * TOPK), h2t)


def _experts_kernel(be_ref, nu_ref, x_ref, wg_ref, wu_ref, wd_ref, o_ref, wg_s, wu_s, wd_s, x_s):
    b = pl.program_id(0)

    @pl.when(b < nu_ref[0])
    def _run():
        @pl.when((b == 0) | (be_ref[b] != be_ref[jnp.maximum(b - 1, 0)]))
        def _load():
            wg_s[...] = wg_ref[0].astype(BF16)
            wu_s[...] = wu_ref[0].astype(BF16)
            wd_s[...] = wd_ref[0].astype(BF16)

        d = functools.partial(jnp.dot, preferred_element_type=F32)
        for s in range(RT):
            x_s[:, s * LANES:(s + 1) * LANES] = x_ref[pl.ds(s, GROUP, stride=RT), :].astype(BF16)
        x = x_s[...]
        hb = _silu(d(x, wg_s[...])) * d(x, wu_s[...])
        _to_row_tiles(o_ref, d(hb.astype(BF16), wd_s[...]))


def _experts(blk_e, n_used, xs, w_gate, w_up, w_down):
    n_rows = xs.shape[0] // RT
    rows = lambda b, be, nu: (jnp.minimum(b, nu[0] - 1), 0)
    gs = pltpu.PrefetchScalarGridSpec(
        num_scalar_prefetch=2, grid=(n_rows // GROUP,),
        in_specs=[pl.BlockSpec((GROUP * RT, LANES), rows),
                  pl.BlockSpec((1, D, DE), lambda b, be, nu: (be[b], 0, 0)),
                  pl.BlockSpec((1, D, DE), lambda b, be, nu: (be[b], 0, 0)),
                  pl.BlockSpec((1, DE, D), lambda b, be, nu: (be[b], 0, 0))],
        out_specs=pl.BlockSpec((GROUP * RT, LANES), rows),
        scratch_shapes=[pltpu.VMEM((D, DE), BF16), pltpu.VMEM((D, DE), BF16), pltpu.VMEM((DE, D), BF16),
                        pltpu.VMEM((GROUP, D), BF16)])
    return pl.pallas_call(
        _experts_kernel, grid_spec=gs, out_shape=jax.ShapeDtypeStruct((n_rows * RT, LANES), F32),
        compiler_params=_cparams("arbitrary"), name="experts")(blk_e, n_used, xs, w_gate, w_up, w_down)


def _combine_kernel(dest_ref, dnext_ref, ys_ref, wt_ref, base_ref, gt2_ref, o_ref, buf, sem):
    step = pl.program_id(0)
    last = pl.num_programs(0) - 1
    tc = base_ref.shape[0]
    slot = step % 2

    def issue(dref, s):
        def start(t, c):
            dst = pl.multiple_of(t * RT, RT)
            for kk in range(TOPK):
                src = pl.multiple_of(dref[0, 0, t * TOPK + kk] * RT, RT)
                _row_copy(ys_ref.at[pl.ds(src, RT)], buf.at[s, kk, pl.ds(dst, RT)], sem.at[s]).start(priority=kk % 2)
            return c
        lax.fori_loop(0, tc, start, 0)

    @pl.when(step == 0)
    def _first():
        issue(dest_ref, slot)

    @pl.when(step < last)
    def _next():
        issue(dnext_ref, 1 - slot)

    def wait(t, c):
        for kk in range(TOPK):
            _row_copy(ys_ref.at[pl.ds(0, RT)], buf.at[slot, kk, pl.ds(0, RT)], sem.at[slot]).wait()
        return c

    lax.fori_loop(0, tc, wait, 0)
    wt = wt_ref[...]
    acc = wt[:, 0:1] * _from_row_tiles(buf, tc, (slot, 0))
    for kk in range(1, TOPK):
        acc = acc + wt[:, kk:kk + 1] * _from_row_tiles(buf, tc, (slot, kk))
    o_ref[...] = base_ref[...] + gt2_ref[0] * acc


def _combine(dest, ys, wt, base, mod3, per_token, tc, tiles_per_batch):
    n = base.shape[0]
    nt = n // tc
    return pl.pallas_call(
        _combine_kernel, grid=(nt,),
        in_specs=[pl.BlockSpec((1, 1, tc * TOPK), lambda i: (i, 0, 0), memory_space=pltpu.SMEM),
                  pl.BlockSpec((1, 1, tc * TOPK), lambda i: (jnp.minimum(i + 1, nt - 1), 0, 0),
                               memory_space=pltpu.SMEM),
                  pl.BlockSpec(memory_space=pl.ANY),
                  pl.BlockSpec((tc, TOPK), lambda i: (i, 0)),
                  pl.BlockSpec((tc, D), lambda i: (i, 0)),
                  _mod_spec(per_token, tc, tiles_per_batch, 5)],
        out_specs=pl.BlockSpec((tc, D), lambda i: (i, 0)),
        out_shape=jax.ShapeDtypeStruct((n, D), F32),
        scratch_shapes=[pltpu.VMEM((2, TOPK, tc * RT, LANES), F32), pltpu.SemaphoreType.DMA((2,))],
        compiler_params=_cparams("arbitrary"), name="combine")(
            dest.reshape(nt, 1, tc * TOPK), dest.reshape(nt, 1, tc * TOPK), ys, wt, base, mod3)


def _pick_tile(n, pref):
    t = pref
    while n % t:
        t //= 2
    return t


def kernel(x_prompt, x_sample, cache_k, cache_v, page_table, state_mlstm_C, state_mlstm_n, state_mlstm_m,
           c_prompt, c_sample, w_ada, b_ada, g_norm1, w_in, b_gates, g_q, g_k, g_mh, w_br_m, w_br_a, w_out,
           g_norm2, w_router, b_router, w_gate, w_up, w_down, ws_gate, ws_up, ws_down):
    depth = w_ada.shape[0]
    assert depth == 1
    bsz, t, _ = x_prompt.shape
    db, ts, _ = x_sample.shape
    n_p, n_s = bsz * t, db * ts
    assert t % BLK == 0 and t % MCHUNK == 0 and ts <= 8
    l = 0

    nc = bsz + db
    ncp = -(-nc // 8) * 8
    c_all = jnp.pad(jnp.concatenate([c_prompt, c_sample], axis=0), ((0, ncp - nc), (0, 0)))
    mod = _ada(c_all, w_ada[l], b_ada[l])
    mod_p = mod[:bsz].reshape(bsz, 1, 6 * D)
    tm_s = _pick_tile(n_s, 256)
    mod_s = jnp.repeat(mod[bsz:nc], ts, axis=0).reshape(n_s // tm_s, tm_s, 6 * D)

    wi = w_in[l]
    sl = lambda name: wi[:, _OFF[name][0]:_OFF[name][1]]
    wm = jnp.concatenate([sl("mq"), sl("mk"), sl("mv"), sl("aq"), sl("ak"), sl("av")], axis=1).astype(BF16)
    wg = jnp.pad(jnp.concatenate([sl("mi"), sl("mf")], axis=1), ((0, 0), (0, 128 - 2 * MH))).astype(BF16)
    wl = jnp.concatenate([sl("mo"), sl("ga"), sl("gb")], axis=1).astype(BF16)
    g1 = g_norm1[l].reshape(1, D)
    g2 = g_norm2[l].reshape(1, D)
    gq = jnp.tile(g_q[l], AH).reshape(1, AW)
    gk = jnp.tile(g_k[l], AH).reshape(1, AW)
    gmh = g_mh[l].reshape(1, MW)
    grp = jnp.arange(AW) // AD
    bd = jnp.where(grp[:, None] == grp[None, :], 1.0 / AD, 0.0).astype(BF16)
    merge_w = (g1, g2, gmh, wl, w_br_m[l].astype(BF16), w_br_a[l].astype(BF16), w_out[l].astype(BF16),
               w_router[l], b_router[l].reshape(1, NE), ws_gate[l].astype(BF16), ws_up[l].astype(BF16),
               ws_down[l].astype(BF16))
    slopes = 2.0 ** (-8.0 * jnp.arange(1, AH + 1, dtype=F32) / AH)

    tm_p = _pick_tile(t, 512)
    xp2 = x_prompt.reshape(n_p, D)
    xs2 = x_sample.reshape(n_s, D)
    mq_p, mk_p, mv_p, gt_p, qa_p, ka_p, va_p, kb_p, vt_p = _inproj(
        xp2, mod_p, False, tm_p, t // tm_p, g1, wm, wg, gq, gk, bd, True)
    mq_s, mk_s, mv_s, gt_s, qa_s, ka_s, va_s = _inproj(
        xs2, mod_s, True, tm_s, 1, g1, wm, wg, gq, gk, bd, False)

    zeros = functools.partial(jnp.zeros, dtype=F32)
    gp = gt_p.reshape(bsz, t, 2 * MH)
    hm_p, c_p, nn_p, m_p = _mlstm(
        mq_p.reshape(bsz, t, MW), mk_p.reshape(bsz, t, MW), mv_p.reshape(bsz, t, MW),
        gp.transpose(0, 2, 1), gp, b_gates[l], zeros((bsz, MH, MD, MD)), zeros((bsz, MH, MD)),
        zeros((bsz, MH, MD)), _pick_tile(bsz, 2), t, MCHUNK)
    ls = 16
    assert ts <= ls
    pad_t = lambda a: jnp.pad(a.reshape(db, ts, -1), ((0, 0), (0, ls - ts), (0, 0)))
    gs_ = pad_t(gt_s)
    m0 = jnp.broadcast_to(state_mlstm_m[l].astype(F32)[:, :, None], (db, MH, MD))
    hm_s, c_sm, nn_s, m_sm = _mlstm(
        pad_t(mq_s), pad_t(mk_s), pad_t(mv_s), gs_.transpose(0, 2, 1), gs_, b_gates[l],
        state_mlstm_C[l].astype(F32), state_mlstm_n[l].astype(F32), m0, _pick_tile(db, 8), ts, ls)
    hm_s = hm_s[:, :ts].reshape(n_s, MW)

    nb = t // BLK
    km = _kmean(ka_p)
    ha_p = _moba_prompt(qa_p.reshape(bsz, t, AW), kb_p.reshape(bsz, nb, BLK, AW), vt_p.reshape(bsz, nb, AW, BLK),
                        km.reshape(bsz, nb, AW), slopes).reshape(n_p, AW)
    qh = qa_s.astype(F32).reshape(db, ts, AH, AD).transpose(0, 2, 1, 3)
    qbd = (qh[:, :, :, None, :] * jnp.eye(AH, dtype=F32)[None, :, None, :, None]).reshape(db, AH * ts, AW)
    tn = 8
    pad_n = lambda a: jnp.pad(a.reshape(db, ts, AW), ((0, 0), (0, tn - ts), (0, 0)))
    to_t = lambda c: jnp.transpose(c[l], (0, 2, 3, 1))
    ha_s = _moba_sample_m(page_table, to_t(cache_k), to_t(cache_v), qbd, pad_n(ka_s), pad_n(va_s), ts)
    ha_s = ha_s.reshape(n_s, AW)

    tmm_p = _pick_tile(t, 256)
    base_p, h2_p, idx_p, wt_p, rk_p, cnt1 = _merge(
        xp2, hm_p.reshape(n_p, MW), ha_p, mod_p, False, tmm_p, t // tmm_p, merge_w, zeros((1, NE)))
    base_s, h2_s, idx_s, wt_s, rk_s, cnt2 = _merge(
        xs2, hm_s, ha_s, mod_s, True, tm_s, 1, merge_w, cnt1)

    n_all = n_p + n_s
    cnt = cnt2.reshape(NE).astype(I32)
    padded = (cnt + GROUP - 1) // GROUP * GROUP
    pend = jnp.cumsum(padded)
    pstart = pend - padded
    n_blocks = -(-(n_all * TOPK + NE * (GROUP - 1)) // GROUP)
    n_used = (pend[-1] // GROUP).reshape(1)
    blk = jnp.minimum(jnp.arange(n_blocks, dtype=I32), n_used[0] - 1) * GROUP
    blk_e = jnp.minimum(jnp.sum((pend[None, :] <= blk[:, None]).astype(I32), axis=1), NE - 1)

    h2 = jnp.concatenate([h2_p, h2_s], axis=0)
    dest_p = _dest(idx_p, rk_p, pstart)
    dest_s = _dest(idx_s, rk_s, pstart)
    td = _pick_tile(n_all, 128)
    xs = _dispatch(pend, cnt, jnp.concatenate([dest_p, dest_s], axis=0), h2, n_blocks * GROUP, td)
    ys = _experts(blk_e, n_used, xs, w_gate[l], w_up[l], w_down[l])
    tc_p = _pick_tile(t, 64)
    y_p = _combine(dest_p, ys, wt_p, base_p, mod_p, False, tc_p, t // tc_p)
    tc_s = _pick_tile(tm_s, 64)
    mod_sc = mod_s.reshape(n_s // tc_s, tc_s, 6 * D)
    y_s = _combine(dest_s, ys, wt_s, base_s, mod_sc, True, tc_s, 1)

    st = lambda a: a[None]
    return (y_p.reshape(bsz, t, D), y_s.reshape(db, ts, D),
            st(ka_p.reshape(bsz, t, AH, AD)), st(va_p.reshape(bsz, t, AH, AD)),
            st(c_p), st(nn_p), st(m_p[:, :, 0]),
            st(ka_s.reshape(db, ts, AH, AD)), st(va_s.reshape(db, ts, AH, AD)),
            st(c_sm), st(nn_s), st(m_sm[:, :, 0]))
```

```python
import functools

import jax
import jax.numpy as jnp
from jax import lax
from jax.experimental import pallas as pl
from jax.experimental.pallas import tpu as pltpu

F32, BF16, I32 = jnp.float32, jnp.bfloat16, jnp.int32

D = 1024
MH, MD = 4, 128
AH, AD = 8, 64
MW, AW = MH * MD, AH * AD
MCHUNK = 128
BLK = 256
TOPB = 3
NE, TOPK, DE, DSH = 256, 8, 256, 256
ROUTE_SCALE = 2.5
GROUP = 256
EPS = 1e-6
NEG = -1e30
MASKV = -3.0e38
VMEM_LIMIT = 56 * 1024 * 1024
_OFF = {}
_o = 0
for _n, _w in (("mq", MW), ("mk", MW), ("mv", MW), ("mo", MW), ("mi", MH), ("mf", MH),
               ("aq", AW), ("ak", AW), ("av", AW), ("ga", D), ("gb", D)):
    _OFF[_n] = (_o, _o + _w)
    _o += _w


def _cparams(*sem):
    return pltpu.CompilerParams(dimension_semantics=sem, vmem_limit_bytes=VMEM_LIMIT)


def _mm(a, b):
    return jnp.dot(a.astype(BF16), b.astype(BF16), preferred_element_type=F32)


def _mm_nt(a, b):
    return lax.dot_general(a.astype(BF16), b.astype(BF16), (((1,), (1,)), ((), ())),
                           preferred_element_type=F32)


def _mm_tn(a, b):
    return lax.dot_general(a.astype(BF16), b.astype(BF16), (((0,), (0,)), ((), ())),
                           preferred_element_type=F32)


def _split2(x):
    hi = x.astype(BF16)
    return hi, (x - hi.astype(F32)).astype(BF16)


def _mm_x3(a, b):
    ah, al = _split2(a)
    bh, bl = _split2(b)
    d = functools.partial(jnp.dot, preferred_element_type=F32)
    return d(ah, bh) + d(al, bh) + d(ah, bl)


def _mm_nt_x2(a_bf16, b):
    bh, bl = _split2(b)
    return _mm_nt(a_bf16, bh) + _mm_nt(a_bf16, bl)


def _rms(x, g):
    return x * lax.rsqrt(jnp.mean(x * x, axis=-1, keepdims=True) + EPS) * g


def _silu(x):
    return x * jax.nn.sigmoid(x)


def _logsig(x):
    return jnp.minimum(x, 0.0) - jnp.log1p(jnp.exp(-jnp.abs(x)))


LANES = 128
RT = D // LANES


def _to_row_tiles(ref, x, idx=()):
    rows = x.shape[0]
    for s in range(RT):
        ref[idx + (pl.ds(s, rows, stride=RT), slice(None))] = x[:, s * LANES:(s + 1) * LANES]


def _from_row_tiles(ref, rows, idx=()):
    return jnp.concatenate([ref[idx + (pl.ds(s, rows, stride=RT), slice(None))] for s in range(RT)], axis=1)


def _ada_kernel(c_ref, w_ref, b_ref, o_ref):
    o_ref[...] = _mm_x3(_silu(c_ref[...]), w_ref[...]) + b_ref[...]


def _ada(c_all, w_ada, b_ada):
    r = c_all.shape[0]
    return pl.pallas_call(
        _ada_kernel, grid=(6,),
        in_specs=[pl.BlockSpec((r, D), lambda j: (0, 0)),
                  pl.BlockSpec((D, D), lambda j: (0, j)),
                  pl.BlockSpec((1, D), lambda j: (0, j))],
        out_specs=pl.BlockSpec((r, D), lambda j: (0, j)),
        out_shape=jax.ShapeDtypeStruct((r, 6 * D), F32),
        compiler_params=_cparams("parallel"), name="ada")(c_all, w_ada, b_ada.reshape(1, 6 * D))


def _mod_spec(per_token, tm, tiles_per_batch, chunk):
    if per_token:
        return pl.BlockSpec((1, tm, D), lambda i, *_: (i, 0, chunk))
    return pl.BlockSpec((1, 1, D), lambda i, *_: (i // tiles_per_batch, 0, chunk))


def _group_ms(x, bd):
    hi, lo = _split2(x * x)
    d = functools.partial(jnp.dot, preferred_element_type=F32)
    return d(hi, bd) + d(lo, bd)


def _inproj_kernel(x_ref, sh_ref, sc_ref, g1_ref, wm_ref, wg_ref, gq_ref, gk_ref, bd_ref,
                   mq_ref, mk_ref, mv_ref, gt_ref, qa_ref, *kv_refs):
    h = _rms(x_ref[...], g1_ref[...]) * (1.0 + sc_ref[0]) + sh_ref[0]
    hb = h.astype(BF16)
    d = functools.partial(jnp.dot, preferred_element_type=F32)
    mq_ref[...] = d(hb, wm_ref[:, 0:MW]).astype(BF16)
    mk_ref[...] = (d(hb, wm_ref[:, MW:2 * MW]) * (MD ** -0.5)).astype(BF16)
    mv_ref[...] = d(hb, wm_ref[:, 2 * MW:3 * MW]).astype(BF16)
    o = 3 * MW
    aq = d(hb, wm_ref[:, o:o + AW])
    ak = d(hb, wm_ref[:, o + AW:o + 2 * AW])
    av = d(hb, wm_ref[:, o + 2 * AW:o + 3 * AW])
    bd = bd_ref[...]
    qa = aq * lax.rsqrt(_group_ms(aq, bd) + EPS) * gq_ref[...]
    ka = ak * lax.rsqrt(_group_ms(ak, bd) + EPS) * gk_ref[...]
    qa_ref[...] = qa.astype(BF16)
    gt_ref[...] = d(hb, wg_ref[...])[:, 0:2 * MH]
    if len(kv_refs) == 2:
        ka_ref, va_ref = kv_refs
        ka_ref[...] = ka
        va_ref[...] = av
    else:
        kat_ref, vat_ref, kb_ref, vt_ref, km_ref = kv_refs
        for r in range(vt_ref.shape[0]):
            blk = slice(r * BLK, (r + 1) * BLK)
            kat_ref[0, :, blk] = ka[blk, :].T
            vbt = av[blk, :].T
            vat_ref[0, :, blk] = vbt
            kb_ref[r] = ka[blk, :].astype(BF16)
            vt_ref[r] = vbt.astype(BF16)
            km_ref[r] = jnp.mean(ka[blk, :], axis=0, keepdims=True)


def _inproj(x2, mod3, per_token, tm, tiles_per_batch, g1, wm, wg, gq, gk, bd, attn_layouts):
    n = x2.shape[0]
    row = lambda w: pl.BlockSpec((tm, w), lambda i: (i, 0))
    full = lambda a: pl.BlockSpec(a.shape, lambda i: (0,) * a.ndim)
    sds = jax.ShapeDtypeStruct
    out_specs = [row(MW), row(MW), row(MW), row(2 * MH), row(AW)]
    out_shape = [sds((n, MW), BF16), sds((n, MW), BF16), sds((n, MW), BF16), sds((n, 2 * MH), F32),
                 sds((n, AW), BF16)]
    if attn_layouts:
        tpb = tiles_per_batch
        pos_minor = pl.BlockSpec((1, AW, tm), lambda i: (i // tpb, 0, i % tpb))
        out_specs += [pos_minor, pos_minor,
                      pl.BlockSpec((tm // BLK, BLK, AW), lambda i: (i, 0, 0)),
                      pl.BlockSpec((tm // BLK, AW, BLK), lambda i: (i, 0, 0)),
                      pl.BlockSpec((tm // BLK, 1, AW), lambda i: (i, 0, 0))]
        out_shape += [sds((n // (tpb * tm), AW, tpb * tm), F32), sds((n // (tpb * tm), AW, tpb * tm), F32),
                      sds((n // BLK, BLK, AW), BF16), sds((n // BLK, AW, BLK), BF16),
                      sds((n // BLK, 1, AW), F32)]
    else:
        out_specs += [row(AW), row(AW)]
        out_shape += [sds((n, AW), F32), sds((n, AW), F32)]
    return pl.pallas_call(
        _inproj_kernel, grid=(n // tm,),
        in_specs=[row(D), _mod_spec(per_token, tm, tiles_per_batch, 0),
                  _mod_spec(per_token, tm, tiles_per_batch, 1),
                  full(g1), full(wm), full(wg), full(gq), full(gk), full(bd)],
        out_specs=out_specs, out_shape=out_shape,
        compiler_params=_cparams("parallel"), name="inproj")(x2, mod3, mod3, g1, wm, wg, gq, gk, bd)


def _mlstm_kernel(q_ref, k_ref, v_ref, gr_ref, gc_ref, bgc_ref, bgr_ref, c0_ref, n0_ref, m0_ref,
                  h_ref, cn_ref, nn_ref, mn_ref, c_s, n_s, m_s, *, bb, t_valid, L):
    c = pl.program_id(1)

    @pl.when(c == 0)
    def _init():
        c_s[...] = c0_ref[...]
        n_s[...] = n0_ref[...]
        m_s[...] = m0_ref[...]

    rowi = lax.broadcasted_iota(I32, (L, L), 0)
    coli = lax.broadcasted_iota(I32, (L, L), 1)
    tri = rowi >= coli
    ok_r = (c * L + lax.broadcasted_iota(I32, (1, L), 1)) < t_valid
    ok_c = (c * L + lax.broadcasted_iota(I32, (L, 1), 0)) < t_valid
    for b in range(bb):
        g_r = gr_ref[b] + bgc_ref[...]
        g_c = gc_ref[b] + bgr_ref[...]
        for h in range(MH):
            li_r = jnp.where(ok_r, g_r[h:h + 1, :], NEG)
            lf_r = jnp.where(ok_r, _logsig(g_r[MH + h:MH + h + 1, :]), 0.0)
            li_c = jnp.where(ok_c, g_c[:, h:h + 1], NEG)
            lf_c = jnp.where(ok_c, _logsig(g_c[:, MH + h:MH + h + 1]), 0.0)
            b_c = jnp.sum(jnp.where(tri, lf_r, 0.0), axis=1, keepdims=True)
            b_r = jnp.sum(jnp.where(rowi <= coli, lf_c, 0.0), axis=0, keepdims=True)
            q = q_ref[b, :, h * MD:(h + 1) * MD]
            k = k_ref[b, :, h * MD:(h + 1) * MD]
            v = v_ref[b, :, h * MD:(h + 1) * MD]
            cm = c_s[b, h]
            nv = n_s[b, h:h + 1, :]
            m_prev = m_s[b, h:h + 1, 0:1]
            inter = m_prev + b_c
            intra = jnp.where(tri, li_r + b_c - b_r, NEG)
            m_t = jnp.maximum(inter, jnp.max(intra, axis=1, keepdims=True))
            w_inter = jnp.exp(inter - m_t)
            s = _mm_nt(q, k) * jnp.exp(intra - m_t)
            num = w_inter * _mm_nt(q, cm) + _mm(s, v)
            qn = jnp.sum(q.astype(F32) * nv, axis=1, keepdims=True)
            den = w_inter * qn + jnp.sum(s, axis=1, keepdims=True)
            h_ref[b, :, h * MD:(h + 1) * MD] = num / jnp.maximum(jnp.abs(den), jnp.exp(-m_t))
            m_last = m_t[L - 1:L, :]
            b_last = b_c[L - 1:L, :]
            w_c = jnp.exp(m_prev + b_last - m_last)
            w_s = jnp.exp(li_c + b_last - b_c - m_last)
            c_s[b, h] = w_c * cm + _mm_tn(v.astype(F32) * w_s, k)
            n_s[b, h:h + 1, :] = w_c * nv + jnp.sum(k.astype(F32) * w_s, axis=0, keepdims=True)
            m_s[b, h:h + 1, :] = jnp.broadcast_to(m_last, (1, MD))

    @pl.when(c == pl.num_programs(1) - 1)
    def _fin():
        cn_ref[...] = c_s[...]
        nn_ref[...] = n_s[...]
        mn_ref[...] = m_s[...]


def _mlstm(q, k, v, g_row, g_col, b_gates, c0, n0, m0, bb, t_valid, L):
    bsz, t, _ = q.shape
    seq = pl.BlockSpec((bb, L, MW), lambda g, c: (g, c, 0))
    st4 = pl.BlockSpec((bb, MH, MD, MD), lambda g, c: (g, 0, 0, 0))
    st3 = pl.BlockSpec((bb, MH, MD), lambda g, c: (g, 0, 0))
    sds = jax.ShapeDtypeStruct
    return pl.pallas_call(
        functools.partial(_mlstm_kernel, bb=bb, t_valid=t_valid, L=L),
        grid=(bsz // bb, t // L),
        in_specs=[seq, seq, seq,
                  pl.BlockSpec((bb, 2 * MH, L), lambda g, c: (g, 0, c)),
                  pl.BlockSpec((bb, L, 2 * MH), lambda g, c: (g, c, 0)),
                  pl.BlockSpec((2 * MH, 1), lambda g, c: (0, 0)),
                  pl.BlockSpec((1, 2 * MH), lambda g, c: (0, 0)),
                  st4, st3, st3],
        out_specs=[seq, st4, st3, st3],
        out_shape=[sds((bsz, t, MW), F32), sds((bsz, MH, MD, MD), F32),
                   sds((bsz, MH, MD), F32), sds((bsz, MH, MD), F32)],
        scratch_shapes=[pltpu.VMEM((bb, MH, MD, MD), F32), pltpu.VMEM((bb, MH, MD), F32),
                        pltpu.VMEM((bb, MH, MD), F32)],
        compiler_params=_cparams("parallel", "arbitrary"), name="mlstm")(
            q, k, v, g_row, g_col, b_gates.reshape(2 * MH, 1), b_gates.reshape(1, 2 * MH), c0, n0, m0)


def _kmean_kernel(k_ref, o_ref):
    for r in range(o_ref.shape[0]):
        o_ref[r:r + 1, :] = jnp.mean(k_ref[r * BLK:(r + 1) * BLK, :], axis=0, keepdims=True)


def _kmean(ka):
    n = ka.shape[0]
    nblk = n // BLK
    r = 8 if nblk % 8 == 0 else nblk
    return pl.pallas_call(
        _kmean_kernel, grid=(nblk // r,),
        in_specs=[pl.BlockSpec((r * BLK, AW), lambda i: (i, 0))],
        out_specs=pl.BlockSpec((r, AW), lambda i: (i, 0)),
        out_shape=jax.ShapeDtypeStruct((nblk, AW), F32),
        compiler_params=_cparams("parallel"), name="kmean")(ka)


def _select_blocks(gate, own, n_sel):
    nb = gate.shape[1]
    col = lax.broadcasted_iota(I32, gate.shape, 1).astype(F32)
    ownf = jnp.asarray(own, F32) if not isinstance(own, int) else float(own)
    g = jnp.where(col < ownf, gate, NEG)
    sel = jnp.zeros(gate.shape, F32)
    for _ in range(n_sel):
        mx = jnp.max(g, axis=1, keepdims=True)
        idx = jnp.min(jnp.where(g == mx, col, float(nb)), axis=1, keepdims=True)
        hit = col == idx
        sel = jnp.where(hit & (idx < ownf), 1.0, sel)
        g = jnp.where(hit, MASKV, g)
    return sel


def _select_blocks_t(gate_t, own, n_sel):
    nb = gate_t.shape[0]
    row = lax.broadcasted_iota(I32, gate_t.shape, 0).astype(F32)
    g = jnp.where(row < own, gate_t, NEG)
    sel = jnp.zeros(gate_t.shape, F32)
    for _ in range(n_sel):
        mx = jnp.max(g, axis=0, keepdims=True)
        idx = jnp.min(jnp.where(g == mx, row, float(nb)), axis=0, keepdims=True)
        hit = row == idx
        sel = jnp.where(hit & (idx < own), 1.0, sel)
        g = jnp.where(hit, MASKV, g)
    return sel


def _moba_p_kernel(slope_ref, q_ref, k_ref, vt_ref, km_ref, o_ref, a_s, s_s, bias_s, qs_s, m_s, l_s, acc_s):
    b = pl.program_id(0)
    i = pl.program_id(1)
    nb = km_ref.shape[1]
    rel_t = lax.broadcasted_iota(I32, (BLK, BLK), 1) - lax.broadcasted_iota(I32, (BLK, BLK), 0)

    @pl.when((b == 0) & (i == 0))
    def _alibi():
        relf = rel_t.astype(F32)
        for h in range(AH):
            a_s[h] = slope_ref[h] * relf

    causal = rel_t >= 0
    own = i.astype(F32)
    for h in range(AH):
        hs = slice(h * AD, (h + 1) * AD)
        q = q_ref[0, :, hs]
        kmh, kml = _split2(km_ref[0, :, hs])
        gate_t = _mm_nt(kmh, q) + _mm_nt(kml, q)
        sel = _select_blocks_t(gate_t, own, min(TOPB, nb))
        bias_s[h] = jnp.where(sel > 0.5, 0.0, MASKV)
        qs = (q.astype(F32) * (AD ** -0.5)).astype(BF16)
        qs_s[h] = qs
        s_s[h] = jnp.where(causal, _mm_nt(k_ref[0, i, :, hs], qs) - a_s[h], MASKV)
    for h in range(AH):
        hs = slice(h * AD, (h + 1) * AD)
        m = jnp.max(s_s[h], axis=0, keepdims=True)
        p = jnp.exp(s_s[h] - m)
        m_s[h] = m
        l_s[h] = jnp.sum(p, axis=0, keepdims=True)
        acc_s[h] = _mm(vt_ref[0, i, hs, :], p)

    def body(j, c):
        dj = ((i - j) * BLK).astype(F32)
        for h in range(AH):
            hs = slice(h * AD, (h + 1) * AD)
            rowterm = bias_s[h, pl.ds(j, 1), :] - slope_ref[h] * dj
            s_s[h] = (_mm_nt(k_ref[0, j, :, hs], qs_s[h]) - a_s[h]) + rowterm
        for h in range(AH):
            hs = slice(h * AD, (h + 1) * AD)
            m = m_s[h]
            m_new = jnp.maximum(m, jnp.max(s_s[h], axis=0, keepdims=True))
            p = jnp.exp(s_s[h] - m_new)
            alpha = jnp.exp(m - m_new)
            l_s[h] = alpha * l_s[h] + jnp.sum(p, axis=0, keepdims=True)
            acc_s[h] = alpha * acc_s[h] + _mm(vt_ref[0, j, hs, :], p)
            m_s[h] = m_new
        return c

    lax.fori_loop(0, i, body, 0)
    out_t = jnp.concatenate([acc_s[h] / l_s[h] for h in range(AH)], axis=0)
    o_ref[0] = out_t.T


def _moba_prompt(q, kb, vt, km, slopes):
    bsz, t, _ = q.shape
    nb = t // BLK
    gs = pltpu.PrefetchScalarGridSpec(
        num_scalar_prefetch=1, grid=(bsz, nb),
        in_specs=[pl.BlockSpec((1, BLK, AW), lambda b, i, s: (b, i, 0)),
                  pl.BlockSpec((1, nb, BLK, AW), lambda b, i, s: (b, 0, 0, 0)),
                  pl.BlockSpec((1, nb, AW, BLK), lambda b, i, s: (b, 0, 0, 0)),
                  pl.BlockSpec((1, nb, AW), lambda b, i, s: (b, 0, 0))],
        out_specs=pl.BlockSpec((1, BLK, AW), lambda b, i, s: (b, i, 0)),
        scratch_shapes=[pltpu.VMEM((AH, BLK, BLK), F32), pltpu.VMEM((AH, BLK, BLK), F32),
                        pltpu.VMEM((AH, nb, BLK), F32),
                        pltpu.VMEM((AH, BLK, AD), BF16), pltpu.VMEM((AH, 1, BLK), F32),
                        pltpu.VMEM((AH, 1, BLK), F32), pltpu.VMEM((AH, AD, BLK), F32)])
    return pl.pallas_call(
        _moba_p_kernel, grid_spec=gs,
        out_shape=jax.ShapeDtypeStruct((bsz, t, AW), F32),
        compiler_params=_cparams("arbitrary", "arbitrary"), name="moba_p")(slopes, q, kb, vt, km)


def _moba_s_kernel(pt_ref, *refs, ppb, page, nbp, past, rq):
    k_pages = refs[0:ppb]
    v_pages = refs[ppb:2 * ppb]
    q_ref, kn_ref, vn_ref, o_ref, k_s, v_s, km_s = refs[2 * ppb:]
    b = pl.program_id(0)
    j = pl.program_id(1)
    lk = (nbp + 1) * BLK

    @pl.when((b == 0) & (j == 0))
    def _zero():
        k_s[past:lk, :] = jnp.zeros((BLK, AW), BF16)
        v_s[past:lk, :] = jnp.zeros((BLK, AW), BF16)
        km_s[...] = jnp.zeros(km_s.shape, F32)

    ksum = jnp.zeros((1, AW), F32)
    for p in range(ppb):
        kp = k_pages[p][0]
        off = pl.multiple_of(j * BLK + p * page, page)
        k_s[pl.ds(off, page), :] = kp
        v_s[pl.ds(off, page), :] = v_pages[p][0]
        ksum = ksum + jnp.sum(kp.astype(F32), axis=0, keepdims=True)
    km_s[pl.ds(j, 1), :] = ksum * (1.0 / BLK)

    @pl.when(j == nbp - 1)
    def _attend():
        k_s[past:past + rq, :] = kn_ref[0].astype(BF16)
        v_s[past:past + rq, :] = vn_ref[0].astype(BF16)
        rows = AH * rq
        q8 = q_ref[0]
        qt = jnp.concatenate([q8] * AH, axis=0)
        rowh = lax.broadcasted_iota(I32, (rows, AW), 0) // rq
        laneh = lax.broadcasted_iota(I32, (rows, AW), 1) // AD
        qe = jnp.where(rowh == laneh, qt, 0.0)
        qe_b = qe.astype(BF16)
        gate = _mm_nt_x2(qe_b, km_s[...])
        sel = _select_blocks(gate, nbp, min(TOPB, nbp + 1))
        qs = (qe * (AD ** -0.5)).astype(BF16)
        hrow = lax.broadcasted_iota(I32, (rows, 1), 0) // rq
        slope = jnp.zeros((rows, 1), F32)
        for hh in range(AH):
            slope = jnp.where(hrow == hh, 2.0 ** (-8.0 * (hh + 1) / AH), slope)
        rq_pos = past + lax.broadcasted_iota(I32, (rows, BLK), 0) % rq
        lane = lax.broadcasted_iota(I32, (rows, BLK), 1)
        segs = []
        m = jnp.full((rows, 1), NEG, F32)
        for cb in range(nbp + 1):
            s = _mm_nt(qs, k_s[cb * BLK:(cb + 1) * BLK, :])
            dist = rq_pos - (lane + cb * BLK)
            s = s - slope * dist.astype(F32)
            ok = dist >= 0
            if cb < nbp:
                ok = ok & (sel[:, cb:cb + 1] > 0.5)
            s = jnp.where(ok, s, MASKV)
            segs.append(s)
            m = jnp.maximum(m, jnp.max(s, axis=1, keepdims=True))
        l = jnp.zeros((rows, 1), F32)
        acc = jnp.zeros((rows, AW), F32)
        for cb in range(nbp + 1):
            p = jnp.exp(segs[cb] - m)
            l = l + jnp.sum(p, axis=1, keepdims=True)
            acc = acc + _mm(p, v_s[cb * BLK:(cb + 1) * BLK, :])
        acc = acc / l
        out = jnp.zeros((rq, AW), F32)
        lh = lax.broadcasted_iota(I32, (rq, AW), 1) // AD
        for hh in range(AH):
            out = out + jnp.where(lh == hh, acc[hh * rq:(hh + 1) * rq, :], 0.0)
        o_ref[0] = out


def _moba_sample(page_table, cache_k, cache_v, q, k_new, v_new):
    db, n_pages = page_table.shape
    page = cache_k.shape[1]
    rq = q.shape[1]
    past = n_pages * page
    assert past % BLK == 0 and BLK % page == 0
    ppb = BLK // page
    nbp = past // BLK

    def page_spec(p):
        return pl.BlockSpec((1, page, AW), lambda b, j, pt: (pt[b * n_pages + j * ppb + p], 0, 0))

    tok = pl.BlockSpec((1, rq, AW), lambda b, j, pt: (b, 0, 0))
    gs = pltpu.PrefetchScalarGridSpec(
        num_scalar_prefetch=1, grid=(db, nbp),
        in_specs=[page_spec(p) for p in range(ppb)] * 2 + [tok, tok, tok],
        out_specs=tok,
        scratch_shapes=[pltpu.VMEM(((nbp + 1) * BLK, AW), BF16), pltpu.VMEM(((nbp + 1) * BLK, AW), BF16),
                        pltpu.VMEM((max(8, -(-(nbp + 1) // 8) * 8), AW), F32)])
    return pl.pallas_call(
        functools.partial(_moba_s_kernel, ppb=ppb, page=page, nbp=nbp, past=past, rq=rq),
        grid_spec=gs, out_shape=jax.ShapeDtypeStruct((db, rq, AW), F32),
        compiler_params=_cparams("arbitrary", "arbitrary"), name="moba_s")(
            page_table.reshape(-1), *([cache_k] * ppb), *([cache_v] * ppb), q, k_new, v_new)


def _moba_sn_kernel(pt_ref, *refs, ppb, page, nbp, past, ts):
    k_pages = refs[0:ppb]
    v_pages = refs[ppb:2 * ppb]
    q_ref, kn_ref, vn_ref, o_ref, s_s, v_s, km_s = refs[2 * ppb:]
    j = pl.program_id(1)
    rows = AH * ts
    pl_rows = page * AH
    scale = AD ** -0.5
    qx = q_ref[0]
    qs = (qx * scale).astype(BF16)

    ksum = jnp.zeros((AH, AD), F32)
    for p in range(ppb):
        kp = k_pages[p][0]
        pg = j * ppb + p
        s_s[pg] = _mm_nt(qs, kp.reshape(pl_rows, AD))
        v_s[pg] = v_pages[p][0].reshape(pl_rows, AD).astype(BF16)
        ksum = ksum + jnp.sum(kp, axis=0)
    km_s[j] = ksum * (1.0 / BLK)

    @pl.when(j == nbp - 1)
    def _attend():
        rh = lax.broadcasted_iota(I32, (rows, 1), 0) // ts
        rr = lax.broadcasted_iota(I32, (rows, 1), 0) % ts
        slope = jnp.zeros((rows, 1), F32)
        for hh in range(AH):
            slope = jnp.where(rh == hh, 2.0 ** (-8.0 * (hh + 1) / AH), slope)
        ncol = nbp * AH
        gate = _mm_nt_x2(qx.astype(BF16), km_s[...].reshape(ncol, AD))
        gcol = lax.broadcasted_iota(I32, (rows, ncol), 1)
        colf = gcol.astype(F32)
        g = jnp.where(gcol % AH == rh, gate, MASKV)
        picks = []
        for _ in range(min(TOPB, nbp)):
            mx = jnp.max(g, axis=1, keepdims=True)
            idx = jnp.min(jnp.where(g == mx, colf, float(ncol)), axis=1, keepdims=True)
            picks.append(jnp.floor(idx * (1.0 / AH)))
            g = jnp.where(colf == idx, MASKV, g)
        lane = lax.broadcasted_iota(I32, (rows, pl_rows), 1)
        head_ok = lane % AH == rh
        kpos = (lane // AH).astype(F32)
        qpos = (past + rr).astype(F32)
        lane_n = lax.broadcasted_iota(I32, (rows, ts * AH), 1)
        dist_n = rr - lane_n // AH
        s_new = _mm_nt(qs, kn_ref[0]) - slope * dist_n.astype(F32)
        s_new = jnp.where((lane_n % AH == rh) & (dist_n >= 0), s_new, MASKV)
        m = jnp.max(s_new, axis=1, keepdims=True)
        n_pg = nbp * ppb
        for pg in range(n_pg):
            blk = float(pg // ppb)
            chosen = picks[0] == blk
            for c in picks[1:]:
                chosen = chosen | (c == blk)
            s = s_s[pg] - slope * (qpos - (kpos + float(pg * page)))
            s = jnp.where(head_ok & chosen, s, MASKV)
            s_s[pg] = s
            m = jnp.maximum(m, jnp.max(s, axis=1, keepdims=True))
        p_new = jnp.exp(s_new - m)
        l = jnp.sum(p_new, axis=1, keepdims=True)
        acc = _mm(p_new, vn_ref[0])
        for pg in range(n_pg):
            p = jnp.exp(s_s[pg] - m)
            l = l + jnp.sum(p, axis=1, keepdims=True)
            acc = acc + _mm(p, v_s[pg])
        o_ref[0] = acc / l


def _moba_sample_native(page_table, cache_k, cache_v, qx, k_new, v_new, ts):
    db, n_pages = page_table.shape
    page = cache_k.shape[1]
    past = n_pages * page
    assert past % BLK == 0 and BLK % page == 0
    ppb = BLK // page
    nbp = past // BLK
    rows = AH * ts

    def page_spec(p):
        return pl.BlockSpec((1, page, AH, AD), lambda b, j, pt: (pt[b * n_pages + j * ppb + p], 0, 0, 0))

    tok = pl.BlockSpec((1, rows, AD), lambda b, j, pt: (b, 0, 0))
    gs = pltpu.PrefetchScalarGridSpec(
        num_scalar_prefetch=1, grid=(db, nbp),
        in_specs=[page_spec(p) for p in range(ppb)] * 2 + [tok, tok, tok],
        out_specs=tok,
        scratch_shapes=[pltpu.VMEM((n_pages, rows, page * AH), F32), pltpu.VMEM((n_pages, page * AH, AD), BF16),
                        pltpu.VMEM((nbp, AH, AD), F32)])
    return pl.pallas_call(
        functools.partial(_moba_sn_kernel, ppb=ppb, page=page, nbp=nbp, past=past, ts=ts),
        grid_spec=gs, out_shape=jax.ShapeDtypeStruct((db, rows, AD), F32),
        compiler_params=_cparams("arbitrary", "arbitrary"), name="moba_s")(
            page_table.reshape(-1), *([cache_k] * ppb), *([cache_v] * ppb), qx, k_new, v_new)


def _moba_st_kernel(pt_ref, *refs, ppb, page, nbp, past, ts, tn):
    k_pages = refs[0:ppb]
    v_pages = refs[ppb:2 * ppb]
    qbd_ref, kn_ref, vn_ref, o_ref, s_s, v_s, g_s = refs[2 * ppb:]
    j = pl.program_id(1)
    rows = AH * ts
    qsf = qbd_ref[0] * (AD ** -0.5)
    qs = qsf.astype(BF16)

    gsum = jnp.zeros((rows, page), F32)
    for p in range(ppb):
        pg = j * ppb + p
        s = _mm(qs, k_pages[p][0].reshape(AW, page))
        s_s[pg] = s
        gsum = gsum + s
        v_s[pg] = v_pages[p][0].reshape(AW, page).astype(BF16)
    g_s[j] = gsum

    @pl.when(j == nbp - 1)
    def _attend():
        rh = lax.broadcasted_iota(I32, (rows, 1), 0) // ts
        rr = lax.broadcasted_iota(I32, (rows, 1), 0) % ts
        slope = jnp.zeros((rows, 1), F32)
        for hh in range(AH):
            slope = jnp.where(rh == hh, 2.0 ** (-8.0 * (hh + 1) / AH), slope)
        g = [jnp.sum(g_s[jb], axis=1, keepdims=True) for jb in range(nbp)]
        picks = []
        for _ in range(min(TOPB, nbp)):
            mx = g[0]
            for jb in range(1, nbp):
                mx = jnp.maximum(mx, g[jb])
            idx = jnp.full((rows, 1), float(nbp), F32)
            for jb in reversed(range(nbp)):
                idx = jnp.where(g[jb] == mx, float(jb), idx)
            picks.append(idx)
            g = [jnp.where(idx == float(jb), MASKV, g[jb]) for jb in range(nbp)]
        lane = lax.broadcasted_iota(I32, (rows, page), 1).astype(F32)
        qpos = (past + rr).astype(F32)
        lane_n = lax.broadcasted_iota(I32, (rows, tn), 1)
        dist_n = rr - lane_n
        s_new = lax.dot_general(qsf, kn_ref[0], (((1,), (1,)), ((), ())), preferred_element_type=F32)
        s_new = jnp.where(dist_n >= 0, s_new - slope * dist_n.astype(F32), MASKV)
        m = jnp.max(s_new, axis=1, keepdims=True)
        n_pg = nbp * ppb
        for pg in range(n_pg):
            blk = float(pg // ppb)
            chosen = picks[0] == blk
            for c in picks[1:]:
                chosen = chosen | (c == blk)
            s = s_s[pg] - slope * (qpos - (lane + float(pg * page)))
            s = jnp.where(chosen, s, MASKV)
            s_s[pg] = s
            m = jnp.maximum(m, jnp.max(s, axis=1, keepdims=True))
        p_new = jnp.exp(s_new - m)
        l = jnp.sum(p_new, axis=1, keepdims=True)
        acc = jnp.dot(p_new, vn_ref[0], preferred_element_type=F32)
        for pg in range(n_pg):
            p = jnp.exp(s_s[pg] - m)
            l = l + jnp.sum(p, axis=1, keepdims=True)
            acc = acc + _mm_nt(p, v_s[pg])
        acc = acc / l
        lh = lax.broadcasted_iota(I32, (ts, AW), 1) // AD
        out = jnp.zeros((ts, AW), F32)
        for hh in range(AH):
            out = out + jnp.where(lh == hh, acc[hh * ts:(hh + 1) * ts, :], 0.0)
        o_ref[0] = out


def _moba_sample_t(page_table, cache_kt, cache_vt, qbd, k_new, v_new, ts):
    db, n_pages = page_table.shape
    page = cache_kt.shape[3]
    past = n_pages * page
    assert past % BLK == 0 and BLK % page == 0
    ppb = BLK // page
    nbp = past // BLK
    rows = AH * ts
    tn = k_new.shape[1]

    def page_spec(p):
        return pl.BlockSpec((1, AH, AD, page), lambda b, j, pt: (pt[b * n_pages + j * ppb + p], 0, 0, 0))

    gs = pltpu.PrefetchScalarGridSpec(
        num_scalar_prefetch=1, grid=(db, nbp),
        in_specs=[page_spec(p) for p in range(ppb)] * 2 + [
            pl.BlockSpec((1, rows, AW), lambda b, j, pt: (b, 0, 0)),
            pl.BlockSpec((1, tn, AW), lambda b, j, pt: (b, 0, 0)),
            pl.BlockSpec((1, tn, AW), lambda b, j, pt: (b, 0, 0))],
        out_specs=pl.BlockSpec((1, ts, AW), lambda b, j, pt: (b, 0, 0)),
        scratch_shapes=[pltpu.VMEM((n_pages, rows, page), F32), pltpu.VMEM((n_pages, AW, page), BF16),
                        pltpu.VMEM((nbp, rows, page), F32)])
    return pl.pallas_call(
        functools.partial(_moba_st_kernel, ppb=ppb, page=page, nbp=nbp, past=past, ts=ts, tn=tn),
        grid_spec=gs, out_shape=jax.ShapeDtypeStruct((db, ts, AW), F32),
        compiler_params=_cparams("arbitrary", "arbitrary"), name="moba_s")(
            page_table.reshape(-1), *([cache_kt] * ppb), *([cache_vt] * ppb), qbd, k_new, v_new)


def _moba_sm_kernel(pt_ref, qbd_ref, kn_ref, vn_ref, ck_ref, cv_ref, o_ref, kbuf, vbuf, s_s, sem,
                    *, n_pages, page, nbp, past, ts, tn):
    b = pl.program_id(0)
    slot = b % 2
    ppb = n_pages // nbp
    rows = AH * ts

    def page_copies(seq, sl, pg):
        pid = pt_ref[seq * n_pages + pg]
        return (pltpu.make_async_copy(ck_ref.at[pid], kbuf.at[sl, pg], sem.at[sl, 0]),
                pltpu.make_async_copy(cv_ref.at[pid], vbuf.at[sl, pg], sem.at[sl, 1]))

    def fetch(seq, sl):
        for pg in range(n_pages):
            for cp in page_copies(seq, sl, pg):
                cp.start()

    @pl.when(b == 0)
    def _first():
        fetch(b, slot)

    @pl.when(b + 1 < pl.num_programs(0))
    def _next():
        fetch(b + 1, 1 - slot)

    for pg in range(n_pages):
        for cp in page_copies(b, slot, pg):
            cp.wait()

    qsf = qbd_ref[0] * (AD ** -0.5)
    qs = qsf.astype(BF16)
    g = []
    for jb in range(nbp):
        gsum = jnp.zeros((rows, page), F32)
        for p in range(ppb):
            pg = jb * ppb + p
            s = _mm(qs, kbuf[slot, pg].reshape(AW, page))
            s_s[pg] = s
            gsum = gsum + s
        g.append(jnp.sum(gsum, axis=1, keepdims=True))
    rh = lax.broadcasted_iota(I32, (rows, 1), 0) // ts
    rr = lax.broadcasted_iota(I32, (rows, 1), 0) % ts
    slope = jnp.zeros((rows, 1), F32)
    for hh in range(AH):
        slope = jnp.where(rh == hh, 2.0 ** (-8.0 * (hh + 1) / AH), slope)
    picks = []
    for _ in range(min(TOPB, nbp)):
        mx = g[0]
        for jb in range(1, nbp):
            mx = jnp.maximum(mx, g[jb])
        idx = jnp.full((rows, 1), float(nbp), F32)
        for jb in reversed(range(nbp)):
            idx = jnp.where(g[jb] == mx, float(jb), idx)
        picks.append(idx)
        g = [jnp.where(idx == float(jb), MASKV, g[jb]) for jb in range(nbp)]
    lane = lax.broadcasted_iota(I32, (rows, page), 1).astype(F32)
    qpos = (past + rr).astype(F32)
    dist_n = rr - lax.broadcasted_iota(I32, (rows, tn), 1)
    s_new = lax.dot_general(qsf, kn_ref[0], (((1,), (1,)), ((), ())), preferred_element_type=F32)
    s_new = jnp.where(dist_n >= 0, s_new - slope * dist_n.astype(F32), MASKV)
    m = jnp.max(s_new, axis=1, keepdims=True)
    for pg in range(n_pages):
        blk = float(pg // ppb)
        chosen = picks[0] == blk
        for c in picks[1:]:
            chosen = chosen | (c == blk)
        s = jnp.where(chosen, s_s[pg] - slope * (qpos - (lane + float(pg * page))), MASKV)
        s_s[pg] = s
        m = jnp.maximum(m, jnp.max(s, axis=1, keepdims=True))
    p_new = jnp.exp(s_new - m)
    l = jnp.sum(p_new, axis=1, keepdims=True)
    acc = jnp.dot(p_new, vn_ref[0], preferred_element_type=F32)
    for pg in range(n_pages):
        p = jnp.exp(s_s[pg] - m)
        l = l + jnp.sum(p, axis=1, keepdims=True)
        acc = acc + _mm_nt(p, vbuf[slot, pg].reshape(AW, page))
    acc = acc / l
    lh = lax.broadcasted_iota(I32, (ts, AW), 1) // AD
    out = jnp.zeros((ts, AW), F32)
    for hh in range(AH):
        out = out + jnp.where(lh == hh, acc[hh * ts:(hh + 1) * ts, :], 0.0)
    o_ref[0] = out


def _moba_sample_m(page_table, cache_kt, cache_vt, qbd, k_new, v_new, ts):
    db, n_pages = page_table.shape
    page = cache_kt.shape[3]
    past = n_pages * page
    assert past % BLK == 0 and BLK % page == 0
    nbp = past // BLK
    rows = AH * ts
    tn = k_new.shape[1]
    gs = pltpu.PrefetchScalarGridSpec(
        num_scalar_prefetch=1, grid=(db,),
        in_specs=[pl.BlockSpec((1, rows, AW), lambda b, pt: (b, 0, 0)),
                  pl.BlockSpec((1, tn, AW), lambda b, pt: (b, 0, 0)),
                  pl.BlockSpec((1, tn, AW), lambda b, pt: (b, 0, 0)),
                  pl.BlockSpec(memory_space=pl.ANY), pl.BlockSpec(memory_space=pl.ANY)],
        out_specs=pl.BlockSpec((1, ts, AW), lambda b, pt: (b, 0, 0)),
        scratch_shapes=[pltpu.VMEM((2, n_pages, AH, AD, page), F32), pltpu.VMEM((2, n_pages, AH, AD, page), F32),
                        pltpu.VMEM((n_pages, rows, page), F32), pltpu.SemaphoreType.DMA((2, 2))])
    return pl.pallas_call(
        functools.partial(_moba_sm_kernel, n_pages=n_pages, page=page, nbp=nbp, past=past, ts=ts, tn=tn),
        grid_spec=gs, out_shape=jax.ShapeDtypeStruct((db, ts, AW), F32),
        compiler_params=_cparams("arbitrary"), name="moba_s")(
            page_table.reshape(-1), qbd, k_new, v_new, cache_kt, cache_vt)


def _merge_kernel(x_ref, hm_ref, ha_ref, sh1_ref, sc1_ref, gt1_ref, sh2_ref, sc2_ref, gt2_ref,
                  g1_ref, g2_ref, gmh_ref, wl_ref, wbm_ref, wba_ref, wo_ref, wr_ref, br_ref,
                  wsg_ref, wsu_ref, wsd_ref, cin_ref,
                  base_ref, h2_ref, idx_ref, wt_ref, rank_ref, cout_ref, cnt_s):
    i = pl.program_id(0)
    tm = x_ref.shape[0]

    @pl.when(i == 0)
    def _init():
        cnt_s[...] = cin_ref[...]

    x = x_ref[...]
    h1 = (_rms(x, g1_ref[...]) * (1.0 + sc1_ref[0]) + sh1_ref[0]).astype(BF16)
    d = functools.partial(jnp.dot, preferred_element_type=F32)
    mo = d(h1, wl_ref[:, 0:MW])
    ga = d(h1, wl_ref[:, MW:MW + D])
    gb = d(h1, wl_ref[:, MW + D:MW + 2 * D])
    hm = hm_ref[...]
    parts = []
    for h in range(MH):
        xh = hm[:, h * MD:(h + 1) * MD]
        parts.append(_rms(xh, gmh_ref[:, h * MD:(h + 1) * MD]))
    hmn = jnp.concatenate(parts, axis=1) * jax.nn.sigmoid(mo)
    merged = jax.nn.sigmoid(ga) * _mm(hmn, wbm_ref[...]) + jax.nn.sigmoid(gb) * _mm(ha_ref[...], wba_ref[...])
    x1 = x + gt1_ref[0] * _mm(merged, wo_ref[...])
    h2 = _rms(x1, g2_ref[...]) * (1.0 + sc2_ref[0]) + sh2_ref[0]
    _to_row_tiles(h2_ref, h2)
    h2b = h2.astype(BF16)
    shared = _mm(_silu(d(h2b, wsg_ref[...])) * d(h2b, wsu_ref[...]), wsd_ref[...])
    base_ref[...] = x1 + gt2_ref[0] * shared
    scores = jax.nn.sigmoid(_mm_x3(h2, wr_ref[...]))
    col = lax.broadcasted_iota(I32, (tm, NE), 1).astype(F32)
    g = scores + br_ref[...]
    idxs, wts = [], []
    onehot = jnp.zeros((tm, NE), F32)
    for _ in range(TOPK):
        mx = jnp.max(g, axis=1, keepdims=True)
        idx = jnp.min(jnp.where(g == mx, col, float(NE)), axis=1, keepdims=True)
        hit = col == idx
        idxs.append(idx)
        wts.append(jnp.sum(jnp.where(hit, scores, 0.0), axis=1, keepdims=True))
        onehot = jnp.where(hit, 1.0, onehot)
        g = jnp.where(hit, MASKV, g)
    wsum = wts[0]
    for w in wts[1:]:
        wsum = wsum + w
    ri = lax.broadcasted_iota(I32, (tm, tm), 0)
    ci = lax.broadcasted_iota(I32, (tm, tm), 1)
    before = jnp.where(ci < ri, 1.0, 0.0).astype(BF16)
    pref = cnt_s[...] + d(before, onehot.astype(BF16))
    lane8 = lax.broadcasted_iota(I32, (tm, TOPK), 1)
    idx_o = jnp.zeros((tm, TOPK), F32)
    wt_o = jnp.zeros((tm, TOPK), F32)
    rk_o = jnp.zeros((tm, TOPK), F32)
    for kk in range(TOPK):
        rk = jnp.sum(jnp.where(col == idxs[kk], pref, 0.0), axis=1, keepdims=True)
        idx_o = jnp.where(lane8 == kk, idxs[kk], idx_o)
        wt_o = jnp.where(lane8 == kk, wts[kk] / wsum * ROUTE_SCALE, wt_o)
        rk_o = jnp.where(lane8 == kk, rk, rk_o)
    idx_ref[...] = idx_o.astype(I32)
    wt_ref[...] = wt_o
    rank_ref[...] = rk_o.astype(I32)
    cnt_s[...] = cnt_s[...] + jnp.sum(onehot, axis=0, keepdims=True)
    cout_ref[...] = cnt_s[...]


def _merge(x2, hm, ha, mod3, per_token, tm, tiles_per_batch, weights, cnt_in):
    n = x2.shape[0]
    row = lambda w: pl.BlockSpec((tm, w), lambda i: (i, 0))
    full = lambda a: pl.BlockSpec(a.shape, lambda i: (0,) * a.ndim)
    mods = [_mod_spec(per_token, tm, tiles_per_batch, c) for c in range(6)]
    sds = jax.ShapeDtypeStruct
    return pl.pallas_call(
        _merge_kernel, grid=(n // tm,),
        in_specs=[row(D), row(MW), row(AW)] + mods + [full(w) for w in weights] + [full(cnt_in)],
        out_specs=[row(D), pl.BlockSpec((tm * RT, LANES), lambda i: (i, 0)), row(TOPK), row(TOPK), row(TOPK),
                   full(cnt_in)],
        out_shape=[sds((n, D), F32), sds((n * RT, LANES), F32), sds((n, TOPK), I32), sds((n, TOPK), F32),
                   sds((n, TOPK), I32), sds((1, NE), F32)],
        scratch_shapes=[pltpu.VMEM((1, NE), F32)],
        compiler_params=_cparams("arbitrary"), name="merge")(
            x2, hm, ha, *([mod3] * 6), *weights, cnt_in)


def _dest_kernel(idx_ref, rank_ref, pstart_ref, o_ref):
    tm = idx_ref.shape[0]
    col = lax.broadcasted_iota(I32, (tm, NE), 1)
    lane8 = lax.broadcasted_iota(I32, (tm, TOPK), 1)
    idx = idx_ref[...]
    out = rank_ref[...]
    for kk in range(TOPK):
        start = jnp.sum(jnp.where(col == idx[:, kk:kk + 1], pstart_ref[...], 0.0), axis=1, keepdims=True)
        out = out + jnp.where(lane8 == kk, start.astype(I32), 0)
    o_ref[...] = out


def _dest(idx, rank, pstart):
    n = idx.shape[0]
    tm = _pick_tile(n, 512)
    row = pl.BlockSpec((tm, TOPK), lambda i: (i, 0))
    return pl.pallas_call(
        _dest_kernel, grid=(n // tm,),
        in_specs=[row, row, pl.BlockSpec((1, NE), lambda i: (0, 0))],
        out_specs=row, out_shape=jax.ShapeDtypeStruct((n, TOPK), I32),
        compiler_params=_cparams("parallel"), name="dest")(idx, rank, pstart.astype(F32).reshape(1, NE))


def _row_copy(src, dst, sem):
    return pltpu.make_async_copy(src, dst, sem)


def _dispatch_kernel(pend_ref, cnt_ref, dest_ref, h_ref, xs_ref, zbuf, ring, zsem, sem):
    step = pl.program_id(0)
    last = pl.num_programs(0) - 1
    td = h_ref.shape[0] // RT
    slot = step % 2

    @pl.when(step == 0)
    def _zero_tails():
        zbuf[...] = jnp.zeros(zbuf.shape, F32)

        def tail(e):
            first = pl.multiple_of((pend_ref[e] - GROUP) * RT, RT)
            return _row_copy(zbuf, xs_ref.at[pl.ds(first, GROUP * RT)], zsem)

        def start(e, c):
            @pl.when(cnt_ref[e] > 0)
            def _():
                tail(e).start()
            return c

        def wait(e, c):
            @pl.when(cnt_ref[e] > 0)
            def _():
                tail(e).wait()
            return c

        lax.fori_loop(0, NE, start, 0)
        lax.fori_loop(0, NE, wait, 0)

    ring[slot] = h_ref[...]

    def start(t, c):
        src = ring.at[slot, pl.ds(pl.multiple_of(t * RT, RT), RT)]
        for kk in range(TOPK):
            dst = pl.multiple_of(dest_ref[0, 0, t * TOPK + kk] * RT, RT)
            _row_copy(src, xs_ref.at[pl.ds(dst, RT)], sem.at[slot]).start(priority=kk % 2)
        return c

    def drain(s):
        def wait(t, c):
            for kk in range(TOPK):
                _row_copy(ring.at[s, pl.ds(0, RT)], xs_ref.at[pl.ds(0, RT)], sem.at[s]).wait()
            return c
        lax.fori_loop(0, td, wait, 0)

    lax.fori_loop(0, td, start, 0)

    @pl.when(step > 0)
    def _prev():
        drain(1 - slot)

    @pl.when(step == last)
    def _own():
        drain(slot)


def _dispatch(pend, cnt, dest, h2t, n_rows, td):
    n = h2t.shape[0] // RT
    gs = pltpu.PrefetchScalarGridSpec(
        num_scalar_prefetch=2, grid=(n // td,),
        in_specs=[pl.BlockSpec((1, 1, td * TOPK), lambda i, *_: (i, 0, 0), memory_space=pltpu.SMEM),
                  pl.BlockSpec((td * RT, LANES), lambda i, *_: (i, 0))],
        out_specs=pl.BlockSpec(memory_space=pl.ANY),
        scratch_shapes=[pltpu.VMEM((GROUP * RT, LANES), F32), pltpu.VMEM((2, td * RT, LANES), F32),
                        pltpu.SemaphoreType.DMA(()), pltpu.SemaphoreType.DMA((2,))])
    return pl.pallas_call(
        _dispatch_kernel, grid_spec=gs, out_shape=jax.ShapeDtypeStruct((n_rows * RT, LANES), F32),
        compiler_params=_cparams("arbitrary"), name="dispatch")(
            pend, cnt, dest.reshape(n // td, 1, td * TOPK), h2t)


def _experts_kernel(be_ref, nx_ref, sl_ref, nu_ref, x_ref, wg_hbm, wu_hbm, wd_hbm, o_ref,
                    wg_f, wu_f, wd_f, wg_s, wu_s, wd_s, x_s, sem):
    b = pl.program_id(0)

    def fetch(e, slot):
        return (pltpu.make_async_copy(wg_hbm.at[e], wg_f.at[slot], sem.at[slot, 0]),
                pltpu.make_async_copy(wu_hbm.at[e], wu_f.at[slot], sem.at[slot, 1]),
                pltpu.make_async_copy(wd_hbm.at[e], wd_f.at[slot], sem.at[slot, 2]))

    @pl.when(b < nu_ref[0])
    def _run():
        slot = sl_ref[b]

        @pl.when((b == 0) | (be_ref[b] != be_ref[jnp.maximum(b - 1, 0)]))
        def _load():
            @pl.when(b == 0)
            def _first():
                for cp in fetch(be_ref[b], slot):
                    cp.start()

            for cp in fetch(be_ref[b], slot):
                cp.wait()

            @pl.when(nx_ref[b] < NE)
            def _next():
                for cp in fetch(nx_ref[b], 1 - slot):
                    cp.start()

            wg_s[...] = wg_f[slot].astype(BF16)
            wu_s[...] = wu_f[slot].astype(BF16)
            wd_s[...] = wd_f[slot].astype(BF16)

        d = functools.partial(jnp.dot, preferred_element_type=F32)
        for s in range(RT):
            x_s[:, s * LANES:(s + 1) * LANES] = x_ref[pl.ds(s, GROUP, stride=RT), :].astype(BF16)
        x = x_s[...]
        hb = _silu(d(x, wg_s[...])) * d(x, wu_s[...])
        _to_row_tiles(o_ref, d(hb.astype(BF16), wd_s[...]))


def _experts(blk_e, blk_next, blk_slot, n_used, xs, w_gate, w_up, w_down):
    n_rows = xs.shape[0] // RT
    rows = lambda b, be, nx, sl, nu: (jnp.minimum(b, nu[0] - 1), 0)
    hbm = pl.BlockSpec(memory_space=pl.ANY)
    gs = pltpu.PrefetchScalarGridSpec(
        num_scalar_prefetch=4, grid=(n_rows // GROUP,),
        in_specs=[pl.BlockSpec((GROUP * RT, LANES), rows), hbm, hbm, hbm],
        out_specs=pl.BlockSpec((GROUP * RT, LANES), rows),
        scratch_shapes=[pltpu.VMEM((2, D, DE), F32), pltpu.VMEM((2, D, DE), F32), pltpu.VMEM((2, DE, D), F32),
                        pltpu.VMEM((D, DE), BF16), pltpu.VMEM((D, DE), BF16), pltpu.VMEM((DE, D), BF16),
                        pltpu.VMEM((GROUP, D), BF16), pltpu.SemaphoreType.DMA((2, 3))])
    return pl.pallas_call(
        _experts_kernel, grid_spec=gs, out_shape=jax.ShapeDtypeStruct((n_rows * RT, LANES), F32),
        compiler_params=_cparams("arbitrary"), name="experts")(
            blk_e, blk_next, blk_slot, n_used, xs, w_gate, w_up, w_down)


def _combine_kernel(dest_ref, dnext_ref, ys_ref, wt_ref, base_ref, gt2_ref, o_ref, buf, sem):
    step = pl.program_id(0)
    last = pl.num_programs(0) - 1
    tc = base_ref.shape[0]
    slot = step % 2

    def issue(dref, s):
        def start(t, c):
            dst = pl.multiple_of(t * RT, RT)
            for kk in range(TOPK):
                src = pl.multiple_of(dref[0, 0, t * TOPK + kk] * RT, RT)
                _row_copy(ys_ref.at[pl.ds(src, RT)], buf.at[s, kk, pl.ds(dst, RT)], sem.at[s]).start(priority=kk % 2)
            return c
        lax.fori_loop(0, tc, start, 0)

    @pl.when(step == 0)
    def _first():
        issue(dest_ref, slot)

    @pl.when(step < last)
    def _next():
        issue(dnext_ref, 1 - slot)

    def wait(t, c):
        for kk in range(TOPK):
            _row_copy(ys_ref.at[pl.ds(0, RT)], buf.at[slot, kk, pl.ds(0, RT)], sem.at[slot]).wait()
        return c

    lax.fori_loop(0, tc, wait, 0)
    wt = wt_ref[...]
    acc = wt[:, 0:1] * _from_row_tiles(buf, tc, (slot, 0))
    for kk in range(1, TOPK):
        acc = acc + wt[:, kk:kk + 1] * _from_row_tiles(buf, tc, (slot, kk))
    o_ref[...] = base_ref[...] + gt2_ref[0] * acc


def _combine(dest, ys, wt, base, mod3, per_token, tc, tiles_per_batch):
    n = base.shape[0]
    nt = n // tc
    return pl.pallas_call(
        _combine_kernel, grid=(nt,),
        in_specs=[pl.BlockSpec((1, 1, tc * TOPK), lambda i: (i, 0, 0), memory_space=pltpu.SMEM),
                  pl.BlockSpec((1, 1, tc * TOPK), lambda i: (jnp.minimum(i + 1, nt - 1), 0, 0),
                               memory_space=pltpu.SMEM),
                  pl.BlockSpec(memory_space=pl.ANY),
                  pl.BlockSpec((tc, TOPK), lambda i: (i, 0)),
                  pl.BlockSpec((tc, D), lambda i: (i, 0)),
                  _mod_spec(per_token, tc, tiles_per_batch, 5)],
        out_specs=pl.BlockSpec((tc, D), lambda i: (i, 0)),
        out_shape=jax.ShapeDtypeStruct((n, D), F32),
        scratch_shapes=[pltpu.VMEM((2, TOPK, tc * RT, LANES), F32), pltpu.SemaphoreType.DMA((2,))],
        compiler_params=_cparams("arbitrary"), name="combine")(
            dest.reshape(nt, 1, tc * TOPK), dest.reshape(nt, 1, tc * TOPK), ys, wt, base, mod3)


def _pick_tile(n, pref):
    t = pref
    while n % t:
        t //= 2
    return t


def kernel(x_prompt, x_sample, cache_k, cache_v, page_table, state_mlstm_C, state_mlstm_n, state_mlstm_m,
           c_prompt, c_sample, w_ada, b_ada, g_norm1, w_in, b_gates, g_q, g_k, g_mh, w_br_m, w_br_a, w_out,
           g_norm2, w_router, b_router, w_gate, w_up, w_down, ws_gate, ws_up, ws_down):
    depth = w_ada.shape[0]
    assert depth == 1
    bsz, t, _ = x_prompt.shape
    db, ts, _ = x_sample.shape
    n_p, n_s = bsz * t, db * ts
    assert t % BLK == 0 and t % MCHUNK == 0 and ts <= 8
    l = 0

    nc = bsz + db
    ncp = -(-nc // 8) * 8
    c_all = jnp.pad(jnp.concatenate([c_prompt, c_sample], axis=0), ((0, ncp - nc), (0, 0)))
    mod = _ada(c_all, w_ada[l], b_ada[l])
    mod_p = mod[:bsz].reshape(bsz, 1, 6 * D)
    tm_s = _pick_tile(n_s, 256)
    mod_s = jnp.repeat(mod[bsz:nc], ts, axis=0).reshape(n_s // tm_s, tm_s, 6 * D)

    wi = w_in[l]
    sl = lambda name: wi[:, _OFF[name][0]:_OFF[name][1]]
    wm = jnp.concatenate([sl("mq"), sl("mk"), sl("mv"), sl("aq"), sl("ak"), sl("av")], axis=1).astype(BF16)
    wg = jnp.pad(jnp.concatenate([sl("mi"), sl("mf")], axis=1), ((0, 0), (0, 128 - 2 * MH))).astype(BF16)
    wl = jnp.concatenate([sl("mo"), sl("ga"), sl("gb")], axis=1).astype(BF16)
    g1 = g_norm1[l].reshape(1, D)
    g2 = g_norm2[l].reshape(1, D)
    gq = jnp.tile(g_q[l], AH).reshape(1, AW)
    gk = jnp.tile(g_k[l], AH).reshape(1, AW)
    gmh = g_mh[l].reshape(1, MW)
    grp = jnp.arange(AW) // AD
    bd = jnp.where(grp[:, None] == grp[None, :], 1.0 / AD, 0.0).astype(BF16)
    merge_w = (g1, g2, gmh, wl, w_br_m[l].astype(BF16), w_br_a[l].astype(BF16), w_out[l].astype(BF16),
               w_router[l], b_router[l].reshape(1, NE), ws_gate[l].astype(BF16), ws_up[l].astype(BF16),
               ws_down[l].astype(BF16))
    slopes = 2.0 ** (-8.0 * jnp.arange(1, AH + 1, dtype=F32) / AH)

    tm_p = _pick_tile(t, 512)
    xp2 = x_prompt.reshape(n_p, D)
    xs2 = x_sample.reshape(n_s, D)
    mq_p, mk_p, mv_p, gt_p, qa_p, kat_p, vat_p, kb_p, vt_p, km = _inproj(
        xp2, mod_p, False, tm_p, t // tm_p, g1, wm, wg, gq, gk, bd, True)
    mq_s, mk_s, mv_s, gt_s, qa_s, ka_s, va_s = _inproj(
        xs2, mod_s, True, tm_s, 1, g1, wm, wg, gq, gk, bd, False)

    zeros = functools.partial(jnp.zeros, dtype=F32)
    gp = gt_p.reshape(bsz, t, 2 * MH)
    hm_p, c_p, nn_p, m_p = _mlstm(
        mq_p.reshape(bsz, t, MW), mk_p.reshape(bsz, t, MW), mv_p.reshape(bsz, t, MW),
        gp.transpose(0, 2, 1), gp, b_gates[l], zeros((bsz, MH, MD, MD)), zeros((bsz, MH, MD)),
        zeros((bsz, MH, MD)), _pick_tile(bsz, 2), t, MCHUNK)
    ls = 16
    assert ts <= ls
    pad_t = lambda a: jnp.pad(a.reshape(db, ts, -1), ((0, 0), (0, ls - ts), (0, 0)))
    gs_ = pad_t(gt_s)
    m0 = jnp.broadcast_to(state_mlstm_m[l].astype(F32)[:, :, None], (db, MH, MD))
    hm_s, c_sm, nn_s, m_sm = _mlstm(
        pad_t(mq_s), pad_t(mk_s), pad_t(mv_s), gs_.transpose(0, 2, 1), gs_, b_gates[l],
        state_mlstm_C[l].astype(F32), state_mlstm_n[l].astype(F32), m0, _pick_tile(db, 8), ts, ls)
    hm_s = hm_s[:, :ts].reshape(n_s, MW)

    nb = t // BLK
    ha_p = _moba_prompt(qa_p.reshape(bsz, t, AW), kb_p.reshape(bsz, nb, BLK, AW), vt_p.reshape(bsz, nb, AW, BLK),
                        km.reshape(bsz, nb, AW), slopes).reshape(n_p, AW)
    qh = qa_s.astype(F32).reshape(db, ts, AH, AD).transpose(0, 2, 1, 3)
    qbd = (qh[:, :, :, None, :] * jnp.eye(AH, dtype=F32)[None, :, None, :, None]).reshape(db, AH * ts, AW)
    tn = 8
    pad_n = lambda a: jnp.pad(a.reshape(db, ts, AW), ((0, 0), (0, tn - ts), (0, 0)))
    to_t = lambda c: jnp.transpose(c[l], (0, 2, 3, 1))
    ha_s = _moba_sample_m(page_table, to_t(cache_k), to_t(cache_v), qbd, pad_n(ka_s), pad_n(va_s), ts)
    ha_s = ha_s.reshape(n_s, AW)

    tmm_p = _pick_tile(t, 256)
    base_p, h2_p, idx_p, wt_p, rk_p, cnt1 = _merge(
        xp2, hm_p.reshape(n_p, MW), ha_p, mod_p, False, tmm_p, t // tmm_p, merge_w, zeros((1, NE)))
    base_s, h2_s, idx_s, wt_s, rk_s, cnt2 = _merge(
        xs2, hm_s, ha_s, mod_s, True, tm_s, 1, merge_w, cnt1)

    n_all = n_p + n_s
    cnt = cnt2.reshape(NE).astype(I32)
    padded = (cnt + GROUP - 1) // GROUP * GROUP
    pend = jnp.cumsum(padded)
    pstart = pend - padded
    n_blocks = -(-(n_all * TOPK + NE * (GROUP - 1)) // GROUP)
    n_used = (pend[-1] // GROUP).reshape(1)
    blk = jnp.minimum(jnp.arange(n_blocks, dtype=I32), n_used[0] - 1) * GROUP
    blk_e = jnp.minimum(jnp.sum((pend[None, :] <= blk[:, None]).astype(I32), axis=1), NE - 1)
    ids = jnp.arange(NE, dtype=I32)
    later = (ids[None, :] > ids[:, None]) & (cnt[None, :] > 0)
    next_e = jnp.min(jnp.where(later, ids[None, :], NE), axis=1)
    slot_e = (jnp.cumsum((cnt > 0).astype(I32)) - 1) % 2
    blk_next = next_e[blk_e].astype(I32)
    blk_slot = slot_e[blk_e].astype(I32)

    h2 = jnp.concatenate([h2_p, h2_s], axis=0)
    dest_p = _dest(idx_p, rk_p, pstart)
    dest_s = _dest(idx_s, rk_s, pstart)
    td = _pick_tile(n_all, 128)
    xs = _dispatch(pend, cnt, jnp.concatenate([dest_p, dest_s], axis=0), h2, n_blocks * GROUP, td)
    ys = _experts(blk_e, blk_next, blk_slot, n_used, xs, w_gate[l], w_up[l], w_down[l])
    tc_p = _pick_tile(t, 64)
    y_p = _combine(dest_p, ys, wt_p, base_p, mod_p, False, tc_p, t // tc_p)
    tc_s = _pick_tile(tm_s, 64)
    mod_sc = mod_s.reshape(n_s // tc_s, tc_s, 6 * D)
    y_s = _combine(dest_s, ys, wt_s, base_s, mod_sc, True, tc_s, 1)

    st = lambda a: a[None]
    return (y_p.reshape(bsz, t, D), y_s.reshape(db, ts, D),
            st(kat_p.reshape(bsz, AH, AD, t).transpose(0, 3, 1, 2)),
            st(vat_p.reshape(bsz, AH, AD, t).transpose(0, 3, 1, 2)),
            st(c_p), st(nn_p), st(m_p[:, :, 0]),
            st(ka_s.reshape(db, ts, AH, AD)), st(va_s.reshape(db, ts, AH, AD)),
            st(c_sm), st(nn_s), st(m_sm[:, :, 0]))
```

```python
import functools

import jax
import jax.numpy as jnp
from jax import lax
from jax.experimental import pallas as pl
from jax.experimental.pallas import tpu as pltpu

F32, BF16, I32 = jnp.float32, jnp.bfloat16, jnp.int32

D = 1024
MH, MD = 4, 128
AH, AD = 8, 64
MW, AW = MH * MD, AH * AD
MCHUNK = 128
BLK = 256
TOPB = 3
NE, TOPK, DE, DSH = 256, 8, 256, 256
ROUTE_SCALE = 2.5
GROUP = 256
EPS = 1e-6
NEG = -1e30
MASKV = -3.0e38
LOG2E = 1.4426950408889634
VMEM_LIMIT = 56 * 1024 * 1024
_OFF = {}
_o = 0
for _n, _w in (("mq", MW), ("mk", MW), ("mv", MW), ("mo", MW), ("mi", MH), ("mf", MH),
               ("aq", AW), ("ak", AW), ("av", AW), ("ga", D), ("gb", D)):
    _OFF[_n] = (_o, _o + _w)
    _o += _w


def _cparams(*sem):
    return pltpu.CompilerParams(dimension_semantics=sem, vmem_limit_bytes=VMEM_LIMIT)


def _mm(a, b):
    return jnp.dot(a.astype(BF16), b.astype(BF16), preferred_element_type=F32)


def _mm_nt(a, b):
    return lax.dot_general(a.astype(BF16), b.astype(BF16), (((1,), (1,)), ((), ())),
                           preferred_element_type=F32)


def _mm_tn(a, b):
    return lax.dot_general(a.astype(BF16), b.astype(BF16), (((0,), (0,)), ((), ())),
                           preferred_element_type=F32)


def _split2(x):
    hi = x.astype(BF16)
    return hi, (x - hi.astype(F32)).astype(BF16)


def _mm_x3(a, b):
    ah, al = _split2(a)
    bh, bl = _split2(b)
    d = functools.partial(jnp.dot, preferred_element_type=F32)
    return d(ah, bh) + d(al, bh) + d(ah, bl)


def _rms(x, g):
    return x * lax.rsqrt(jnp.mean(x * x, axis=-1, keepdims=True) + EPS) * g


def _silu(x):
    return x * jax.nn.sigmoid(x)


def _logsig(x):
    return jnp.minimum(x, 0.0) - jnp.log1p(jnp.exp(-jnp.abs(x)))


LANES = 128
RT = D // LANES


def _to_row_tiles(ref, x, idx=()):
    rows = x.shape[0]
    for s in range(RT):
        ref[idx + (pl.ds(s, rows, stride=RT), slice(None))] = x[:, s * LANES:(s + 1) * LANES]


def _from_row_tiles(ref, rows, idx=()):
    return jnp.concatenate([ref[idx + (pl.ds(s, rows, stride=RT), slice(None))] for s in range(RT)], axis=1)


def _ada_kernel(c_ref, w_ref, b_ref, o_ref):
    o_ref[...] = _mm_x3(_silu(c_ref[...]), w_ref[...]) + b_ref[...]


def _ada(c_all, w_ada, b_ada):
    r = c_all.shape[0]
    return pl.pallas_call(
        _ada_kernel, grid=(6,),
        in_specs=[pl.BlockSpec((r, D), lambda j: (0, 0)),
                  pl.BlockSpec((D, D), lambda j: (0, j)),
                  pl.BlockSpec((1, D), lambda j: (0, j))],
        out_specs=pl.BlockSpec((r, D), lambda j: (0, j)),
        out_shape=jax.ShapeDtypeStruct((r, 6 * D), F32),
        compiler_params=_cparams("parallel"), name="ada")(c_all, w_ada, b_ada.reshape(1, 6 * D))


def _mod_spec(per_token, tm, tiles_per_batch, chunk):
    if per_token:
        return pl.BlockSpec((1, tm, D), lambda i, *_: (i, 0, chunk))
    return pl.BlockSpec((1, 1, D), lambda i, *_: (i // tiles_per_batch, 0, chunk))


def _group_ms(x, bd):
    hi, lo = _split2(x * x)
    d = functools.partial(jnp.dot, preferred_element_type=F32)
    return d(hi, bd) + d(lo, bd)


def _inproj_kernel(x_ref, sh_ref, sc_ref, g1_ref, wm_ref, wg_ref, gq_ref, gk_ref, bd_ref,
                   mq_ref, mk_ref, mv_ref, gt_ref, qa_ref, *kv_refs):
    h = _rms(x_ref[...], g1_ref[...]) * (1.0 + sc_ref[0]) + sh_ref[0]
    hb = h.astype(BF16)
    d = functools.partial(jnp.dot, preferred_element_type=F32)
    mq_ref[...] = d(hb, wm_ref[:, 0:MW]).astype(BF16)
    mk_ref[...] = (d(hb, wm_ref[:, MW:2 * MW]) * (MD ** -0.5)).astype(BF16)
    mv_ref[...] = d(hb, wm_ref[:, 2 * MW:3 * MW]).astype(BF16)
    o = 3 * MW
    aq = d(hb, wm_ref[:, o:o + AW])
    ak = d(hb, wm_ref[:, o + AW:o + 2 * AW])
    av = d(hb, wm_ref[:, o + 2 * AW:o + 3 * AW])
    bd = bd_ref[...]
    qa = aq * lax.rsqrt(_group_ms(aq, bd) + EPS) * gq_ref[...]
    ka = ak * lax.rsqrt(_group_ms(ak, bd) + EPS) * gk_ref[...]
    qa_ref[...] = qa.astype(BF16)
    gt_ref[...] = d(hb, wg_ref[...])[:, 0:2 * MH]
    if len(kv_refs) == 2:
        ka_ref, va_ref = kv_refs
        ka_ref[...] = ka
        va_ref[...] = av
    else:
        kat_ref, vat_ref, kb_ref, vt_ref, km_ref = kv_refs
        for r in range(vt_ref.shape[0]):
            blk = slice(r * BLK, (r + 1) * BLK)
            kat_ref[0, :, blk] = ka[blk, :].T
            vbt = av[blk, :].T
            vat_ref[0, :, blk] = vbt
            kb_ref[r] = ka[blk, :].astype(BF16)
            vt_ref[r] = vbt.astype(BF16)
            km_ref[r] = jnp.mean(ka[blk, :], axis=0, keepdims=True)


def _inproj(x2, mod3, per_token, tm, tiles_per_batch, g1, wm, wg, gq, gk, bd, attn_layouts):
    n = x2.shape[0]
    row = lambda w: pl.BlockSpec((tm, w), lambda i: (i, 0))
    full = lambda a: pl.BlockSpec(a.shape, lambda i: (0,) * a.ndim)
    sds = jax.ShapeDtypeStruct
    out_specs = [row(MW), row(MW), row(MW), row(2 * MH), row(AW)]
    out_shape = [sds((n, MW), BF16), sds((n, MW), BF16), sds((n, MW), BF16), sds((n, 2 * MH), F32),
                 sds((n, AW), BF16)]
    if attn_layouts:
        tpb = tiles_per_batch
        pos_minor = pl.BlockSpec((1, AW, tm), lambda i: (i // tpb, 0, i % tpb))
        out_specs += [pos_minor, pos_minor,
                      pl.BlockSpec((tm // BLK, BLK, AW), lambda i: (i, 0, 0)),
                      pl.BlockSpec((tm // BLK, AW, BLK), lambda i: (i, 0, 0)),
                      pl.BlockSpec((tm // BLK, 1, AW), lambda i: (i, 0, 0))]
        out_shape += [sds((n // (tpb * tm), AW, tpb * tm), F32), sds((n // (tpb * tm), AW, tpb * tm), F32),
                      sds((n // BLK, BLK, AW), BF16), sds((n // BLK, AW, BLK), BF16),
                      sds((n // BLK, 1, AW), F32)]
    else:
        out_specs += [row(AW), row(AW)]
        out_shape += [sds((n, AW), F32), sds((n, AW), F32)]
    return pl.pallas_call(
        _inproj_kernel, grid=(n // tm,),
        in_specs=[row(D), _mod_spec(per_token, tm, tiles_per_batch, 0),
                  _mod_spec(per_token, tm, tiles_per_batch, 1),
                  full(g1), full(wm), full(wg), full(gq), full(gk), full(bd)],
        out_specs=out_specs, out_shape=out_shape,
        compiler_params=_cparams("parallel"), name="inproj")(x2, mod3, mod3, g1, wm, wg, gq, gk, bd)


def _mlstm_kernel(q_ref, k_ref, v_ref, gr_ref, gc_ref, bgc_ref, bgr_ref, c0_ref, n0_ref, m0_ref,
                  h_ref, cn_ref, nn_ref, mn_ref, c_s, n_s, m_s, *, bb, t_valid, L):
    c = pl.program_id(1)

    @pl.when(c == 0)
    def _init():
        c_s[...] = c0_ref[...]
        n_s[...] = n0_ref[...]
        m_s[...] = m0_ref[...]

    rowi = lax.broadcasted_iota(I32, (L, L), 0)
    coli = lax.broadcasted_iota(I32, (L, L), 1)
    tri = rowi >= coli
    ok_r = (c * L + lax.broadcasted_iota(I32, (1, L), 1)) < t_valid
    ok_c = (c * L + lax.broadcasted_iota(I32, (L, 1), 0)) < t_valid
    for b in range(bb):
        g_r = gr_ref[b] + bgc_ref[...]
        g_c = gc_ref[b] + bgr_ref[...]
        for h in range(MH):
            li_r = jnp.where(ok_r, g_r[h:h + 1, :], NEG)
            lf_r = jnp.where(ok_r, _logsig(g_r[MH + h:MH + h + 1, :]), 0.0)
            li_c = jnp.where(ok_c, g_c[:, h:h + 1], NEG)
            lf_c = jnp.where(ok_c, _logsig(g_c[:, MH + h:MH + h + 1]), 0.0)
            b_c = jnp.sum(jnp.where(tri, lf_r, 0.0), axis=1, keepdims=True)
            b_r = jnp.sum(jnp.where(rowi <= coli, lf_c, 0.0), axis=0, keepdims=True)
            q = q_ref[b, :, h * MD:(h + 1) * MD]
            k = k_ref[b, :, h * MD:(h + 1) * MD]
            v = v_ref[b, :, h * MD:(h + 1) * MD]
            cm = c_s[b, h]
            nv = n_s[b, h:h + 1, :]
            m_prev = m_s[b, h:h + 1, 0:1]
            inter = m_prev + b_c
            intra = jnp.where(tri, li_r + b_c - b_r, NEG)
            m_t = jnp.maximum(inter, jnp.max(intra, axis=1, keepdims=True))
            w_inter = jnp.exp(inter - m_t)
            s = _mm_nt(q, k) * jnp.exp(intra - m_t)
            num = w_inter * _mm_nt(q, cm) + _mm(s, v)
            qn = jnp.sum(q.astype(F32) * nv, axis=1, keepdims=True)
            den = w_inter * qn + jnp.sum(s, axis=1, keepdims=True)
            h_ref[b, :, h * MD:(h + 1) * MD] = num / jnp.maximum(jnp.abs(den), jnp.exp(-m_t))
            m_last = m_t[L - 1:L, :]
            b_last = b_c[L - 1:L, :]
            w_c = jnp.exp(m_prev + b_last - m_last)
            w_s = jnp.exp(li_c + b_last - b_c - m_last)
            c_s[b, h] = w_c * cm + _mm_tn(v.astype(F32) * w_s, k)
            n_s[b, h:h + 1, :] = w_c * nv + jnp.sum(k.astype(F32) * w_s, axis=0, keepdims=True)
            m_s[b, h:h + 1, :] = jnp.broadcast_to(m_last, (1, MD))

    @pl.when(c == pl.num_programs(1) - 1)
    def _fin():
        cn_ref[...] = c_s[...]
        nn_ref[...] = n_s[...]
        mn_ref[...] = m_s[...]


def _mlstm(q, k, v, g_row, g_col, b_gates, c0, n0, m0, bb, t_valid, L):
    bsz, t, _ = q.shape
    seq = pl.BlockSpec((bb, L, MW), lambda g, c: (g, c, 0))
    st4 = pl.BlockSpec((bb, MH, MD, MD), lambda g, c: (g, 0, 0, 0))
    st3 = pl.BlockSpec((bb, MH, MD), lambda g, c: (g, 0, 0))
    sds = jax.ShapeDtypeStruct
    return pl.pallas_call(
        functools.partial(_mlstm_kernel, bb=bb, t_valid=t_valid, L=L),
        grid=(bsz // bb, t // L),
        in_specs=[seq, seq, seq,
                  pl.BlockSpec((bb, 2 * MH, L), lambda g, c: (g, 0, c)),
                  pl.BlockSpec((bb, L, 2 * MH), lambda g, c: (g, c, 0)),
                  pl.BlockSpec((2 * MH, 1), lambda g, c: (0, 0)),
                  pl.BlockSpec((1, 2 * MH), lambda g, c: (0, 0)),
                  st4, st3, st3],
        out_specs=[seq, st4, st3, st3],
        out_shape=[sds((bsz, t, MW), F32), sds((bsz, MH, MD, MD), F32),
                   sds((bsz, MH, MD), F32), sds((bsz, MH, MD), F32)],
        scratch_shapes=[pltpu.VMEM((bb, MH, MD, MD), F32), pltpu.VMEM((bb, MH, MD), F32),
                        pltpu.VMEM((bb, MH, MD), F32)],
        compiler_params=_cparams("parallel", "arbitrary"), name="mlstm")(
            q, k, v, g_row, g_col, b_gates.reshape(2 * MH, 1), b_gates.reshape(1, 2 * MH), c0, n0, m0)


def _select_blocks_t(gate_t, own, n_sel):
    nb = gate_t.shape[0]
    row = lax.broadcasted_iota(I32, gate_t.shape, 0).astype(F32)
    g = jnp.where(row < own, gate_t, NEG)
    sel = jnp.zeros(gate_t.shape, F32)
    for _ in range(n_sel):
        mx = jnp.max(g, axis=0, keepdims=True)
        idx = jnp.min(jnp.where(g == mx, row, float(nb)), axis=0, keepdims=True)
        hit = row == idx
        sel = jnp.where(hit & (idx < own), 1.0, sel)
        g = jnp.where(hit, MASKV, g)
    return sel


def _moba_p_kernel(slope_ref, q_ref, k_ref, vt_ref, km_ref, o_ref, a_s, s_s, bias_s, qs_s, m_s, l_s, acc_s):
    b = pl.program_id(0)
    i = pl.program_id(1)
    nb = km_ref.shape[1]
    rel_t = lax.broadcasted_iota(I32, (BLK, BLK), 1) - lax.broadcasted_iota(I32, (BLK, BLK), 0)

    @pl.when((b == 0) & (i == 0))
    def _alibi():
        relf = rel_t.astype(F32)
        for h in range(AH):
            a_s[h] = (slope_ref[h] * LOG2E) * relf

    causal = rel_t >= 0
    own = i.astype(F32)
    for h in range(AH):
        hs = slice(h * AD, (h + 1) * AD)
        q = q_ref[0, :, hs]
        kmh, kml = _split2(km_ref[0, :, hs])
        gate_t = _mm_nt(kmh, q) + _mm_nt(kml, q)
        sel = _select_blocks_t(gate_t, own, min(TOPB, nb))
        bias_s[h] = jnp.where(sel > 0.5, 0.0, MASKV)
        qs = (q.astype(F32) * (AD ** -0.5 * LOG2E)).astype(BF16)
        qs_s[h] = qs
        s_s[h] = jnp.where(causal, _mm_nt(k_ref[0, i, :, hs], qs) - a_s[h], MASKV)
    for h in range(AH):
        hs = slice(h * AD, (h + 1) * AD)
        m = jnp.max(s_s[h], axis=0, keepdims=True)
        p = jnp.exp2(s_s[h] - m)
        m_s[h] = m
        l_s[h] = jnp.sum(p, axis=0, keepdims=True)
        acc_s[h] = _mm(vt_ref[0, i, hs, :], p)

    def body(j, c):
        dj = ((i - j) * BLK).astype(F32)
        for h in range(AH):
            hs = slice(h * AD, (h + 1) * AD)
            s_s[h] = _mm_nt(k_ref[0, j, :, hs], qs_s[h]) - a_s[h]
        for h in range(AH):
            hs = slice(h * AD, (h + 1) * AD)
            rowterm = bias_s[h, pl.ds(j, 1), :] - (slope_ref[h] * LOG2E) * dj
            m = m_s[h]
            m_new = jnp.maximum(m, jnp.max(s_s[h], axis=0, keepdims=True) + rowterm)
            p = jnp.exp2(s_s[h] - (m_new - rowterm))
            alpha = jnp.exp2(m - m_new)
            l_s[h] = alpha * l_s[h] + jnp.sum(p, axis=0, keepdims=True)
            acc_s[h] = alpha * acc_s[h] + _mm(vt_ref[0, j, hs, :], p)
            m_s[h] = m_new
        return c

    lax.fori_loop(0, i, body, 0)
    out_t = jnp.concatenate([acc_s[h] / l_s[h] for h in range(AH)], axis=0)
    o_ref[0] = out_t.T


def _moba_prompt(q, kb, vt, km, slopes):
    bsz, t, _ = q.shape
    nb = t // BLK
    gs = pltpu.PrefetchScalarGridSpec(
        num_scalar_prefetch=1, grid=(bsz, nb),
        in_specs=[pl.BlockSpec((1, BLK, AW), lambda b, i, s: (b, i, 0)),
                  pl.BlockSpec((1, nb, BLK, AW), lambda b, i, s: (b, 0, 0, 0)),
                  pl.BlockSpec((1, nb, AW, BLK), lambda b, i, s: (b, 0, 0, 0)),
                  pl.BlockSpec((1, nb, AW), lambda b, i, s: (b, 0, 0))],
        out_specs=pl.BlockSpec((1, BLK, AW), lambda b, i, s: (b, i, 0)),
        scratch_shapes=[pltpu.VMEM((AH, BLK, BLK), F32), pltpu.VMEM((AH, BLK, BLK), F32),
                        pltpu.VMEM((AH, nb, BLK), F32),
                        pltpu.VMEM((AH, BLK, AD), BF16), pltpu.VMEM((AH, 1, BLK), F32),
                        pltpu.VMEM((AH, 1, BLK), F32), pltpu.VMEM((AH, AD, BLK), F32)])
    return pl.pallas_call(
        _moba_p_kernel, grid_spec=gs,
        out_shape=jax.ShapeDtypeStruct((bsz, t, AW), F32),
        compiler_params=_cparams("arbitrary", "arbitrary"), name="moba_p")(slopes, q, kb, vt, km)


def _moba_sm_kernel(pt_ref, qbd_ref, kn_ref, vn_ref, ck_ref, cv_ref, o_ref, kbuf, vbuf, s_s, sem,
                    *, n_pages, page, nbp, past, ts, tn):
    b = pl.program_id(0)
    slot = b % 2
    ppb = n_pages // nbp
    rows = AH * ts

    def page_copies(seq, sl, pg):
        pid = pt_ref[seq * n_pages + pg]
        return (pltpu.make_async_copy(ck_ref.at[pid], kbuf.at[sl, pg], sem.at[sl, 0]),
                pltpu.make_async_copy(cv_ref.at[pid], vbuf.at[sl, pg], sem.at[sl, 1]))

    def fetch(seq, sl):
        for pg in range(n_pages):
            for cp in page_copies(seq, sl, pg):
                cp.start()

    @pl.when(b == 0)
    def _first():
        fetch(b, slot)

    @pl.when(b + 1 < pl.num_programs(0))
    def _next():
        fetch(b + 1, 1 - slot)

    for pg in range(n_pages):
        for cp in page_copies(b, slot, pg):
            cp.wait()

    qsf = qbd_ref[0] * (AD ** -0.5)
    qs = qsf.astype(BF16)
    g = []
    for jb in range(nbp):
        gsum = jnp.zeros((rows, page), F32)
        for p in range(ppb):
            pg = jb * ppb + p
            s = _mm(qs, kbuf[slot, pg].reshape(AW, page))
            s_s[pg] = s
            gsum = gsum + s
        g.append(jnp.sum(gsum, axis=1, keepdims=True))
    rh = lax.broadcasted_iota(I32, (rows, 1), 0) // ts
    rr = lax.broadcasted_iota(I32, (rows, 1), 0) % ts
    slope = jnp.zeros((rows, 1), F32)
    for hh in range(AH):
        slope = jnp.where(rh == hh, 2.0 ** (-8.0 * (hh + 1) / AH), slope)
    picks = []
    for _ in range(min(TOPB, nbp)):
        mx = g[0]
        for jb in range(1, nbp):
            mx = jnp.maximum(mx, g[jb])
        idx = jnp.full((rows, 1), float(nbp), F32)
        for jb in reversed(range(nbp)):
            idx = jnp.where(g[jb] == mx, float(jb), idx)
        picks.append(idx)
        g = [jnp.where(idx == float(jb), MASKV, g[jb]) for jb in range(nbp)]
    lane = lax.broadcasted_iota(I32, (rows, page), 1).astype(F32)
    qpos = (past + rr).astype(F32)
    dist_n = rr - lax.broadcasted_iota(I32, (rows, tn), 1)
    s_new = lax.dot_general(qsf, kn_ref[0], (((1,), (1,)), ((), ())), preferred_element_type=F32)
    s_new = jnp.where(dist_n >= 0, s_new - slope * dist_n.astype(F32), MASKV)
    m = jnp.max(s_new, axis=1, keepdims=True)
    for pg in range(n_pages):
        blk = float(pg // ppb)
        chosen = picks[0] == blk
        for c in picks[1:]:
            chosen = chosen | (c == blk)
        s = jnp.where(chosen, s_s[pg] - slope * (qpos - (lane + float(pg * page))), MASKV)
        s_s[pg] = s
        m = jnp.maximum(m, jnp.max(s, axis=1, keepdims=True))
    p_new = jnp.exp(s_new - m)
    l = jnp.sum(p_new, axis=1, keepdims=True)
    acc = jnp.dot(p_new, vn_ref[0], preferred_element_type=F32)
    for pg in range(n_pages):
        p = jnp.exp(s_s[pg] - m)
        l = l + jnp.sum(p, axis=1, keepdims=True)
        acc = acc + _mm_nt(p, vbuf[slot, pg].reshape(AW, page))
    acc = acc / l
    lh = lax.broadcasted_iota(I32, (ts, AW), 1) // AD
    out = jnp.zeros((ts, AW), F32)
    for hh in range(AH):
        out = out + jnp.where(lh == hh, acc[hh * ts:(hh + 1) * ts, :], 0.0)
    o_ref[0] = out


def _moba_sample_m(page_table, cache_kt, cache_vt, qbd, k_new, v_new, ts):
    db, n_pages = page_table.shape
    page = cache_kt.shape[3]
    past = n_pages * page
    assert past % BLK == 0 and BLK % page == 0
    nbp = past // BLK
    rows = AH * ts
    tn = k_new.shape[1]
    gs = pltpu.PrefetchScalarGridSpec(
        num_scalar_prefetch=1, grid=(db,),
        in_specs=[pl.BlockSpec((1, rows, AW), lambda b, pt: (b, 0, 0)),
                  pl.BlockSpec((1, tn, AW), lambda b, pt: (b, 0, 0)),
                  pl.BlockSpec((1, tn, AW), lambda b, pt: (b, 0, 0)),
                  pl.BlockSpec(memory_space=pl.ANY), pl.BlockSpec(memory_space=pl.ANY)],
        out_specs=pl.BlockSpec((1, ts, AW), lambda b, pt: (b, 0, 0)),
        scratch_shapes=[pltpu.VMEM((2, n_pages, AH, AD, page), F32), pltpu.VMEM((2, n_pages, AH, AD, page), F32),
                        pltpu.VMEM((n_pages, rows, page), F32), pltpu.SemaphoreType.DMA((2, 2))])
    return pl.pallas_call(
        functools.partial(_moba_sm_kernel, n_pages=n_pages, page=page, nbp=nbp, past=past, ts=ts, tn=tn),
        grid_spec=gs, out_shape=jax.ShapeDtypeStruct((db, ts, AW), F32),
        compiler_params=_cparams("arbitrary"), name="moba_s")(
            page_table.reshape(-1), qbd, k_new, v_new, cache_kt, cache_vt)


def _merge_kernel(x_ref, hm_ref, ha_ref, sh1_ref, sc1_ref, gt1_ref, sh2_ref, sc2_ref, gt2_ref,
                  g1_ref, g2_ref, gmh_ref, wl_ref, wbm_ref, wba_ref, wo_ref, wr_ref, br_ref,
                  wsg_ref, wsu_ref, wsd_ref, cin_ref,
                  base_ref, h2_ref, idx_ref, wt_ref, rank_ref, cout_ref, cnt_s):
    i = pl.program_id(0)
    tm = x_ref.shape[0]

    @pl.when(i == 0)
    def _init():
        cnt_s[...] = cin_ref[...]

    x = x_ref[...]
    h1 = (_rms(x, g1_ref[...]) * (1.0 + sc1_ref[0]) + sh1_ref[0]).astype(BF16)
    d = functools.partial(jnp.dot, preferred_element_type=F32)
    mo = d(h1, wl_ref[:, 0:MW])
    ga = d(h1, wl_ref[:, MW:MW + D])
    gb = d(h1, wl_ref[:, MW + D:MW + 2 * D])
    hm = hm_ref[...]
    parts = []
    for h in range(MH):
        xh = hm[:, h * MD:(h + 1) * MD]
        parts.append(_rms(xh, gmh_ref[:, h * MD:(h + 1) * MD]))
    hmn = jnp.concatenate(parts, axis=1) * jax.nn.sigmoid(mo)
    merged = jax.nn.sigmoid(ga) * _mm(hmn, wbm_ref[...]) + jax.nn.sigmoid(gb) * _mm(ha_ref[...], wba_ref[...])
    x1 = x + gt1_ref[0] * _mm(merged, wo_ref[...])
    h2 = _rms(x1, g2_ref[...]) * (1.0 + sc2_ref[0]) + sh2_ref[0]
    _to_row_tiles(h2_ref, h2)
    h2b = h2.astype(BF16)
    shared = _mm(_silu(d(h2b, wsg_ref[...])) * d(h2b, wsu_ref[...]), wsd_ref[...])
    base_ref[...] = x1 + gt2_ref[0] * shared
    scores = jax.nn.sigmoid(_mm_x3(h2, wr_ref[...]))
    col = lax.broadcasted_iota(I32, (tm, NE), 1).astype(F32)
    g = scores + br_ref[...]
    idxs, wts = [], []
    onehot = jnp.zeros((tm, NE), F32)
    for _ in range(TOPK):
        mx = jnp.max(g, axis=1, keepdims=True)
        idx = jnp.min(jnp.where(g == mx, col, float(NE)), axis=1, keepdims=True)
        hit = col == idx
        idxs.append(idx)
        wts.append(jnp.sum(jnp.where(hit, scores, 0.0), axis=1, keepdims=True))
        onehot = jnp.where(hit, 1.0, onehot)
        g = jnp.where(hit, MASKV, g)
    wsum = wts[0]
    for w in wts[1:]:
        wsum = wsum + w
    ri = lax.broadcasted_iota(I32, (tm, tm), 0)
    ci = lax.broadcasted_iota(I32, (tm, tm), 1)
    before = jnp.where(ci < ri, 1.0, 0.0).astype(BF16)
    pref = cnt_s[...] + d(before, onehot.astype(BF16))
    lane8 = lax.broadcasted_iota(I32, (tm, TOPK), 1)
    idx_o = jnp.zeros((tm, TOPK), F32)
    wt_o = jnp.zeros((tm, TOPK), F32)
    rk_o = jnp.zeros((tm, TOPK), F32)
    for kk in range(TOPK):
        rk = jnp.sum(jnp.where(col == idxs[kk], pref, 0.0), axis=1, keepdims=True)
        idx_o = jnp.where(lane8 == kk, idxs[kk], idx_o)
        wt_o = jnp.where(lane8 == kk, wts[kk] / wsum * ROUTE_SCALE, wt_o)
        rk_o = jnp.where(lane8 == kk, rk, rk_o)
    idx_ref[...] = idx_o.astype(I32)
    wt_ref[...] = wt_o
    rank_ref[...] = rk_o.astype(I32)
    cnt_s[...] = cnt_s[...] + jnp.sum(onehot, axis=0, keepdims=True)
    cout_ref[...] = cnt_s[...]


def _merge(x2, hm, ha, mod3, per_token, tm, tiles_per_batch, weights, cnt_in):
    n = x2.shape[0]
    row = lambda w: pl.BlockSpec((tm, w), lambda i: (i, 0))
    full = lambda a: pl.BlockSpec(a.shape, lambda i: (0,) * a.ndim)
    mods = [_mod_spec(per_token, tm, tiles_per_batch, c) for c in range(6)]
    sds = jax.ShapeDtypeStruct
    return pl.pallas_call(
        _merge_kernel, grid=(n // tm,),
        in_specs=[row(D), row(MW), row(AW)] + mods + [full(w) for w in weights] + [full(cnt_in)],
        out_specs=[row(D), pl.BlockSpec((tm * RT, LANES), lambda i: (i, 0)), row(TOPK), row(TOPK), row(TOPK),
                   full(cnt_in)],
        out_shape=[sds((n, D), F32), sds((n * RT, LANES), F32), sds((n, TOPK), I32), sds((n, TOPK), F32),
                   sds((n, TOPK), I32), sds((1, NE), F32)],
        scratch_shapes=[pltpu.VMEM((1, NE), F32)],
        compiler_params=_cparams("arbitrary"), name="merge")(
            x2, hm, ha, *([mod3] * 6), *weights, cnt_in)


def _dest_kernel(idx_ref, rank_ref, pstart_ref, o_ref):
    tm = idx_ref.shape[0]
    col = lax.broadcasted_iota(I32, (tm, NE), 1)
    lane8 = lax.broadcasted_iota(I32, (tm, TOPK), 1)
    idx = idx_ref[...]
    out = rank_ref[...]
    for kk in range(TOPK):
        start = jnp.sum(jnp.where(col == idx[:, kk:kk + 1], pstart_ref[...], 0.0), axis=1, keepdims=True)
        out = out + jnp.where(lane8 == kk, start.astype(I32), 0)
    o_ref[...] = out


def _dest(idx, rank, pstart):
    n = idx.shape[0]
    tm = _pick_tile(n, 512)
    row = pl.BlockSpec((tm, TOPK), lambda i: (i, 0))
    return pl.pallas_call(
        _dest_kernel, grid=(n // tm,),
        in_specs=[row, row, pl.BlockSpec((1, NE), lambda i: (0, 0))],
        out_specs=row, out_shape=jax.ShapeDtypeStruct((n, TOPK), I32),
        compiler_params=_cparams("parallel"), name="dest")(idx, rank, pstart.astype(F32).reshape(1, NE))


def _row_copy(src, dst, sem):
    return pltpu.make_async_copy(src, dst, sem)


def _dispatch_kernel(pend_ref, cnt_ref, dest_ref, h_ref, xs_ref, zbuf, ring, zsem, sem):
    step = pl.program_id(0)
    last = pl.num_programs(0) - 1
    td = h_ref.shape[0] // RT
    slot = step % 2

    @pl.when(step == 0)
    def _zero_tails():
        zbuf[...] = jnp.zeros(zbuf.shape, F32)

        def tail(e):
            first = pl.multiple_of((pend_ref[e] - GROUP) * RT, RT)
            return _row_copy(zbuf, xs_ref.at[pl.ds(first, GROUP * RT)], zsem)

        def start(e, c):
            @pl.when(cnt_ref[e] > 0)
            def _():
                tail(e).start()
            return c

        def wait(e, c):
            @pl.when(cnt_ref[e] > 0)
            def _():
                tail(e).wait()
            return c

        lax.fori_loop(0, NE, start, 0)
        lax.fori_loop(0, NE, wait, 0)

    ring[slot] = h_ref[...]

    def start(t, c):
        src = ring.at[slot, pl.ds(pl.multiple_of(t * RT, RT), RT)]
        for kk in range(TOPK):
            dst = pl.multiple_of(dest_ref[0, 0, t * TOPK + kk] * RT, RT)
            _row_copy(src, xs_ref.at[pl.ds(dst, RT)], sem.at[slot]).start(priority=kk % 2)
        return c

    def drain(s):
        def wait(t, c):
            for kk in range(TOPK):
                _row_copy(ring.at[s, pl.ds(0, RT)], xs_ref.at[pl.ds(0, RT)], sem.at[s]).wait()
            return c
        lax.fori_loop(0, td, wait, 0)

    lax.fori_loop(0, td, start, 0)

    @pl.when(step > 0)
    def _prev():
        drain(1 - slot)

    @pl.when(step == last)
    def _own():
        drain(slot)


def _dispatch(pend, cnt, dest, h2t, n_rows, td):
    n = h2t.shape[0] // RT
    gs = pltpu.PrefetchScalarGridSpec(
        num_scalar_prefetch=2, grid=(n // td,),
        in_specs=[pl.BlockSpec((1, 1, td * TOPK), lambda i, *_: (i, 0, 0), memory_space=pltpu.SMEM),
                  pl.BlockSpec((td * RT, LANES), lambda i, *_: (i, 0))],
        out_specs=pl.BlockSpec(memory_space=pl.ANY),
        scratch_shapes=[pltpu.VMEM((GROUP * RT, LANES), F32), pltpu.VMEM((2, td * RT, LANES), F32),
                        pltpu.SemaphoreType.DMA(()), pltpu.SemaphoreType.DMA((2,))])
    return pl.pallas_call(
        _dispatch_kernel, grid_spec=gs, out_shape=jax.ShapeDtypeStruct((n_rows * RT, LANES), F32),
        compiler_params=_cparams("arbitrary"), name="dispatch")(
            pend, cnt, dest.reshape(n // td, 1, td * TOPK), h2t)


def _experts_kernel(be_ref, nx_ref, sl_ref, nu_ref, x_ref, wg_hbm, wu_hbm, wd_hbm, o_ref,
                    wg_f, wu_f, wd_f, wg_s, wu_s, wd_s, x_s, sem):
    b = pl.program_id(0)

    def fetch(e, slot):
        return (pltpu.make_async_copy(wg_hbm.at[e], wg_f.at[slot], sem.at[slot, 0]),
                pltpu.make_async_copy(wu_hbm.at[e], wu_f.at[slot], sem.at[slot, 1]),
                pltpu.make_async_copy(wd_hbm.at[e], wd_f.at[slot], sem.at[slot, 2]))

    @pl.when(b < nu_ref[0])
    def _run():
        slot = sl_ref[b]

        @pl.when((b == 0) | (be_ref[b] != be_ref[jnp.maximum(b - 1, 0)]))
        def _load():
            @pl.when(b == 0)
            def _first():
                for cp in fetch(be_ref[b], slot):
                    cp.start()

            for cp in fetch(be_ref[b], slot):
                cp.wait()

            @pl.when(nx_ref[b] < NE)
            def _next():
                for cp in fetch(nx_ref[b], 1 - slot):
                    cp.start()

            wg_s[...] = wg_f[slot].astype(BF16)
            wu_s[...] = wu_f[slot].astype(BF16)
            wd_s[...] = wd_f[slot].astype(BF16)

        d = functools.partial(jnp.dot, preferred_element_type=F32)
        for s in range(RT):
            x_s[:, s * LANES:(s + 1) * LANES] = x_ref[pl.ds(s, GROUP, stride=RT), :].astype(BF16)
        x = x_s[...]
        hb = _silu(d(x, wg_s[...])) * d(x, wu_s[...])
        _to_row_tiles(o_ref, d(hb.astype(BF16), wd_s[...]))


def _experts(blk_e, blk_next, blk_slot, n_used, xs, w_gate, w_up, w_down):
    n_rows = xs.shape[0] // RT
    rows = lambda b, be, nx, sl, nu: (jnp.minimum(b, nu[0] - 1), 0)
    hbm = pl.BlockSpec(memory_space=pl.ANY)
    gs = pltpu.PrefetchScalarGridSpec(
        num_scalar_prefetch=4, grid=(n_rows // GROUP,),
        in_specs=[pl.BlockSpec((GROUP * RT, LANES), rows), hbm, hbm, hbm],
        out_specs=pl.BlockSpec((GROUP * RT, LANES), rows),
        scratch_shapes=[pltpu.VMEM((2, D, DE), F32), pltpu.VMEM((2, D, DE), F32), pltpu.VMEM((2, DE, D), F32),
                        pltpu.VMEM((D, DE), BF16), pltpu.VMEM((D, DE), BF16), pltpu.VMEM((DE, D), BF16),
                        pltpu.VMEM((GROUP, D), BF16), pltpu.SemaphoreType.DMA((2, 3))])
    return pl.pallas_call(
        _experts_kernel, grid_spec=gs, out_shape=jax.ShapeDtypeStruct((n_rows * RT, LANES), F32),
        compiler_params=_cparams("arbitrary"), name="experts")(
            blk_e, blk_next, blk_slot, n_used, xs, w_gate, w_up, w_down)


def _combine_kernel(dest_ref, dnext_ref, ys_ref, wt_ref, base_ref, gt2_ref, o_ref, buf, sem):
    step = pl.program_id(0)
    last = pl.num_programs(0) - 1
    tc = base_ref.shape[0]
    slot = step % 2

    def issue(dref, s):
        def start(t, c):
            dst = pl.multiple_of(t * RT, RT)
            for kk in range(TOPK):
                src = pl.multiple_of(dref[0, 0, t * TOPK + kk] * RT, RT)
                _row_copy(ys_ref.at[pl.ds(src, RT)], buf.at[s, kk, pl.ds(dst, RT)], sem.at[s]).start(priority=kk % 2)
            return c
        lax.fori_loop(0, tc, start, 0)

    @pl.when(step == 0)
    def _first():
        issue(dest_ref, slot)

    @pl.when(step < last)
    def _next():
        issue(dnext_ref, 1 - slot)

    def wait(t, c):
        for kk in range(TOPK):
            _row_copy(ys_ref.at[pl.ds(0, RT)], buf.at[slot, kk, pl.ds(0, RT)], sem.at[slot]).wait()
        return c

    lax.fori_loop(0, tc, wait, 0)
    wt = wt_ref[...]
    acc = wt[:, 0:1] * _from_row_tiles(buf, tc, (slot, 0))
    for kk in range(1, TOPK):
        acc = acc + wt[:, kk:kk + 1] * _from_row_tiles(buf, tc, (slot, kk))
    o_ref[...] = base_ref[...] + gt2_ref[0] * acc


def _combine(dest, ys, wt, base, mod3, per_token, tc, tiles_per_batch):
    n = base.shape[0]
    nt = n // tc
    return pl.pallas_call(
        _combine_kernel, grid=(nt,),
        in_specs=[pl.BlockSpec((1, 1, tc * TOPK), lambda i: (i, 0, 0), memory_space=pltpu.SMEM),
                  pl.BlockSpec((1, 1, tc * TOPK), lambda i: (jnp.minimum(i + 1, nt - 1), 0, 0),
                               memory_space=pltpu.SMEM),
                  pl.BlockSpec(memory_space=pl.ANY),
                  pl.BlockSpec((tc, TOPK), lambda i: (i, 0)),
                  pl.BlockSpec((tc, D), lambda i: (i, 0)),
                  _mod_spec(per_token, tc, tiles_per_batch, 5)],
        out_specs=pl.BlockSpec((tc, D), lambda i: (i, 0)),
        out_shape=jax.ShapeDtypeStruct((n, D), F32),
        scratch_shapes=[pltpu.VMEM((2, TOPK, tc * RT, LANES), F32), pltpu.SemaphoreType.DMA((2,))],
        compiler_params=_cparams("arbitrary"), name="combine")(
            dest.reshape(nt, 1, tc * TOPK), dest.reshape(nt, 1, tc * TOPK), ys, wt, base, mod3)


def _pick_tile(n, pref):
    t = pref
    while n % t:
        t //= 2
    return t


def kernel(x_prompt, x_sample, cache_k, cache_v, page_table, state_mlstm_C, state_mlstm_n, state_mlstm_m,
           c_prompt, c_sample, w_ada, b_ada, g_norm1, w_in, b_gates, g_q, g_k, g_mh, w_br_m, w_br_a, w_out,
           g_norm2, w_router, b_router, w_gate, w_up, w_down, ws_gate, ws_up, ws_down):
    depth = w_ada.shape[0]
    assert depth == 1
    bsz, t, _ = x_prompt.shape
    db, ts, _ = x_sample.shape
    n_p, n_s = bsz * t, db * ts
    assert t % BLK == 0 and t % MCHUNK == 0 and ts <= 8
    l = 0

    nc = bsz + db
    ncp = -(-nc // 8) * 8
    c_all = jnp.pad(jnp.concatenate([c_prompt, c_sample], axis=0), ((0, ncp - nc), (0, 0)))
    mod = _ada(c_all, w_ada[l], b_ada[l])
    mod_p = mod[:bsz].reshape(bsz, 1, 6 * D)
    tm_s = _pick_tile(n_s, 256)
    mod_s = jnp.repeat(mod[bsz:nc], ts, axis=0).reshape(n_s // tm_s, tm_s, 6 * D)

    wi = w_in[l]
    sl = lambda name: wi[:, _OFF[name][0]:_OFF[name][1]]
    wm = jnp.concatenate([sl("mq"), sl("mk"), sl("mv"), sl("aq"), sl("ak"), sl("av")], axis=1).astype(BF16)
    wg = jnp.pad(jnp.concatenate([sl("mi"), sl("mf")], axis=1), ((0, 0), (0, 128 - 2 * MH))).astype(BF16)
    wl = jnp.concatenate([sl("mo"), sl("ga"), sl("gb")], axis=1).astype(BF16)
    g1 = g_norm1[l].reshape(1, D)
    g2 = g_norm2[l].reshape(1, D)
    gq = jnp.tile(g_q[l], AH).reshape(1, AW)
    gk = jnp.tile(g_k[l], AH).reshape(1, AW)
    gmh = g_mh[l].reshape(1, MW)
    grp = jnp.arange(AW) // AD
    bd = jnp.where(grp[:, None] == grp[None, :], 1.0 / AD, 0.0).astype(BF16)
    merge_w = (g1, g2, gmh, wl, w_br_m[l].astype(BF16), w_br_a[l].astype(BF16), w_out[l].astype(BF16),
               w_router[l], b_router[l].reshape(1, NE), ws_gate[l].astype(BF16), ws_up[l].astype(BF16),
               ws_down[l].astype(BF16))
    slopes = 2.0 ** (-8.0 * jnp.arange(1, AH + 1, dtype=F32) / AH)

    tm_p = _pick_tile(t, 512)
    xp2 = x_prompt.reshape(n_p, D)
    xs2 = x_sample.reshape(n_s, D)
    mq_p, mk_p, mv_p, gt_p, qa_p, kat_p, vat_p, kb_p, vt_p, km = _inproj(
        xp2, mod_p, False, tm_p, t // tm_p, g1, wm, wg, gq, gk, bd, True)
    mq_s, mk_s, mv_s, gt_s, qa_s, ka_s, va_s = _inproj(
        xs2, mod_s, True, tm_s, 1, g1, wm, wg, gq, gk, bd, False)

    zeros = functools.partial(jnp.zeros, dtype=F32)
    gp = gt_p.reshape(bsz, t, 2 * MH)
    hm_p, c_p, nn_p, m_p = _mlstm(
        mq_p.reshape(bsz, t, MW), mk_p.reshape(bsz, t, MW), mv_p.reshape(bsz, t, MW),
        gp.transpose(0, 2, 1), gp, b_gates[l], zeros((bsz, MH, MD, MD)), zeros((bsz, MH, MD)),
        zeros((bsz, MH, MD)), _pick_tile(bsz, 2), t, MCHUNK)
    ls = 16
    assert ts <= ls
    pad_t = lambda a: jnp.pad(a.reshape(db, ts, -1), ((0, 0), (0, ls - ts), (0, 0)))
    gs_ = pad_t(gt_s)
    m0 = jnp.broadcast_to(state_mlstm_m[l].astype(F32)[:, :, None], (db, MH, MD))
    hm_s, c_sm, nn_s, m_sm = _mlstm(
        pad_t(mq_s), pad_t(mk_s), pad_t(mv_s), gs_.transpose(0, 2, 1), gs_, b_gates[l],
        state_mlstm_C[l].astype(F32), state_mlstm_n[l].astype(F32), m0, _pick_tile(db, 8), ts, ls)
    hm_s = hm_s[:, :ts].reshape(n_s, MW)

    nb = t // BLK
    ha_p = _moba_prompt(qa_p.reshape(bsz, t, AW), kb_p.reshape(bsz, nb, BLK, AW), vt_p.reshape(bsz, nb, AW, BLK),
                        km.reshape(bsz, nb, AW), slopes).reshape(n_p, AW)
    qh = qa_s.astype(F32).reshape(db, ts, AH, AD).transpose(0, 2, 1, 3)
    qbd = (qh[:, :, :, None, :] * jnp.eye(AH, dtype=F32)[None, :, None, :, None]).reshape(db, AH * ts, AW)
    tn = 8
    pad_n = lambda a: jnp.pad(a.reshape(db, ts, AW), ((0, 0), (0, tn - ts), (0, 0)))
    to_t = lambda c: jnp.transpose(c[l], (0, 2, 3, 1))
    ha_s = _moba_sample_m(page_table, to_t(cache_k), to_t(cache_v), qbd, pad_n(ka_s), pad_n(va_s), ts)
    ha_s = ha_s.reshape(n_s, AW)

    tmm_p = _pick_tile(t, 512)
    base_p, h2_p, idx_p, wt_p, rk_p, cnt1 = _merge(
        xp2, hm_p.reshape(n_p, MW), ha_p, mod_p, False, tmm_p, t // tmm_p, merge_w, zeros((1, NE)))
    base_s, h2_s, idx_s, wt_s, rk_s, cnt2 = _merge(
        xs2, hm_s, ha_s, mod_s, True, tm_s, 1, merge_w, cnt1)

    n_all = n_p + n_s
    cnt = cnt2.reshape(NE).astype(I32)
    padded = (cnt + GROUP - 1) // GROUP * GROUP
    pend = jnp.cumsum(padded)
    pstart = pend - padded
    n_blocks = -(-(n_all * TOPK + NE * (GROUP - 1)) // GROUP)
    n_used = (pend[-1] // GROUP).reshape(1)
    blk = jnp.minimum(jnp.arange(n_blocks, dtype=I32), n_used[0] - 1) * GROUP
    blk_e = jnp.minimum(jnp.sum((pend[None, :] <= blk[:, None]).astype(I32), axis=1), NE - 1)
    ids = jnp.arange(NE, dtype=I32)
    later = (ids[None, :] > ids[:, None]) & (cnt[None, :] > 0)
    next_e = jnp.min(jnp.where(later, ids[None, :], NE), axis=1)
    slot_e = (jnp.cumsum((cnt > 0).astype(I32)) - 1) % 2
    blk_next = next_e[blk_e].astype(I32)
    blk_slot = slot_e[blk_e].astype(I32)

    h2 = jnp.concatenate([h2_p, h2_s], axis=0)
    dest_p = _dest(idx_p, rk_p, pstart)
    dest_s = _dest(idx_s, rk_s, pstart)
    td = _pick_tile(n_all, 128)
    xs = _dispatch(pend, cnt, jnp.concatenate([dest_p, dest_s], axis=0), h2, n_blocks * GROUP, td)
    ys = _experts(blk_e, blk_next, blk_slot, n_used, xs, w_gate[l], w_up[l], w_down[l])
    tc_p = _pick_tile(t, 64)
    y_p = _combine(dest_p, ys, wt_p, base_p, mod_p, False, tc_p, t // tc_p)
    tc_s = _pick_tile(tm_s, 64)
    mod_sc = mod_s.reshape(n_s // tc_s, tc_s, 6 * D)
    y_s = _combine(dest_s, ys, wt_s, base_s, mod_sc, True, tc_s, 1)

    st = lambda a: a[None]
    return (y_p.reshape(bsz, t, D), y_s.reshape(db, ts, D),
            st(kat_p.reshape(bsz, AH, AD, t).transpose(0, 3, 1, 2)),
            st(vat_p.reshape(bsz, AH, AD, t).transpose(0, 3, 1, 2)),
            st(c_p), st(nn_p), st(m_p[:, :, 0]),
            st(ka_s.reshape(db, ts, AH, AD)), st(va_s.reshape(db, ts, AH, AD)),
            st(c_sm), st(nn_s), st(m_sm[:, :, 0]))
```

```python
import functools

import jax
import jax.numpy as jnp
from jax import lax
from jax.experimental import pallas as pl
from jax.experimental.pallas import tpu as pltpu

F32, BF16, I32 = jnp.float32, jnp.bfloat16, jnp.int32

D = 1024
MH, MD = 4, 128
AH, AD = 8, 64
MW, AW = MH * MD, AH * AD
MCHUNK = 128
BLK = 256
TOPB = 3
NE, TOPK, DE, DSH = 256, 8, 256, 256
ROUTE_SCALE = 2.5
GROUP = 256
ROWSTEP = 64
EPS = 1e-6
NEG = -1e30
MASKV = -3.0e38
LOG2E = 1.4426950408889634
VMEM_LIMIT = 56 * 1024 * 1024
_OFF = {}
_o = 0
for _n, _w in (("mq", MW), ("mk", MW), ("mv", MW), ("mo", MW), ("mi", MH), ("mf", MH),
               ("aq", AW), ("ak", AW), ("av", AW), ("ga", D), ("gb", D)):
    _OFF[_n] = (_o, _o + _w)
    _o += _w


def _cparams(*sem):
    return pltpu.CompilerParams(dimension_semantics=sem, vmem_limit_bytes=VMEM_LIMIT)


def _mm(a, b):
    return jnp.dot(a.astype(BF16), b.astype(BF16), preferred_element_type=F32)


def _mm_nt(a, b):
    return lax.dot_general(a.astype(BF16), b.astype(BF16), (((1,), (1,)), ((), ())),
                           preferred_element_type=F32)


def _mm_tn(a, b):
    return lax.dot_general(a.astype(BF16), b.astype(BF16), (((0,), (0,)), ((), ())),
                           preferred_element_type=F32)


def _split2(x):
    hi = x.astype(BF16)
    return hi, (x - hi.astype(F32)).astype(BF16)


def _mm_x3(a, b):
    ah, al = _split2(a)
    bh, bl = _split2(b)
    d = functools.partial(jnp.dot, preferred_element_type=F32)
    return d(ah, bh) + d(al, bh) + d(ah, bl)


def _rms(x, g):
    return x * lax.rsqrt(jnp.mean(x * x, axis=-1, keepdims=True) + EPS) * g


def _silu(x):
    return x * jax.nn.sigmoid(x)


def _logsig(x):
    return jnp.minimum(x, 0.0) - jnp.log1p(jnp.exp(-jnp.abs(x)))


LANES = 128
RT = D // LANES


def _to_row_tiles(ref, x, idx=()):
    rows = x.shape[0]
    for s in range(RT):
        ref[idx + (pl.ds(s, rows, stride=RT), slice(None))] = x[:, s * LANES:(s + 1) * LANES]


def _from_row_tiles(ref, rows, idx=()):
    return jnp.concatenate([ref[idx + (pl.ds(s, rows, stride=RT), slice(None))] for s in range(RT)], axis=1)


def _ada_kernel(c_ref, w_ref, b_ref, o_ref):
    o_ref[...] = _mm_x3(_silu(c_ref[...]), w_ref[...]) + b_ref[...]


def _ada(c_all, w_ada, b_ada):
    r = c_all.shape[0]
    return pl.pallas_call(
        _ada_kernel, grid=(6,),
        in_specs=[pl.BlockSpec((r, D), lambda j: (0, 0)),
                  pl.BlockSpec((D, D), lambda j: (0, j)),
                  pl.BlockSpec((1, D), lambda j: (0, j))],
        out_specs=pl.BlockSpec((r, D), lambda j: (0, j)),
        out_shape=jax.ShapeDtypeStruct((r, 6 * D), F32),
        compiler_params=_cparams("parallel"), name="ada")(c_all, w_ada, b_ada.reshape(1, 6 * D))


def _mod_spec(per_token, tm, tiles_per_batch, chunk):
    if per_token:
        return pl.BlockSpec((1, tm, D), lambda i, *_: (i, 0, chunk))
    return pl.BlockSpec((1, 1, D), lambda i, *_: (i // tiles_per_batch, 0, chunk))


def _group_ms(x, bd):
    hi, lo = _split2(x * x)
    d = functools.partial(jnp.dot, preferred_element_type=F32)
    return d(hi, bd) + d(lo, bd)


def _inproj_kernel(x_ref, sh_ref, sc_ref, g1_ref, wm_ref, wg_ref, gq_ref, gk_ref, bd_ref,
                   mq_ref, mk_ref, mv_ref, gt_ref, qa_ref, *kv_refs):
    h = _rms(x_ref[...], g1_ref[...]) * (1.0 + sc_ref[0]) + sh_ref[0]
    hb = h.astype(BF16)
    d = functools.partial(jnp.dot, preferred_element_type=F32)
    mq_ref[...] = d(hb, wm_ref[:, 0:MW]).astype(BF16)
    mk_ref[...] = (d(hb, wm_ref[:, MW:2 * MW]) * (MD ** -0.5)).astype(BF16)
    mv_ref[...] = d(hb, wm_ref[:, 2 * MW:3 * MW]).astype(BF16)
    o = 3 * MW
    aq = d(hb, wm_ref[:, o:o + AW])
    ak = d(hb, wm_ref[:, o + AW:o + 2 * AW])
    av = d(hb, wm_ref[:, o + 2 * AW:o + 3 * AW])
    bd = bd_ref[...]
    qa = aq * lax.rsqrt(_group_ms(aq, bd) + EPS) * gq_ref[...]
    ka = ak * lax.rsqrt(_group_ms(ak, bd) + EPS) * gk_ref[...]
    qa_ref[...] = qa.astype(BF16)
    gt_ref[...] = d(hb, wg_ref[...])[:, 0:2 * MH]
    if len(kv_refs) == 2:
        ka_ref, va_ref = kv_refs
        ka_ref[...] = ka
        va_ref[...] = av
    else:
        kat_ref, vat_ref, kb_ref, vt_ref, km_ref = kv_refs
        for r in range(vt_ref.shape[0]):
            blk = slice(r * BLK, (r + 1) * BLK)
            kat_ref[0, :, blk] = ka[blk, :].T
            vbt = av[blk, :].T
            vat_ref[0, :, blk] = vbt
            kb_ref[r] = ka[blk, :].astype(BF16)
            vt_ref[r] = vbt.astype(BF16)
            km_ref[r] = jnp.mean(ka[blk, :], axis=0, keepdims=True)


def _inproj(x2, mod3, per_token, tm, tiles_per_batch, g1, wm, wg, gq, gk, bd, attn_layouts):
    n = x2.shape[0]
    row = lambda w: pl.BlockSpec((tm, w), lambda i: (i, 0))
    full = lambda a: pl.BlockSpec(a.shape, lambda i: (0,) * a.ndim)
    sds = jax.ShapeDtypeStruct
    out_specs = [row(MW), row(MW), row(MW), row(2 * MH), row(AW)]
    out_shape = [sds((n, MW), BF16), sds((n, MW), BF16), sds((n, MW), BF16), sds((n, 2 * MH), F32),
                 sds((n, AW), BF16)]
    if attn_layouts:
        tpb = tiles_per_batch
        pos_minor = pl.BlockSpec((1, AW, tm), lambda i: (i // tpb, 0, i % tpb))
        out_specs += [pos_minor, pos_minor,
                      pl.BlockSpec((tm // BLK, BLK, AW), lambda i: (i, 0, 0)),
                      pl.BlockSpec((tm // BLK, AW, BLK), lambda i: (i, 0, 0)),
                      pl.BlockSpec((tm // BLK, 1, AW), lambda i: (i, 0, 0))]
        out_shape += [sds((n // (tpb * tm), AW, tpb * tm), F32), sds((n // (tpb * tm), AW, tpb * tm), F32),
                      sds((n // BLK, BLK, AW), BF16), sds((n // BLK, AW, BLK), BF16),
                      sds((n // BLK, 1, AW), F32)]
    else:
        out_specs += [row(AW), row(AW)]
        out_shape += [sds((n, AW), F32), sds((n, AW), F32)]
    return pl.pallas_call(
        _inproj_kernel, grid=(n // tm,),
        in_specs=[row(D), _mod_spec(per_token, tm, tiles_per_batch, 0),
                  _mod_spec(per_token, tm, tiles_per_batch, 1),
                  full(g1), full(wm), full(wg), full(gq), full(gk), full(bd)],
        out_specs=out_specs, out_shape=out_shape,
        compiler_params=_cparams("parallel"), name="inproj")(x2, mod3, mod3, g1, wm, wg, gq, gk, bd)


def _mlstm_kernel(q_ref, k_ref, v_ref, gr_ref, gc_ref, bgc_ref, bgr_ref, c0_ref, n0_ref, m0_ref,
                  h_ref, cn_ref, nn_ref, mn_ref, c_s, n_s, m_s, *, bb, t_valid, L):
    c = pl.program_id(1)

    @pl.when(c == 0)
    def _init():
        c_s[...] = c0_ref[...]
        n_s[...] = n0_ref[...]
        m_s[...] = m0_ref[...]

    rowi = lax.broadcasted_iota(I32, (L, L), 0)
    coli = lax.broadcasted_iota(I32, (L, L), 1)
    tri = rowi >= coli
    ok_r = (c * L + lax.broadcasted_iota(I32, (1, L), 1)) < t_valid
    ok_c = (c * L + lax.broadcasted_iota(I32, (L, 1), 0)) < t_valid
    for b in range(bb):
        g_r = gr_ref[b] + bgc_ref[...]
        g_c = gc_ref[b] + bgr_ref[...]
        for h in range(MH):
            li_r = jnp.where(ok_r, g_r[h:h + 1, :], NEG)
            lf_r = jnp.where(ok_r, _logsig(g_r[MH + h:MH + h + 1, :]), 0.0)
            li_c = jnp.where(ok_c, g_c[:, h:h + 1], NEG)
            lf_c = jnp.where(ok_c, _logsig(g_c[:, MH + h:MH + h + 1]), 0.0)
            b_c = jnp.sum(jnp.where(tri, lf_r, 0.0), axis=1, keepdims=True)
            b_r = jnp.sum(jnp.where(rowi <= coli, lf_c, 0.0), axis=0, keepdims=True)
            q = q_ref[b, :, h * MD:(h + 1) * MD]
            k = k_ref[b, :, h * MD:(h + 1) * MD]
            v = v_ref[b, :, h * MD:(h + 1) * MD]
            cm = c_s[b, h]
            nv = n_s[b, h:h + 1, :]
            m_prev = m_s[b, h:h + 1, 0:1]
            inter = m_prev + b_c
            intra = jnp.where(tri, li_r + b_c - b_r, NEG)
            m_t = jnp.maximum(inter, jnp.max(intra, axis=1, keepdims=True))
            w_inter = jnp.exp(inter - m_t)
            s = _mm_nt(q, k) * jnp.exp(intra - m_t)
            num = w_inter * _mm_nt(q, cm) + _mm(s, v)
            qn = jnp.sum(q.astype(F32) * nv, axis=1, keepdims=True)
            den = w_inter * qn + jnp.sum(s, axis=1, keepdims=True)
            h_ref[b, :, h * MD:(h + 1) * MD] = num / jnp.maximum(jnp.abs(den), jnp.exp(-m_t))
            m_last = m_t[L - 1:L, :]
            b_last = b_c[L - 1:L, :]
            w_c = jnp.exp(m_prev + b_last - m_last)
            w_s = jnp.exp(li_c + b_last - b_c - m_last)
            c_s[b, h] = w_c * cm + _mm_tn(v.astype(F32) * w_s, k)
            n_s[b, h:h + 1, :] = w_c * nv + jnp.sum(k.astype(F32) * w_s, axis=0, keepdims=True)
            m_s[b, h:h + 1, :] = jnp.broadcast_to(m_last, (1, MD))

    @pl.when(c == pl.num_programs(1) - 1)
    def _fin():
        cn_ref[...] = c_s[...]
        nn_ref[...] = n_s[...]
        mn_ref[...] = m_s[...]


def _mlstm(q, k, v, g_row, g_col, b_gates, c0, n0, m0, bb, t_valid, L):
    bsz, t, _ = q.shape
    seq = pl.BlockSpec((bb, L, MW), lambda g, c: (g, c, 0))
    st4 = pl.BlockSpec((bb, MH, MD, MD), lambda g, c: (g, 0, 0, 0))
    st3 = pl.BlockSpec((bb, MH, MD), lambda g, c: (g, 0, 0))
    sds = jax.ShapeDtypeStruct
    return pl.pallas_call(
        functools.partial(_mlstm_kernel, bb=bb, t_valid=t_valid, L=L),
        grid=(bsz // bb, t // L),
        in_specs=[seq, seq, seq,
                  pl.BlockSpec((bb, 2 * MH, L), lambda g, c: (g, 0, c)),
                  pl.BlockSpec((bb, L, 2 * MH), lambda g, c: (g, c, 0)),
                  pl.BlockSpec((2 * MH, 1), lambda g, c: (0, 0)),
                  pl.BlockSpec((1, 2 * MH), lambda g, c: (0, 0)),
                  st4, st3, st3],
        out_specs=[seq, st4, st3, st3],
        out_shape=[sds((bsz, t, MW), F32), sds((bsz, MH, MD, MD), F32),
                   sds((bsz, MH, MD), F32), sds((bsz, MH, MD), F32)],
        scratch_shapes=[pltpu.VMEM((bb, MH, MD, MD), F32), pltpu.VMEM((bb, MH, MD), F32),
                        pltpu.VMEM((bb, MH, MD), F32)],
        compiler_params=_cparams("parallel", "arbitrary"), name="mlstm")(
            q, k, v, g_row, g_col, b_gates.reshape(2 * MH, 1), b_gates.reshape(1, 2 * MH), c0, n0, m0)


def _select_blocks_t(gate_t, own, n_sel):
    nb = gate_t.shape[0]
    row = lax.broadcasted_iota(I32, gate_t.shape, 0).astype(F32)
    g = jnp.where(row < own, gate_t, NEG)
    sel = jnp.zeros(gate_t.shape, F32)
    for _ in range(n_sel):
        mx = jnp.max(g, axis=0, keepdims=True)
        idx = jnp.min(jnp.where(g == mx, row, float(nb)), axis=0, keepdims=True)
        hit = row == idx
        sel = jnp.where(hit & (idx < own), 1.0, sel)
        g = jnp.where(hit, MASKV, g)
    return sel


def _moba_p_kernel(slope_ref, q_ref, k_ref, vt_ref, km_ref, o_ref, a_s, s_s, s2_s, bias_s, qs_s, m_s, l_s, acc_s):
    b = pl.program_id(0)
    i = pl.program_id(1)
    nb = km_ref.shape[1]
    rel_t = lax.broadcasted_iota(I32, (BLK, BLK), 1) - lax.broadcasted_iota(I32, (BLK, BLK), 0)

    @pl.when((b == 0) & (i == 0))
    def _alibi():
        relf = rel_t.astype(F32)
        for h in range(AH):
            a_s[h] = (slope_ref[h] * LOG2E) * relf

    causal = rel_t >= 0
    own = i.astype(F32)
    for h in range(AH):
        hs = slice(h * AD, (h + 1) * AD)
        q = q_ref[0, :, hs]
        kmh, kml = _split2(km_ref[0, :, hs])
        gate_t = _mm_nt(kmh, q) + _mm_nt(kml, q)
        sel = _select_blocks_t(gate_t, own, min(TOPB, nb))
        bias_s[h] = jnp.where(sel > 0.5, 0.0, MASKV)
        qs = (q.astype(F32) * (AD ** -0.5 * LOG2E)).astype(BF16)
        qs_s[h] = qs
        s_s[h] = jnp.where(causal, _mm_nt(k_ref[0, i, :, hs], qs) - a_s[h], MASKV)
    for h in range(AH):
        hs = slice(h * AD, (h + 1) * AD)
        m = jnp.max(s_s[h], axis=0, keepdims=True)
        p = jnp.exp2(s_s[h] - m)
        m_s[h] = m
        l_s[h] = jnp.sum(p, axis=0, keepdims=True)
        acc_s[h] = _mm(vt_ref[0, i, hs, :], p)

    def scores(j, buf):
        for h in range(AH):
            hs = slice(h * AD, (h + 1) * AD)
            buf[h] = _mm_nt(k_ref[0, j, :, hs], qs_s[h]) - a_s[h]

    def update(j, buf):
        dj = ((i - j) * BLK).astype(F32)
        for h in range(AH):
            hs = slice(h * AD, (h + 1) * AD)
            rowterm = bias_s[h, pl.ds(j, 1), :] - (slope_ref[h] * LOG2E) * dj
            m = m_s[h]
            m_new = jnp.maximum(m, jnp.max(buf[h], axis=0, keepdims=True) + rowterm)
            p = jnp.exp2(buf[h] - (m_new - rowterm))
            alpha = jnp.exp2(m - m_new)
            l_s[h] = alpha * l_s[h] + jnp.sum(p, axis=0, keepdims=True)
            acc_s[h] = alpha * acc_s[h] + _mm(vt_ref[0, j, hs, :], p)
            m_s[h] = m_new

    def pair(jj, c):
        scores(2 * jj, s_s)
        scores(2 * jj + 1, s2_s)
        update(2 * jj, s_s)
        update(2 * jj + 1, s2_s)
        return c

    lax.fori_loop(0, i // 2, pair, 0)

    @pl.when(i % 2 == 1)
    def _odd():
        scores(i - 1, s_s)
        update(i - 1, s_s)

    out_t = jnp.concatenate([acc_s[h] / l_s[h] for h in range(AH)], axis=0)
    o_ref[0] = out_t.T


def _moba_prompt(q, kb, vt, km, slopes):
    bsz, t, _ = q.shape
    nb = t // BLK
    gs = pltpu.PrefetchScalarGridSpec(
        num_scalar_prefetch=1, grid=(bsz, nb),
        in_specs=[pl.BlockSpec((1, BLK, AW), lambda b, i, s: (b, i, 0)),
                  pl.BlockSpec((1, nb, BLK, AW), lambda b, i, s: (b, 0, 0, 0)),
                  pl.BlockSpec((1, nb, AW, BLK), lambda b, i, s: (b, 0, 0, 0)),
                  pl.BlockSpec((1, nb, AW), lambda b, i, s: (b, 0, 0))],
        out_specs=pl.BlockSpec((1, BLK, AW), lambda b, i, s: (b, i, 0)),
        scratch_shapes=[pltpu.VMEM((AH, BLK, BLK), F32), pltpu.VMEM((AH, BLK, BLK), F32),
                        pltpu.VMEM((AH, BLK, BLK), F32), pltpu.VMEM((AH, nb, BLK), F32),
                        pltpu.VMEM((AH, BLK, AD), BF16), pltpu.VMEM((AH, 1, BLK), F32),
                        pltpu.VMEM((AH, 1, BLK), F32), pltpu.VMEM((AH, AD, BLK), F32)])
    return pl.pallas_call(
        _moba_p_kernel, grid_spec=gs,
        out_shape=jax.ShapeDtypeStruct((bsz, t, AW), F32),
        compiler_params=_cparams("arbitrary", "arbitrary"), name="moba_p")(slopes, q, kb, vt, km)


def _moba_sm_kernel(pt_ref, qbd_ref, kn_ref, vn_ref, ck_ref, cv_ref, o_ref, kbuf, vbuf, s_s, sem,
                    *, n_pages, page, nbp, past, ts, tn):
    b = pl.program_id(0)
    slot = b % 2
    ppb = n_pages // nbp
    rows = AH * ts

    def page_copies(seq, sl, pg):
        pid = pt_ref[seq * n_pages + pg]
        return (pltpu.make_async_copy(ck_ref.at[pid], kbuf.at[sl, pg], sem.at[sl, 0]),
                pltpu.make_async_copy(cv_ref.at[pid], vbuf.at[sl, pg], sem.at[sl, 1]))

    def fetch(seq, sl):
        for pg in range(n_pages):
            for cp in page_copies(seq, sl, pg):
                cp.start()

    @pl.when(b == 0)
    def _first():
        fetch(b, slot)

    @pl.when(b + 1 < pl.num_programs(0))
    def _next():
        fetch(b + 1, 1 - slot)

    for pg in range(n_pages):
        for cp in page_copies(b, slot, pg):
            cp.wait()

    qsf = qbd_ref[0] * (AD ** -0.5)
    qs = qsf.astype(BF16)
    g = []
    for jb in range(nbp):
        gsum = jnp.zeros((rows, page), F32)
        for p in range(ppb):
            pg = jb * ppb + p
            s = _mm(qs, kbuf[slot, pg].reshape(AW, page))
            s_s[pg] = s
            gsum = gsum + s
        g.append(jnp.sum(gsum, axis=1, keepdims=True))
    rh = lax.broadcasted_iota(I32, (rows, 1), 0) // ts
    rr = lax.broadcasted_iota(I32, (rows, 1), 0) % ts
    slope = jnp.zeros((rows, 1), F32)
    for hh in range(AH):
        slope = jnp.where(rh == hh, 2.0 ** (-8.0 * (hh + 1) / AH), slope)
    picks = []
    for _ in range(min(TOPB, nbp)):
        mx = g[0]
        for jb in range(1, nbp):
            mx = jnp.maximum(mx, g[jb])
        idx = jnp.full((rows, 1), float(nbp), F32)
        for jb in reversed(range(nbp)):
            idx = jnp.where(g[jb] == mx, float(jb), idx)
        picks.append(idx)
        g = [jnp.where(idx == float(jb), MASKV, g[jb]) for jb in range(nbp)]
    lane = lax.broadcasted_iota(I32, (rows, page), 1).astype(F32)
    qpos = (past + rr).astype(F32)
    dist_n = rr - lax.broadcasted_iota(I32, (rows, tn), 1)
    s_new = lax.dot_general(qsf, kn_ref[0], (((1,), (1,)), ((), ())), preferred_element_type=F32)
    s_new = jnp.where(dist_n >= 0, s_new - slope * dist_n.astype(F32), MASKV)
    m = jnp.max(s_new, axis=1, keepdims=True)
    for pg in range(n_pages):
        blk = float(pg // ppb)
        chosen = picks[0] == blk
        for c in picks[1:]:
            chosen = chosen | (c == blk)
        s = jnp.where(chosen, s_s[pg] - slope * (qpos - (lane + float(pg * page))), MASKV)
        s_s[pg] = s
        m = jnp.maximum(m, jnp.max(s, axis=1, keepdims=True))
    p_new = jnp.exp(s_new - m)
    l = jnp.sum(p_new, axis=1, keepdims=True)
    acc = jnp.dot(p_new, vn_ref[0], preferred_element_type=F32)
    for pg in range(n_pages):
        p = jnp.exp(s_s[pg] - m)
        l = l + jnp.sum(p, axis=1, keepdims=True)
        acc = acc + _mm_nt(p, vbuf[slot, pg].reshape(AW, page))
    acc = acc / l
    lh = lax.broadcasted_iota(I32, (ts, AW), 1) // AD
    out = jnp.zeros((ts, AW), F32)
    for hh in range(AH):
        out = out + jnp.where(lh == hh, acc[hh * ts:(hh + 1) * ts, :], 0.0)
    o_ref[0] = out


def _moba_sample_m(page_table, cache_kt, cache_vt, qbd, k_new, v_new, ts):
    db, n_pages = page_table.shape
    page = cache_kt.shape[3]
    past = n_pages * page
    assert past % BLK == 0 and BLK % page == 0
    nbp = past // BLK
    rows = AH * ts
    tn = k_new.shape[1]
    gs = pltpu.PrefetchScalarGridSpec(
        num_scalar_prefetch=1, grid=(db,),
        in_specs=[pl.BlockSpec((1, rows, AW), lambda b, pt: (b, 0, 0)),
                  pl.BlockSpec((1, tn, AW), lambda b, pt: (b, 0, 0)),
                  pl.BlockSpec((1, tn, AW), lambda b, pt: (b, 0, 0)),
                  pl.BlockSpec(memory_space=pl.ANY), pl.BlockSpec(memory_space=pl.ANY)],
        out_specs=pl.BlockSpec((1, ts, AW), lambda b, pt: (b, 0, 0)),
        scratch_shapes=[pltpu.VMEM((2, n_pages, AH, AD, page), F32), pltpu.VMEM((2, n_pages, AH, AD, page), F32),
                        pltpu.VMEM((n_pages, rows, page), F32), pltpu.SemaphoreType.DMA((2, 2))])
    return pl.pallas_call(
        functools.partial(_moba_sm_kernel, n_pages=n_pages, page=page, nbp=nbp, past=past, ts=ts, tn=tn),
        grid_spec=gs, out_shape=jax.ShapeDtypeStruct((db, ts, AW), F32),
        compiler_params=_cparams("arbitrary"), name="moba_s")(
            page_table.reshape(-1), qbd, k_new, v_new, cache_kt, cache_vt)


def _merge_kernel(x_ref, hm_ref, ha_ref, sh1_ref, sc1_ref, gt1_ref, sh2_ref, sc2_ref, gt2_ref,
                  g1_ref, g2_ref, gmh_ref, wl_ref, wbm_ref, wba_ref, wo_ref, wr_ref, br_ref,
                  wsg_ref, wsu_ref, wsd_ref, cin_ref,
                  base_ref, h2_ref, idx_ref, wt_ref, rank_ref, cout_ref, cnt_s):
    i = pl.program_id(0)
    tm = x_ref.shape[0]

    @pl.when(i == 0)
    def _init():
        cnt_s[...] = cin_ref[...]

    x = x_ref[...]
    h1 = (_rms(x, g1_ref[...]) * (1.0 + sc1_ref[0]) + sh1_ref[0]).astype(BF16)
    d = functools.partial(jnp.dot, preferred_element_type=F32)
    mo = d(h1, wl_ref[:, 0:MW])
    ga = d(h1, wl_ref[:, MW:MW + D])
    gb = d(h1, wl_ref[:, MW + D:MW + 2 * D])
    hm = hm_ref[...]
    parts = []
    for h in range(MH):
        xh = hm[:, h * MD:(h + 1) * MD]
        parts.append(_rms(xh, gmh_ref[:, h * MD:(h + 1) * MD]))
    hmn = jnp.concatenate(parts, axis=1) * jax.nn.sigmoid(mo)
    merged = jax.nn.sigmoid(ga) * _mm(hmn, wbm_ref[...]) + jax.nn.sigmoid(gb) * _mm(ha_ref[...], wba_ref[...])
    x1 = x + gt1_ref[0] * _mm(merged, wo_ref[...])
    h2 = _rms(x1, g2_ref[...]) * (1.0 + sc2_ref[0]) + sh2_ref[0]
    _to_row_tiles(h2_ref, h2)
    h2b = h2.astype(BF16)
    shared = _mm(_silu(d(h2b, wsg_ref[...])) * d(h2b, wsu_ref[...]), wsd_ref[...])
    base_ref[...] = x1 + gt2_ref[0] * shared
    scores = jax.nn.sigmoid(_mm_x3(h2, wr_ref[...]))
    col = lax.broadcasted_iota(I32, (tm, NE), 1).astype(F32)
    g = scores + br_ref[...]
    idxs, wts = [], []
    onehot = jnp.zeros((tm, NE), F32)
    for _ in range(TOPK):
        mx = jnp.max(g, axis=1, keepdims=True)
        idx = jnp.min(jnp.where(g == mx, col, float(NE)), axis=1, keepdims=True)
        hit = col == idx
        idxs.append(idx)
        wts.append(jnp.sum(jnp.where(hit, scores, 0.0), axis=1, keepdims=True))
        onehot = jnp.where(hit, 1.0, onehot)
        g = jnp.where(hit, MASKV, g)
    wsum = wts[0]
    for w in wts[1:]:
        wsum = wsum + w
    ri = lax.broadcasted_iota(I32, (tm, tm), 0)
    ci = lax.broadcasted_iota(I32, (tm, tm), 1)
    before = jnp.where(ci < ri, 1.0, 0.0).astype(BF16)
    pref = cnt_s[...] + d(before, onehot.astype(BF16))
    lane8 = lax.broadcasted_iota(I32, (tm, TOPK), 1)
    idx_o = jnp.zeros((tm, TOPK), F32)
    wt_o = jnp.zeros((tm, TOPK), F32)
    rk_o = jnp.zeros((tm, TOPK), F32)
    for kk in range(TOPK):
        rk = jnp.sum(jnp.where(col == idxs[kk], pref, 0.0), axis=1, keepdims=True)
        idx_o = jnp.where(lane8 == kk, idxs[kk], idx_o)
        wt_o = jnp.where(lane8 == kk, wts[kk] / wsum * ROUTE_SCALE, wt_o)
        rk_o = jnp.where(lane8 == kk, rk, rk_o)
    idx_ref[...] = idx_o.astype(I32)
    wt_ref[...] = wt_o
    rank_ref[...] = rk_o.astype(I32)
    cnt_s[...] = cnt_s[...] + jnp.sum(onehot, axis=0, keepdims=True)
    cout_ref[...] = cnt_s[...]


def _merge(x2, hm, ha, mod3, per_token, tm, tiles_per_batch, weights, cnt_in):
    n = x2.shape[0]
    row = lambda w: pl.BlockSpec((tm, w), lambda i: (i, 0))
    full = lambda a: pl.BlockSpec(a.shape, lambda i: (0,) * a.ndim)
    mods = [_mod_spec(per_token, tm, tiles_per_batch, c) for c in range(6)]
    sds = jax.ShapeDtypeStruct
    return pl.pallas_call(
        _merge_kernel, grid=(n // tm,),
        in_specs=[row(D), row(MW), row(AW)] + mods + [full(w) for w in weights] + [full(cnt_in)],
        out_specs=[row(D), pl.BlockSpec((tm * RT, LANES), lambda i: (i, 0)), row(TOPK), row(TOPK), row(TOPK),
                   full(cnt_in)],
        out_shape=[sds((n, D), F32), sds((n * RT, LANES), F32), sds((n, TOPK), I32), sds((n, TOPK), F32),
                   sds((n, TOPK), I32), sds((1, NE), F32)],
        scratch_shapes=[pltpu.VMEM((1, NE), F32)],
        compiler_params=_cparams("arbitrary"), name="merge")(
            x2, hm, ha, *([mod3] * 6), *weights, cnt_in)


def _dest_kernel(idx_ref, rank_ref, pstart_ref, o_ref):
    tm = idx_ref.shape[0]
    col = lax.broadcasted_iota(I32, (tm, NE), 1)
    lane8 = lax.broadcasted_iota(I32, (tm, TOPK), 1)
    idx = idx_ref[...]
    out = rank_ref[...]
    for kk in range(TOPK):
        start = jnp.sum(jnp.where(col == idx[:, kk:kk + 1], pstart_ref[...], 0.0), axis=1, keepdims=True)
        out = out + jnp.where(lane8 == kk, start.astype(I32), 0)
    o_ref[...] = out


def _dest(idx, rank, pstart):
    n = idx.shape[0]
    tm = _pick_tile(n, 512)
    row = pl.BlockSpec((tm, TOPK), lambda i: (i, 0))
    return pl.pallas_call(
        _dest_kernel, grid=(n // tm,),
        in_specs=[row, row, pl.BlockSpec((1, NE), lambda i: (0, 0))],
        out_specs=row, out_shape=jax.ShapeDtypeStruct((n, TOPK), I32),
        compiler_params=_cparams("parallel"), name="dest")(idx, rank, pstart.astype(F32).reshape(1, NE))


def _row_copy(src, dst, sem):
    return pltpu.make_async_copy(src, dst, sem)


def _dispatch_kernel(pend_ref, cnt_ref, dest_ref, h_ref, xs_ref, zbuf, ring, zsem, sem):
    step = pl.program_id(0)
    last = pl.num_programs(0) - 1
    td = h_ref.shape[0] // RT
    slot = step % 2

    @pl.when(step == 0)
    def _zero_tails():
        zbuf[...] = jnp.zeros(zbuf.shape, F32)

        def tail(e):
            first = pl.multiple_of((pend_ref[e] - GROUP) * RT, RT)
            return _row_copy(zbuf, xs_ref.at[pl.ds(first, GROUP * RT)], zsem)

        def start(e, c):
            @pl.when(cnt_ref[e] > 0)
            def _():
                tail(e).start()
            return c

        def wait(e, c):
            @pl.when(cnt_ref[e] > 0)
            def _():
                tail(e).wait()
            return c

        lax.fori_loop(0, NE, start, 0)
        lax.fori_loop(0, NE, wait, 0)

    ring[slot] = h_ref[...]

    def start(t, c):
        src = ring.at[slot, pl.ds(pl.multiple_of(t * RT, RT), RT)]
        for kk in range(TOPK):
            dst = pl.multiple_of(dest_ref[0, 0, t * TOPK + kk] * RT, RT)
            _row_copy(src, xs_ref.at[pl.ds(dst, RT)], sem.at[slot]).start(priority=kk % 2)
        return c

    def drain(s):
        def wait(t, c):
            for kk in range(TOPK):
                _row_copy(ring.at[s, pl.ds(0, RT)], xs_ref.at[pl.ds(0, RT)], sem.at[s]).wait()
            return c
        lax.fori_loop(0, td, wait, 0)

    lax.fori_loop(0, td, start, 0)

    @pl.when(step > 0)
    def _prev():
        drain(1 - slot)

    @pl.when(step == last)
    def _own():
        drain(slot)


def _dispatch(pend, cnt, dest, h2t, n_rows, td):
    n = h2t.shape[0] // RT
    gs = pltpu.PrefetchScalarGridSpec(
        num_scalar_prefetch=2, grid=(n // td,),
        in_specs=[pl.BlockSpec((1, 1, td * TOPK), lambda i, *_: (i, 0, 0), memory_space=pltpu.SMEM),
                  pl.BlockSpec((td * RT, LANES), lambda i, *_: (i, 0))],
        out_specs=pl.BlockSpec(memory_space=pl.ANY),
        scratch_shapes=[pltpu.VMEM((GROUP * RT, LANES), F32), pltpu.VMEM((2, td * RT, LANES), F32),
                        pltpu.SemaphoreType.DMA(()), pltpu.SemaphoreType.DMA((2,))])
    return pl.pallas_call(
        _dispatch_kernel, grid_spec=gs, out_shape=jax.ShapeDtypeStruct((n_rows * RT, LANES), F32),
        compiler_params=_cparams("arbitrary"), name="dispatch")(
            pend, cnt, dest.reshape(n // td, 1, td * TOPK), h2t)


def _experts_kernel(be_ref, nx_ref, sl_ref, nv_ref, nu_ref, x_ref, wg_hbm, wu_hbm, wd_hbm, o_ref,
                    wg_f, wu_f, wd_f, wg_s, wu_s, wd_s, x_s, sem):
    b = pl.program_id(0)

    def fetch(e, slot):
        return (pltpu.make_async_copy(wg_hbm.at[e], wg_f.at[slot], sem.at[slot, 0]),
                pltpu.make_async_copy(wu_hbm.at[e], wu_f.at[slot], sem.at[slot, 1]),
                pltpu.make_async_copy(wd_hbm.at[e], wd_f.at[slot], sem.at[slot, 2]))

    @pl.when(b < nu_ref[0])
    def _run():
        slot = sl_ref[b]

        @pl.when((b == 0) | (be_ref[b] != be_ref[jnp.maximum(b - 1, 0)]))
        def _load():
            @pl.when(b == 0)
            def _first():
                for cp in fetch(be_ref[b], slot):
                    cp.start()

            for cp in fetch(be_ref[b], slot):
                cp.wait()

            @pl.when(nx_ref[b] < NE)
            def _next():
                for cp in fetch(nx_ref[b], 1 - slot):
                    cp.start()

            wg_s[...] = wg_f[slot].astype(BF16)
            wu_s[...] = wu_f[slot].astype(BF16)
            wd_s[...] = wd_f[slot].astype(BF16)

        d = functools.partial(jnp.dot, preferred_element_type=F32)
        nv = nv_ref[b]
        for m in range(ROWSTEP, GROUP + 1, ROWSTEP):
            @pl.when((nv > m - ROWSTEP) & (nv <= m))
            def _rows(m=m):
                for s in range(RT):
                    x_s[0:m, s * LANES:(s + 1) * LANES] = x_ref[pl.ds(s, m, stride=RT), :].astype(BF16)
                x = x_s[0:m, :]
                hb = _silu(d(x, wg_s[...])) * d(x, wu_s[...])
                _to_row_tiles(o_ref, d(hb.astype(BF16), wd_s[...]))


def _experts(blk_e, blk_next, blk_slot, blk_rows, n_used, xs, w_gate, w_up, w_down):
    n_rows = xs.shape[0] // RT
    rows = lambda b, be, nx, sl, nv, nu: (jnp.minimum(b, nu[0] - 1), 0)
    hbm = pl.BlockSpec(memory_space=pl.ANY)
    gs = pltpu.PrefetchScalarGridSpec(
        num_scalar_prefetch=5, grid=(n_rows // GROUP,),
        in_specs=[pl.BlockSpec((GROUP * RT, LANES), rows), hbm, hbm, hbm],
        out_specs=pl.BlockSpec((GROUP * RT, LANES), rows),
        scratch_shapes=[pltpu.VMEM((2, D, DE), F32), pltpu.VMEM((2, D, DE), F32), pltpu.VMEM((2, DE, D), F32),
                        pltpu.VMEM((D, DE), BF16), pltpu.VMEM((D, DE), BF16), pltpu.VMEM((DE, D), BF16),
                        pltpu.VMEM((GROUP, D), BF16), pltpu.SemaphoreType.DMA((2, 3))])
    return pl.pallas_call(
        _experts_kernel, grid_spec=gs, out_shape=jax.ShapeDtypeStruct((n_rows * RT, LANES), F32),
        compiler_params=_cparams("arbitrary"), name="experts")(
            blk_e, blk_next, blk_slot, blk_rows, n_used, xs, w_gate, w_up, w_down)


def _combine_kernel(dest_ref, dnext_ref, ys_ref, wt_ref, base_ref, gt2_ref, o_ref, buf, sem):
    step = pl.program_id(0)
    last = pl.num_programs(0) - 1
    tc = base_ref.shape[0]
    slot = step % 2

    def issue(dref, s):
        def start(t, c):
            dst = pl.multiple_of(t * RT, RT)
            for kk in range(TOPK):
                src = pl.multiple_of(dref[0, 0, t * TOPK + kk] * RT, RT)
                _row_copy(ys_ref.at[pl.ds(src, RT)], buf.at[s, kk, pl.ds(dst, RT)], sem.at[s]).start(priority=kk % 2)
            return c
        lax.fori_loop(0, tc, start, 0)

    @pl.when(step == 0)
    def _first():
        issue(dest_ref, slot)

    @pl.when(step < last)
    def _next():
        issue(dnext_ref, 1 - slot)

    def wait(t, c):
        for kk in range(TOPK):
            _row_copy(ys_ref.at[pl.ds(0, RT)], buf.at[slot, kk, pl.ds(0, RT)], sem.at[slot]).wait()
        return c

    lax.fori_loop(0, tc, wait, 0)
    wt = wt_ref[...]
    acc = wt[:, 0:1] * _from_row_tiles(buf, tc, (slot, 0))
    for kk in range(1, TOPK):
        acc = acc + wt[:, kk:kk + 1] * _from_row_tiles(buf, tc, (slot, kk))
    o_ref[...] = base_ref[...] + gt2_ref[0] * acc


def _combine(dest, ys, wt, base, mod3, per_token, tc, tiles_per_batch):
    n = base.shape[0]
    nt = n // tc
    return pl.pallas_call(
        _combine_kernel, grid=(nt,),
        in_specs=[pl.BlockSpec((1, 1, tc * TOPK), lambda i: (i, 0, 0), memory_space=pltpu.SMEM),
                  pl.BlockSpec((1, 1, tc * TOPK), lambda i: (jnp.minimum(i + 1, nt - 1), 0, 0),
                               memory_space=pltpu.SMEM),
                  pl.BlockSpec(memory_space=pl.ANY),
                  pl.BlockSpec((tc, TOPK), lambda i: (i, 0)),
                  pl.BlockSpec((tc, D), lambda i: (i, 0)),
                  _mod_spec(per_token, tc, tiles_per_batch, 5)],
        out_specs=pl.BlockSpec((tc, D), lambda i: (i, 0)),
        out_shape=jax.ShapeDtypeStruct((n, D), F32),
        scratch_shapes=[pltpu.VMEM((2, TOPK, tc * RT, LANES), F32), pltpu.SemaphoreType.DMA((2,))],
        compiler_params=_cparams("arbitrary"), name="combine")(
            dest.reshape(nt, 1, tc * TOPK), dest.reshape(nt, 1, tc * TOPK), ys, wt, base, mod3)


def _pick_tile(n, pref):
    t = pref
    while n % t:
        t //= 2
    return t


def kernel(x_prompt, x_sample, cache_k, cache_v, page_table, state_mlstm_C, state_mlstm_n, state_mlstm_m,
           c_prompt, c_sample, w_ada, b_ada, g_norm1, w_in, b_gates, g_q, g_k, g_mh, w_br_m, w_br_a, w_out,
           g_norm2, w_router, b_router, w_gate, w_up, w_down, ws_gate, ws_up, ws_down):
    depth = w_ada.shape[0]
    assert depth == 1
    bsz, t, _ = x_prompt.shape
    db, ts, _ = x_sample.shape
    n_p, n_s = bsz * t, db * ts
    assert t % BLK == 0 and t % MCHUNK == 0 and ts <= 8
    l = 0

    nc = bsz + db
    ncp = -(-nc // 8) * 8
    c_all = jnp.pad(jnp.concatenate([c_prompt, c_sample], axis=0), ((0, ncp - nc), (0, 0)))
    mod = _ada(c_all, w_ada[l], b_ada[l])
    mod_p = mod[:bsz].reshape(bsz, 1, 6 * D)
    tm_s = _pick_tile(n_s, 256)
    mod_s = jnp.repeat(mod[bsz:nc], ts, axis=0).reshape(n_s // tm_s, tm_s, 6 * D)

    wi = w_in[l]
    sl = lambda name: wi[:, _OFF[name][0]:_OFF[name][1]]
    wm = jnp.concatenate([sl("mq"), sl("mk"), sl("mv"), sl("aq"), sl("ak"), sl("av")], axis=1).astype(BF16)
    wg = jnp.pad(jnp.concatenate([sl("mi"), sl("mf")], axis=1), ((0, 0), (0, 128 - 2 * MH))).astype(BF16)
    wl = jnp.concatenate([sl("mo"), sl("ga"), sl("gb")], axis=1).astype(BF16)
    g1 = g_norm1[l].reshape(1, D)
    g2 = g_norm2[l].reshape(1, D)
    gq = jnp.tile(g_q[l], AH).reshape(1, AW)
    gk = jnp.tile(g_k[l], AH).reshape(1, AW)
    gmh = g_mh[l].reshape(1, MW)
    grp = jnp.arange(AW) // AD
    bd = jnp.where(grp[:, None] == grp[None, :], 1.0 / AD, 0.0).astype(BF16)
    merge_w = (g1, g2, gmh, wl, w_br_m[l].astype(BF16), w_br_a[l].astype(BF16), w_out[l].astype(BF16),
               w_router[l], b_router[l].reshape(1, NE), ws_gate[l].astype(BF16), ws_up[l].astype(BF16),
               ws_down[l].astype(BF16))
    slopes = 2.0 ** (-8.0 * jnp.arange(1, AH + 1, dtype=F32) / AH)

    tm_p = _pick_tile(t, 512)
    xp2 = x_prompt.reshape(n_p, D)
    xs2 = x_sample.reshape(n_s, D)
    mq_p, mk_p, mv_p, gt_p, qa_p, kat_p, vat_p, kb_p, vt_p, km = _inproj(
        xp2, mod_p, False, tm_p, t // tm_p, g1, wm, wg, gq, gk, bd, True)
    mq_s, mk_s, mv_s, gt_s, qa_s, ka_s, va_s = _inproj(
        xs2, mod_s, True, tm_s, 1, g1, wm, wg, gq, gk, bd, False)

    zeros = functools.partial(jnp.zeros, dtype=F32)
    gp = gt_p.reshape(bsz, t, 2 * MH)
    hm_p, c_p, nn_p, m_p = _mlstm(
        mq_p.reshape(bsz, t, MW), mk_p.reshape(bsz, t, MW), mv_p.reshape(bsz, t, MW),
        gp.transpose(0, 2, 1), gp, b_gates[l], zeros((bsz, MH, MD, MD)), zeros((bsz, MH, MD)),
        zeros((bsz, MH, MD)), _pick_tile(bsz, 2), t, MCHUNK)
    ls = 16
    assert ts <= ls
    pad_t = lambda a: jnp.pad(a.reshape(db, ts, -1), ((0, 0), (0, ls - ts), (0, 0)))
    gs_ = pad_t(gt_s)
    m0 = jnp.broadcast_to(state_mlstm_m[l].astype(F32)[:, :, None], (db, MH, MD))
    hm_s, c_sm, nn_s, m_sm = _mlstm(
        pad_t(mq_s), pad_t(mk_s), pad_t(mv_s), gs_.transpose(0, 2, 1), gs_, b_gates[l],
        state_mlstm_C[l].astype(F32), state_mlstm_n[l].astype(F32), m0, _pick_tile(db, 8), ts, ls)
    hm_s = hm_s[:, :ts].reshape(n_s, MW)

    nb = t // BLK
    ha_p = _moba_prompt(qa_p.reshape(bsz, t, AW), kb_p.reshape(bsz, nb, BLK, AW), vt_p.reshape(bsz, nb, AW, BLK),
                        km.reshape(bsz, nb, AW), slopes).reshape(n_p, AW)
    qh = qa_s.astype(F32).reshape(db, ts, AH, AD).transpose(0, 2, 1, 3)
    qbd = (qh[:, :, :, None, :] * jnp.eye(AH, dtype=F32)[None, :, None, :, None]).reshape(db, AH * ts, AW)
    tn = 8
    pad_n = lambda a: jnp.pad(a.reshape(db, ts, AW), ((0, 0), (0, tn - ts), (0, 0)))
    to_t = lambda c: jnp.transpose(c[l], (0, 2, 3, 1))
    ha_s = _moba_sample_m(page_table, to_t(cache_k), to_t(cache_v), qbd, pad_n(ka_s), pad_n(va_s), ts)
    ha_s = ha_s.reshape(n_s, AW)

    tmm_p = _pick_tile(t, 512)
    base_p, h2_p, idx_p, wt_p, rk_p, cnt1 = _merge(
        xp2, hm_p.reshape(n_p, MW), ha_p, mod_p, False, tmm_p, t // tmm_p, merge_w, zeros((1, NE)))
    base_s, h2_s, idx_s, wt_s, rk_s, cnt2 = _merge(
        xs2, hm_s, ha_s, mod_s, True, tm_s, 1, merge_w, cnt1)

    n_all = n_p + n_s
    cnt = cnt2.reshape(NE).astype(I32)
    padded = (cnt + GROUP - 1) // GROUP * GROUP
    pend = jnp.cumsum(padded)
    pstart = pend - padded
    n_blocks = -(-(n_all * TOPK + NE * (GROUP - 1)) // GROUP)
    n_used = (pend[-1] // GROUP).reshape(1)
    blk = jnp.minimum(jnp.arange(n_blocks, dtype=I32), n_used[0] - 1) * GROUP
    blk_e = jnp.minimum(jnp.sum((pend[None, :] <= blk[:, None]).astype(I32), axis=1), NE - 1)
    ids = jnp.arange(NE, dtype=I32)
    later = (ids[None, :] > ids[:, None]) & (cnt[None, :] > 0)
    next_e = jnp.min(jnp.where(later, ids[None, :], NE), axis=1)
    slot_e = (jnp.cumsum((cnt > 0).astype(I32)) - 1) % 2
    blk_next = next_e[blk_e].astype(I32)
    blk_slot = slot_e[blk_e].astype(I32)
    blk_rows = jnp.clip(cnt[blk_e] - (blk - pstart[blk_e]), 0, GROUP).astype(I32)

    h2 = jnp.concatenate([h2_p, h2_s], axis=0)
    dest_p = _dest(idx_p, rk_p, pstart)
    dest_s = _dest(idx_s, rk_s, pstart)
    td = _pick_tile(n_all, 128)
    xs = _dispatch(pend, cnt, jnp.concatenate([dest_p, dest_s], axis=0), h2, n_blocks * GROUP, td)
    ys = _experts(blk_e, blk_next, blk_slot, blk_rows, n_used, xs, w_gate[l], w_up[l], w_down[l])
    tc_p = _pick_tile(t, 64)
    y_p = _combine(dest_p, ys, wt_p, base_p, mod_p, False, tc_p, t // tc_p)
    tc_s = _pick_tile(tm_s, 64)
    mod_sc = mod_s.reshape(n_s // tc_s, tc_s, 6 * D)
    y_s = _combine(dest_s, ys, wt_s, base_s, mod_sc, True, tc_s, 1)

    st = lambda a: a[None]
    return (y_p.reshape(bsz, t, D), y_s.reshape(db, ts, D),
            st(kat_p.reshape(bsz, AH, AD, t).transpose(0, 3, 1, 2)),
            st(vat_p.reshape(bsz, AH, AD, t).transpose(0, 3, 1, 2)),
            st(c_p), st(nn_p), st(m_p[:, :, 0]),
            st(ka_s.reshape(db, ts, AH, AD)), st(va_s.reshape(db, ts, AH, AD)),
            st(c_sm), st(nn_s), st(m_sm[:, :, 0]))
```

```python
import functools

import jax
import jax.numpy as jnp
from jax import lax
from jax.experimental import pallas as pl
from jax.experimental.pallas import tpu as pltpu

F32, BF16, I32 = jnp.float32, jnp.bfloat16, jnp.int32

D = 1024
MH, MD = 4, 128
AH, AD = 8, 64
MW, AW = MH * MD, AH * AD
MCHUNK = 128
BLK = 256
TOPB = 3
NE, TOPK, DE, DSH = 256, 8, 256, 256
ROUTE_SCALE = 2.5
GROUP = 256
EPS = 1e-6
NEG = -1e30
MASKV = -3.0e38
LOG2E = 1.4426950408889634
VMEM_LIMIT = 56 * 1024 * 1024
_OFF = {}
_o = 0
for _n, _w in (("mq", MW), ("mk", MW), ("mv", MW), ("mo", MW), ("mi", MH), ("mf", MH),
               ("aq", AW), ("ak", AW), ("av", AW), ("ga", D), ("gb", D)):
    _OFF[_n] = (_o, _o + _w)
    _o += _w


def _cparams(*sem):
    return pltpu.CompilerParams(dimension_semantics=sem, vmem_limit_bytes=VMEM_LIMIT)


def _mm(a, b):
    return jnp.dot(a.astype(BF16), b.astype(BF16), preferred_element_type=F32)


def _mm_nt(a, b):
    return lax.dot_general(a.astype(BF16), b.astype(BF16), (((1,), (1,)), ((), ())),
                           preferred_element_type=F32)


def _mm_tn(a, b):
    return lax.dot_general(a.astype(BF16), b.astype(BF16), (((0,), (0,)), ((), ())),
                           preferred_element_type=F32)


def _split2(x):
    hi = x.astype(BF16)
    return hi, (x - hi.astype(F32)).astype(BF16)


def _mm_x3(a, b):
    ah, al = _split2(a)
    bh, bl = _split2(b)
    d = functools.partial(jnp.dot, preferred_element_type=F32)
    return d(ah, bh) + d(al, bh) + d(ah, bl)


def _rms(x, g):
    return x * lax.rsqrt(jnp.mean(x * x, axis=-1, keepdims=True) + EPS) * g


def _silu(x):
    return x * jax.nn.sigmoid(x)


def _logsig(x):
    return jnp.minimum(x, 0.0) - jnp.log1p(jnp.exp(-jnp.abs(x)))


LANES = 128
RT = D // LANES


def _to_row_tiles(ref, x, idx=()):
    rows = x.shape[0]
    for s in range(RT):
        ref[idx + (pl.ds(s, rows, stride=RT), slice(None))] = x[:, s * LANES:(s + 1) * LANES]


def _from_row_tiles(ref, rows, idx=()):
    return jnp.concatenate([ref[idx + (pl.ds(s, rows, stride=RT), slice(None))] for s in range(RT)], axis=1)


def _ada_kernel(c_ref, w_ref, b_ref, o_ref):
    o_ref[...] = _mm_x3(_silu(c_ref[...]), w_ref[...]) + b_ref[...]


def _ada(c_all, w_ada, b_ada):
    r = c_all.shape[0]
    return pl.pallas_call(
        _ada_kernel, grid=(6,),
        in_specs=[pl.BlockSpec((r, D), lambda j: (0, 0)),
                  pl.BlockSpec((D, D), lambda j: (0, j)),
                  pl.BlockSpec((1, D), lambda j: (0, j))],
        out_specs=pl.BlockSpec((r, D), lambda j: (0, j)),
        out_shape=jax.ShapeDtypeStruct((r, 6 * D), F32),
        compiler_params=_cparams("parallel"), name="ada")(c_all, w_ada, b_ada.reshape(1, 6 * D))


def _mod_spec(per_token, tm, tiles_per_batch, chunk):
    if per_token:
        return pl.BlockSpec((1, tm, D), lambda i, *_: (i, 0, chunk))
    return pl.BlockSpec((1, 1, D), lambda i, *_: (i // tiles_per_batch, 0, chunk))


def _group_ms(x, bd):
    hi, lo = _split2(x * x)
    d = functools.partial(jnp.dot, preferred_element_type=F32)
    return d(hi, bd) + d(lo, bd)


def _inproj_kernel(x_ref, sh_ref, sc_ref, g1_ref, wm_ref, wg_ref, gq_ref, gk_ref, bd_ref,
                   mq_ref, mk_ref, mv_ref, gt_ref, qa_ref, *kv_refs):
    h = _rms(x_ref[...], g1_ref[...]) * (1.0 + sc_ref[0]) + sh_ref[0]
    hb = h.astype(BF16)
    d = functools.partial(jnp.dot, preferred_element_type=F32)
    mq_ref[...] = d(hb, wm_ref[:, 0:MW]).astype(BF16)
    mk_ref[...] = (d(hb, wm_ref[:, MW:2 * MW]) * (MD ** -0.5)).astype(BF16)
    mv_ref[...] = d(hb, wm_ref[:, 2 * MW:3 * MW]).astype(BF16)
    o = 3 * MW
    aq = d(hb, wm_ref[:, o:o + AW])
    ak = d(hb, wm_ref[:, o + AW:o + 2 * AW])
    av = d(hb, wm_ref[:, o + 2 * AW:o + 3 * AW])
    bd = bd_ref[...]
    qa = aq * lax.rsqrt(_group_ms(aq, bd) + EPS) * gq_ref[...]
    ka = ak * lax.rsqrt(_group_ms(ak, bd) + EPS) * gk_ref[...]
    qa_ref[...] = qa.astype(BF16)
    gt_ref[...] = d(hb, wg_ref[...])[:, 0:2 * MH]
    if len(kv_refs) == 2:
        ka_ref, va_ref = kv_refs
        ka_ref[...] = ka
        va_ref[...] = av
    else:
        kat_ref, vat_ref, kb_ref, vt_ref, km_ref = kv_refs
        for r in range(vt_ref.shape[0]):
            blk = slice(r * BLK, (r + 1) * BLK)
            kat_ref[0, :, blk] = ka[blk, :].T
            vbt = av[blk, :].T
            vat_ref[0, :, blk] = vbt
            kb_ref[r] = ka[blk, :].astype(BF16)
            vt_ref[r] = vbt.astype(BF16)
            km_ref[r] = jnp.mean(ka[blk, :], axis=0, keepdims=True)


def _inproj(x2, mod3, per_token, tm, tiles_per_batch, g1, wm, wg, gq, gk, bd, attn_layouts):
    n = x2.shape[0]
    row = lambda w: pl.BlockSpec((tm, w), lambda i: (i, 0))
    full = lambda a: pl.BlockSpec(a.shape, lambda i: (0,) * a.ndim)
    sds = jax.ShapeDtypeStruct
    out_specs = [row(MW), row(MW), row(MW), row(2 * MH), row(AW)]
    out_shape = [sds((n, MW), BF16), sds((n, MW), BF16), sds((n, MW), BF16), sds((n, 2 * MH), F32),
                 sds((n, AW), BF16)]
    if attn_layouts:
        tpb = tiles_per_batch
        pos_minor = pl.BlockSpec((1, AW, tm), lambda i: (i // tpb, 0, i % tpb))
        out_specs += [pos_minor, pos_minor,
                      pl.BlockSpec((tm // BLK, BLK, AW), lambda i: (i, 0, 0)),
                      pl.BlockSpec((tm // BLK, AW, BLK), lambda i: (i, 0, 0)),
                      pl.BlockSpec((tm // BLK, 1, AW), lambda i: (i, 0, 0))]
        out_shape += [sds((n // (tpb * tm), AW, tpb * tm), F32), sds((n // (tpb * tm), AW, tpb * tm), F32),
                      sds((n // BLK, BLK, AW), BF16), sds((n // BLK, AW, BLK), BF16),
                      sds((n // BLK, 1, AW), F32)]
    else:
        out_specs += [row(AW), row(AW)]
        out_shape += [sds((n, AW), F32), sds((n, AW), F32)]
    return pl.pallas_call(
        _inproj_kernel, grid=(n // tm,),
        in_specs=[row(D), _mod_spec(per_token, tm, tiles_per_batch, 0),
                  _mod_spec(per_token, tm, tiles_per_batch, 1),
                  full(g1), full(wm), full(wg), full(gq), full(gk), full(bd)],
        out_specs=out_specs, out_shape=out_shape,
        compiler_params=_cparams("parallel"), name="inproj")(x2, mod3, mod3, g1, wm, wg, gq, gk, bd)


def _mlstm_kernel(q_ref, k_ref, v_ref, gr_ref, gc_ref, bgc_ref, bgr_ref, c0_ref, n0_ref, m0_ref,
                  h_ref, cn_ref, nn_ref, mn_ref, c_s, n_s, m_s, *, bb, t_valid, L):
    c = pl.program_id(1)

    @pl.when(c == 0)
    def _init():
        c_s[...] = c0_ref[...]
        n_s[...] = n0_ref[...]
        m_s[...] = m0_ref[...]

    rowi = lax.broadcasted_iota(I32, (L, L), 0)
    coli = lax.broadcasted_iota(I32, (L, L), 1)
    tri = rowi >= coli
    ok_r = (c * L + lax.broadcasted_iota(I32, (1, L), 1)) < t_valid
    ok_c = (c * L + lax.broadcasted_iota(I32, (L, 1), 0)) < t_valid
    for b in range(bb):
        g_r = gr_ref[b] + bgc_ref[...]
        g_c = gc_ref[b] + bgr_ref[...]
        for h in range(MH):
            li_r = jnp.where(ok_r, g_r[h:h + 1, :], NEG)
            lf_r = jnp.where(ok_r, _logsig(g_r[MH + h:MH + h + 1, :]), 0.0)
            li_c = jnp.where(ok_c, g_c[:, h:h + 1], NEG)
            lf_c = jnp.where(ok_c, _logsig(g_c[:, MH + h:MH + h + 1]), 0.0)
            b_c = jnp.sum(jnp.where(tri, lf_r, 0.0), axis=1, keepdims=True)
            b_r = jnp.sum(jnp.where(rowi <= coli, lf_c, 0.0), axis=0, keepdims=True)
            q = q_ref[b, :, h * MD:(h + 1) * MD]
            k = k_ref[b, :, h * MD:(h + 1) * MD]
            v = v_ref[b, :, h * MD:(h + 1) * MD]
            cm = c_s[b, h]
            nv = n_s[b, h:h + 1, :]
            m_prev = m_s[b, h:h + 1, 0:1]
            inter = m_prev + b_c
            intra = jnp.where(tri, li_r + b_c - b_r, NEG)
            m_t = jnp.maximum(inter, jnp.max(intra, axis=1, keepdims=True))
            w_inter = jnp.exp(inter - m_t)
            s = _mm_nt(q, k) * jnp.exp(intra - m_t)
            num = w_inter * _mm_nt(q, cm) + _mm(s, v)
            qn = jnp.sum(q.astype(F32) * nv, axis=1, keepdims=True)
            den = w_inter * qn + jnp.sum(s, axis=1, keepdims=True)
            h_ref[b, :, h * MD:(h + 1) * MD] = num / jnp.maximum(jnp.abs(den), jnp.exp(-m_t))
            m_last = m_t[L - 1:L, :]
            b_last = b_c[L - 1:L, :]
            w_c = jnp.exp(m_prev + b_last - m_last)
            w_s = jnp.exp(li_c + b_last - b_c - m_last)
            c_s[b, h] = w_c * cm + _mm_tn(v.astype(F32) * w_s, k)
            n_s[b, h:h + 1, :] = w_c * nv + jnp.sum(k.astype(F32) * w_s, axis=0, keepdims=True)
            m_s[b, h:h + 1, :] = jnp.broadcast_to(m_last, (1, MD))

    @pl.when(c == pl.num_programs(1) - 1)
    def _fin():
        cn_ref[...] = c_s[...]
        nn_ref[...] = n_s[...]
        mn_ref[...] = m_s[...]


def _mlstm(q, k, v, g_row, g_col, b_gates, c0, n0, m0, bb, t_valid, L):
    bsz, t, _ = q.shape
    seq = pl.BlockSpec((bb, L, MW), lambda g, c: (g, c, 0))
    st4 = pl.BlockSpec((bb, MH, MD, MD), lambda g, c: (g, 0, 0, 0))
    st3 = pl.BlockSpec((bb, MH, MD), lambda g, c: (g, 0, 0))
    sds = jax.ShapeDtypeStruct
    return pl.pallas_call(
        functools.partial(_mlstm_kernel, bb=bb, t_valid=t_valid, L=L),
        grid=(bsz // bb, t // L),
        in_specs=[seq, seq, seq,
                  pl.BlockSpec((bb, 2 * MH, L), lambda g, c: (g, 0, c)),
                  pl.BlockSpec((bb, L, 2 * MH), lambda g, c: (g, c, 0)),
                  pl.BlockSpec((2 * MH, 1), lambda g, c: (0, 0)),
                  pl.BlockSpec((1, 2 * MH), lambda g, c: (0, 0)),
                  st4, st3, st3],
        out_specs=[seq, st4, st3, st3],
        out_shape=[sds((bsz, t, MW), F32), sds((bsz, MH, MD, MD), F32),
                   sds((bsz, MH, MD), F32), sds((bsz, MH, MD), F32)],
        scratch_shapes=[pltpu.VMEM((bb, MH, MD, MD), F32), pltpu.VMEM((bb, MH, MD), F32),
                        pltpu.VMEM((bb, MH, MD), F32)],
        compiler_params=_cparams("parallel", "arbitrary"), name="mlstm")(
            q, k, v, g_row, g_col, b_gates.reshape(2 * MH, 1), b_gates.reshape(1, 2 * MH), c0, n0, m0)


def _select_blocks_t(gate_t, own, n_sel):
    nb = gate_t.shape[0]
    row = lax.broadcasted_iota(I32, gate_t.shape, 0).astype(F32)
    g = jnp.where(row < own, gate_t, NEG)
    sel = jnp.zeros(gate_t.shape, F32)
    for _ in range(n_sel):
        mx = jnp.max(g, axis=0, keepdims=True)
        idx = jnp.min(jnp.where(g == mx, row, float(nb)), axis=0, keepdims=True)
        hit = row == idx
        sel = jnp.where(hit & (idx < own), 1.0, sel)
        g = jnp.where(hit, MASKV, g)
    return sel


def _moba_p_kernel(slope_ref, q_ref, k_ref, vt_ref, km_ref, o_ref, a_s, s_s, s2_s, bias_s, qs_s, m_s, l_s, acc_s):
    b = pl.program_id(0)
    i = pl.program_id(1)
    nb = km_ref.shape[1]
    rel_t = lax.broadcasted_iota(I32, (BLK, BLK), 1) - lax.broadcasted_iota(I32, (BLK, BLK), 0)

    @pl.when((b == 0) & (i == 0))
    def _alibi():
        relf = rel_t.astype(F32)
        for h in range(AH):
            a_s[h] = (slope_ref[h] * LOG2E) * relf

    causal = rel_t >= 0
    own = i.astype(F32)
    for h in range(AH):
        hs = slice(h * AD, (h + 1) * AD)
        q = q_ref[0, :, hs]
        kmh, kml = _split2(km_ref[0, :, hs])
        gate_t = _mm_nt(kmh, q) + _mm_nt(kml, q)
        sel = _select_blocks_t(gate_t, own, min(TOPB, nb))
        bias_s[h] = jnp.where(sel > 0.5, 0.0, MASKV)
        qs = (q.astype(F32) * (AD ** -0.5 * LOG2E)).astype(BF16)
        qs_s[h] = qs
        s_s[h] = jnp.where(causal, _mm_nt(k_ref[0, i, :, hs], qs) - a_s[h], MASKV)
    for h in range(AH):
        hs = slice(h * AD, (h + 1) * AD)
        m = jnp.max(s_s[h], axis=0, keepdims=True)
        p = jnp.exp2(s_s[h] - m)
        m_s[h] = m
        l_s[h] = jnp.sum(p, axis=0, keepdims=True)
        acc_s[h] = _mm(vt_ref[0, i, hs, :], p)

    def scores(j, buf):
        for h in range(AH):
            hs = slice(h * AD, (h + 1) * AD)
            buf[h] = _mm_nt(k_ref[0, j, :, hs], qs_s[h]) - a_s[h]

    def update(j, buf):
        dj = ((i - j) * BLK).astype(F32)
        for h in range(AH):
            hs = slice(h * AD, (h + 1) * AD)
            rowterm = bias_s[h, pl.ds(j, 1), :] - (slope_ref[h] * LOG2E) * dj
            m = m_s[h]
            m_new = jnp.maximum(m, jnp.max(buf[h], axis=0, keepdims=True) + rowterm)
            p = jnp.exp2(buf[h] - (m_new - rowterm))
            alpha = jnp.exp2(m - m_new)
            l_s[h] = alpha * l_s[h] + jnp.sum(p, axis=0, keepdims=True)
            acc_s[h] = alpha * acc_s[h] + _mm(vt_ref[0, j, hs, :], p)
            m_s[h] = m_new

    def pair(jj, c):
        scores(2 * jj, s_s)
        scores(2 * jj + 1, s2_s)
        update(2 * jj, s_s)
        update(2 * jj + 1, s2_s)
        return c

    lax.fori_loop(0, i // 2, pair, 0)

    @pl.when(i % 2 == 1)
    def _odd():
        scores(i - 1, s_s)
        update(i - 1, s_s)

    out_t = jnp.concatenate([acc_s[h] / l_s[h] for h in range(AH)], axis=0)
    o_ref[0] = out_t.T


def _moba_prompt(q, kb, vt, km, slopes):
    bsz, t, _ = q.shape
    nb = t // BLK
    gs = pltpu.PrefetchScalarGridSpec(
        num_scalar_prefetch=1, grid=(bsz, nb),
        in_specs=[pl.BlockSpec((1, BLK, AW), lambda b, i, s: (b, i, 0)),
                  pl.BlockSpec((1, nb, BLK, AW), lambda b, i, s: (b, 0, 0, 0)),
                  pl.BlockSpec((1, nb, AW, BLK), lambda b, i, s: (b, 0, 0, 0)),
                  pl.BlockSpec((1, nb, AW), lambda b, i, s: (b, 0, 0))],
        out_specs=pl.BlockSpec((1, BLK, AW), lambda b, i, s: (b, i, 0)),
        scratch_shapes=[pltpu.VMEM((AH, BLK, BLK), F32), pltpu.VMEM((AH, BLK, BLK), F32),
                        pltpu.VMEM((AH, BLK, BLK), F32), pltpu.VMEM((AH, nb, BLK), F32),
                        pltpu.VMEM((AH, BLK, AD), BF16), pltpu.VMEM((AH, 1, BLK), F32),
                        pltpu.VMEM((AH, 1, BLK), F32), pltpu.VMEM((AH, AD, BLK), F32)])
    return pl.pallas_call(
        _moba_p_kernel, grid_spec=gs,
        out_shape=jax.ShapeDtypeStruct((bsz, t, AW), F32),
        compiler_params=_cparams("arbitrary", "arbitrary"), name="moba_p")(slopes, q, kb, vt, km)


def _moba_sm_kernel(pt_ref, qbd_ref, kn_ref, vn_ref, ck_ref, cv_ref, o_ref, kbuf, vbuf, s_s, sem,
                    *, n_pages, page, nbp, past, ts, tn):
    b = pl.program_id(0)
    slot = b % 2
    ppb = n_pages // nbp
    rows = AH * ts

    def page_copies(seq, sl, pg):
        pid = pt_ref[seq * n_pages + pg]
        return (pltpu.make_async_copy(ck_ref.at[pid], kbuf.at[sl, pg], sem.at[sl, 0]),
                pltpu.make_async_copy(cv_ref.at[pid], vbuf.at[sl, pg], sem.at[sl, 1]))

    def fetch(seq, sl):
        for pg in range(n_pages):
            for cp in page_copies(seq, sl, pg):
                cp.start()

    @pl.when(b == 0)
    def _first():
        fetch(b, slot)

    @pl.when(b + 1 < pl.num_programs(0))
    def _next():
        fetch(b + 1, 1 - slot)

    for pg in range(n_pages):
        for cp in page_copies(b, slot, pg):
            cp.wait()

    qsf = qbd_ref[0] * (AD ** -0.5)
    qs = qsf.astype(BF16)
    g = []
    for jb in range(nbp):
        gsum = jnp.zeros((rows, page), F32)
        for p in range(ppb):
            pg = jb * ppb + p
            s = _mm(qs, kbuf[slot, pg].reshape(AW, page))
            s_s[pg] = s
            gsum = gsum + s
        g.append(jnp.sum(gsum, axis=1, keepdims=True))
    rh = lax.broadcasted_iota(I32, (rows, 1), 0) // ts
    rr = lax.broadcasted_iota(I32, (rows, 1), 0) % ts
    slope = jnp.zeros((rows, 1), F32)
    for hh in range(AH):
        slope = jnp.where(rh == hh, 2.0 ** (-8.0 * (hh + 1) / AH), slope)
    picks = []
    for _ in range(min(TOPB, nbp)):
        mx = g[0]
        for jb in range(1, nbp):
            mx = jnp.maximum(mx, g[jb])
        idx = jnp.full((rows, 1), float(nbp), F32)
        for jb in reversed(range(nbp)):
            idx = jnp.where(g[jb] == mx, float(jb), idx)
        picks.append(idx)
        g = [jnp.where(idx == float(jb), MASKV, g[jb]) for jb in range(nbp)]
    lane = lax.broadcasted_iota(I32, (rows, page), 1).astype(F32)
    qpos = (past + rr).astype(F32)
    dist_n = rr - lax.broadcasted_iota(I32, (rows, tn), 1)
    s_new = lax.dot_general(qsf, kn_ref[0], (((1,), (1,)), ((), ())), preferred_element_type=F32)
    s_new = jnp.where(dist_n >= 0, s_new - slope * dist_n.astype(F32), MASKV)
    m = jnp.max(s_new, axis=1, keepdims=True)
    for pg in range(n_pages):
        blk = float(pg // ppb)
        chosen = picks[0] == blk
        for c in picks[1:]:
            chosen = chosen | (c == blk)
        s = jnp.where(chosen, s_s[pg] - slope * (qpos - (lane + float(pg * page))), MASKV)
        s_s[pg] = s
        m = jnp.maximum(m, jnp.max(s, axis=1, keepdims=True))
    p_new = jnp.exp(s_new - m)
    l = jnp.sum(p_new, axis=1, keepdims=True)
    acc = jnp.dot(p_new, vn_ref[0], preferred_element_type=F32)
    for pg in range(n_pages):
        p = jnp.exp(s_s[pg] - m)
        l = l + jnp.sum(p, axis=1, keepdims=True)
        acc = acc + _mm_nt(p, vbuf[slot, pg].reshape(AW, page))
    acc = acc / l
    lh = lax.broadcasted_iota(I32, (ts, AW), 1) // AD
    out = jnp.zeros((ts, AW), F32)
    for hh in range(AH):
        out = out + jnp.where(lh == hh, acc[hh * ts:(hh + 1) * ts, :], 0.0)
    o_ref[0] = out


def _moba_sample_m(page_table, cache_kt, cache_vt, qbd, k_new, v_new, ts):
    db, n_pages = page_table.shape
    page = cache_kt.shape[3]
    past = n_pages * page
    assert past % BLK == 0 and BLK % page == 0
    nbp = past // BLK
    rows = AH * ts
    tn = k_new.shape[1]
    gs = pltpu.PrefetchScalarGridSpec(
        num_scalar_prefetch=1, grid=(db,),
        in_specs=[pl.BlockSpec((1, rows, AW), lambda b, pt: (b, 0, 0)),
                  pl.BlockSpec((1, tn, AW), lambda b, pt: (b, 0, 0)),
                  pl.BlockSpec((1, tn, AW), lambda b, pt: (b, 0, 0)),
                  pl.BlockSpec(memory_space=pl.ANY), pl.BlockSpec(memory_space=pl.ANY)],
        out_specs=pl.BlockSpec((1, ts, AW), lambda b, pt: (b, 0, 0)),
        scratch_shapes=[pltpu.VMEM((2, n_pages, AH, AD, page), F32), pltpu.VMEM((2, n_pages, AH, AD, page), F32),
                        pltpu.VMEM((n_pages, rows, page), F32), pltpu.SemaphoreType.DMA((2, 2))])
    return pl.pallas_call(
        functools.partial(_moba_sm_kernel, n_pages=n_pages, page=page, nbp=nbp, past=past, ts=ts, tn=tn),
        grid_spec=gs, out_shape=jax.ShapeDtypeStruct((db, ts, AW), F32),
        compiler_params=_cparams("arbitrary"), name="moba_s")(
            page_table.reshape(-1), qbd, k_new, v_new, cache_kt, cache_vt)


def _merge_kernel(x_ref, hm_ref, ha_ref, sh1_ref, sc1_ref, gt1_ref, sh2_ref, sc2_ref, gt2_ref,
                  g1_ref, g2_ref, gmh_ref, wl_ref, wbm_ref, wba_ref, wo_ref, wr_ref, br_ref,
                  wsg_ref, wsu_ref, wsd_ref, cin_ref,
                  base_ref, h2_ref, idx_ref, wt_ref, rank_ref, cout_ref, cnt_s):
    i = pl.program_id(0)
    tm = x_ref.shape[0]

    @pl.when(i == 0)
    def _init():
        cnt_s[...] = cin_ref[...]

    x = x_ref[...]
    h1 = (_rms(x, g1_ref[...]) * (1.0 + sc1_ref[0]) + sh1_ref[0]).astype(BF16)
    d = functools.partial(jnp.dot, preferred_element_type=F32)
    mo = d(h1, wl_ref[:, 0:MW])
    ga = d(h1, wl_ref[:, MW:MW + D])
    gb = d(h1, wl_ref[:, MW + D:MW + 2 * D])
    hm = hm_ref[...]
    parts = []
    for h in range(MH):
        xh = hm[:, h * MD:(h + 1) * MD]
        parts.append(_rms(xh, gmh_ref[:, h * MD:(h + 1) * MD]))
    hmn = jnp.concatenate(parts, axis=1) * jax.nn.sigmoid(mo)
    merged = jax.nn.sigmoid(ga) * _mm(hmn, wbm_ref[...]) + jax.nn.sigmoid(gb) * _mm(ha_ref[...], wba_ref[...])
    x1 = x + gt1_ref[0] * _mm(merged, wo_ref[...])
    h2 = _rms(x1, g2_ref[...]) * (1.0 + sc2_ref[0]) + sh2_ref[0]
    _to_row_tiles(h2_ref, h2)
    h2b = h2.astype(BF16)
    shared = _mm(_silu(d(h2b, wsg_ref[...])) * d(h2b, wsu_ref[...]), wsd_ref[...])
    base_ref[...] = x1 + gt2_ref[0] * shared
    scores = jax.nn.sigmoid(_mm_x3(h2, wr_ref[...]))
    col = lax.broadcasted_iota(I32, (tm, NE), 1).astype(F32)
    g = scores + br_ref[...]
    idxs, wts = [], []
    onehot = jnp.zeros((tm, NE), F32)
    for _ in range(TOPK):
        mx = jnp.max(g, axis=1, keepdims=True)
        idx = jnp.min(jnp.where(g == mx, col, float(NE)), axis=1, keepdims=True)
        hit = col == idx
        idxs.append(idx)
        wts.append(jnp.sum(jnp.where(hit, scores, 0.0), axis=1, keepdims=True))
        onehot = jnp.where(hit, 1.0, onehot)
        g = jnp.where(hit, MASKV, g)
    wsum = wts[0]
    for w in wts[1:]:
        wsum = wsum + w
    ri = lax.broadcasted_iota(I32, (tm, tm), 0)
    ci = lax.broadcasted_iota(I32, (tm, tm), 1)
    before = jnp.where(ci < ri, 1.0, 0.0).astype(BF16)
    pref = cnt_s[...] + d(before, onehot.astype(BF16))
    lane8 = lax.broadcasted_iota(I32, (tm, TOPK), 1)
    idx_o = jnp.zeros((tm, TOPK), F32)
    wt_o = jnp.zeros((tm, TOPK), F32)
    rk_o = jnp.zeros((tm, TOPK), F32)
    for kk in range(TOPK):
        rk = jnp.sum(jnp.where(col == idxs[kk], pref, 0.0), axis=1, keepdims=True)
        idx_o = jnp.where(lane8 == kk, idxs[kk], idx_o)
        wt_o = jnp.where(lane8 == kk, wts[kk] / wsum * ROUTE_SCALE, wt_o)
        rk_o = jnp.where(lane8 == kk, rk, rk_o)
    idx_ref[...] = idx_o.astype(I32)
    wt_ref[...] = wt_o
    rank_ref[...] = rk_o.astype(I32)
    cnt_s[...] = cnt_s[...] + jnp.sum(onehot, axis=0, keepdims=True)
    cout_ref[...] = cnt_s[...]


def _merge(x2, hm, ha, mod3, per_token, tm, tiles_per_batch, weights, cnt_in):
    n = x2.shape[0]
    row = lambda w: pl.BlockSpec((tm, w), lambda i: (i, 0))
    full = lambda a: pl.BlockSpec(a.shape, lambda i: (0,) * a.ndim)
    mods = [_mod_spec(per_token, tm, tiles_per_batch, c) for c in range(6)]
    sds = jax.ShapeDtypeStruct
    return pl.pallas_call(
        _merge_kernel, grid=(n // tm,),
        in_specs=[row(D), row(MW), row(AW)] + mods + [full(w) for w in weights] + [full(cnt_in)],
        out_specs=[row(D), pl.BlockSpec((tm * RT, LANES), lambda i: (i, 0)), row(TOPK), row(TOPK), row(TOPK),
                   full(cnt_in)],
        out_shape=[sds((n, D), F32), sds((n * RT, LANES), F32), sds((n, TOPK), I32), sds((n, TOPK), F32),
                   sds((n, TOPK), I32), sds((1, NE), F32)],
        scratch_shapes=[pltpu.VMEM((1, NE), F32)],
        compiler_params=_cparams("arbitrary"), name="merge")(
            x2, hm, ha, *([mod3] * 6), *weights, cnt_in)


def _dest_kernel(idx_ref, rank_ref, pstart_ref, o_ref):
    tm = idx_ref.shape[0]
    col = lax.broadcasted_iota(I32, (tm, NE), 1)
    lane8 = lax.broadcasted_iota(I32, (tm, TOPK), 1)
    idx = idx_ref[...]
    out = rank_ref[...]
    for kk in range(TOPK):
        start = jnp.sum(jnp.where(col == idx[:, kk:kk + 1], pstart_ref[...], 0.0), axis=1, keepdims=True)
        out = out + jnp.where(lane8 == kk, start.astype(I32), 0)
    o_ref[...] = out


def _dest(idx, rank, pstart):
    n = idx.shape[0]
    tm = _pick_tile(n, 512)
    row = pl.BlockSpec((tm, TOPK), lambda i: (i, 0))
    return pl.pallas_call(
        _dest_kernel, grid=(n // tm,),
        in_specs=[row, row, pl.BlockSpec((1, NE), lambda i: (0, 0))],
        out_specs=row, out_shape=jax.ShapeDtypeStruct((n, TOPK), I32),
        compiler_params=_cparams("parallel"), name="dest")(idx, rank, pstart.astype(F32).reshape(1, NE))


def _row_copy(src, dst, sem):
    return pltpu.make_async_copy(src, dst, sem)


def _dispatch_kernel(pend_ref, cnt_ref, dest_ref, h_ref, xs_ref, zbuf, ring, zsem, sem):
    step = pl.program_id(0)
    last = pl.num_programs(0) - 1
    td = h_ref.shape[0] // RT
    slot = step % 2

    @pl.when(step == 0)
    def _zero_tails():
        zbuf[...] = jnp.zeros(zbuf.shape, F32)

        def tail(e):
            first = pl.multiple_of((pend_ref[e] - GROUP) * RT, RT)
            return _row_copy(zbuf, xs_ref.at[pl.ds(first, GROUP * RT)], zsem)

        def start(e, c):
            @pl.when(cnt_ref[e] > 0)
            def _():
                tail(e).start()
            return c

        def wait(e, c):
            @pl.when(cnt_ref[e] > 0)
            def _():
                tail(e).wait()
            return c

        lax.fori_loop(0, NE, start, 0)
        lax.fori_loop(0, NE, wait, 0)

    ring[slot] = h_ref[...]

    def start(t, c):
        src = ring.at[slot, pl.ds(pl.multiple_of(t * RT, RT), RT)]
        for kk in range(TOPK):
            dst = pl.multiple_of(dest_ref[0, 0, t * TOPK + kk] * RT, RT)
            _row_copy(src, xs_ref.at[pl.ds(dst, RT)], sem.at[slot]).start(priority=kk % 2)
        return c

    def drain(s):
        def wait(t, c):
            for kk in range(TOPK):
                _row_copy(ring.at[s, pl.ds(0, RT)], xs_ref.at[pl.ds(0, RT)], sem.at[s]).wait()
            return c
        lax.fori_loop(0, td, wait, 0)

    lax.fori_loop(0, td, start, 0)

    @pl.when(step > 0)
    def _prev():
        drain(1 - slot)

    @pl.when(step == last)
    def _own():
        drain(slot)


def _dispatch(pend, cnt, dest, h2t, n_rows, td):
    n = h2t.shape[0] // RT
    gs = pltpu.PrefetchScalarGridSpec(
        num_scalar_prefetch=2, grid=(n // td,),
        in_specs=[pl.BlockSpec((1, 1, td * TOPK), lambda i, *_: (i, 0, 0), memory_space=pltpu.SMEM),
                  pl.BlockSpec((td * RT, LANES), lambda i, *_: (i, 0))],
        out_specs=pl.BlockSpec(memory_space=pl.ANY),
        scratch_shapes=[pltpu.VMEM((GROUP * RT, LANES), F32), pltpu.VMEM((2, td * RT, LANES), F32),
                        pltpu.SemaphoreType.DMA(()), pltpu.SemaphoreType.DMA((2,))])
    return pl.pallas_call(
        _dispatch_kernel, grid_spec=gs, out_shape=jax.ShapeDtypeStruct((n_rows * RT, LANES), F32),
        compiler_params=_cparams("arbitrary"), name="dispatch")(
            pend, cnt, dest.reshape(n // td, 1, td * TOPK), h2t)


def _experts_kernel(be_ref, nx_ref, sl_ref, nu_ref, x_ref, wg_hbm, wu_hbm, wd_hbm, o_ref,
                    wg_f, wu_f, wd_f, wg_s, wu_s, wd_s, x_s, sem):
    b = pl.program_id(0)

    def fetch(e, slot):
        return (pltpu.make_async_copy(wg_hbm.at[e], wg_f.at[slot], sem.at[slot, 0]),
                pltpu.make_async_copy(wu_hbm.at[e], wu_f.at[slot], sem.at[slot, 1]),
                pltpu.make_async_copy(wd_hbm.at[e], wd_f.at[slot], sem.at[slot, 2]))

    @pl.when(b < nu_ref[0])
    def _run():
        slot = sl_ref[b]

        @pl.when((b == 0) | (be_ref[b] != be_ref[jnp.maximum(b - 1, 0)]))
        def _load():
            @pl.when(b == 0)
            def _first():
                for cp in fetch(be_ref[b], slot):
                    cp.start()

            for cp in fetch(be_ref[b], slot):
                cp.wait()

            @pl.when(nx_ref[b] < NE)
            def _next():
                for cp in fetch(nx_ref[b], 1 - slot):
                    cp.start()

            wg_s[...] = wg_f[slot].astype(BF16)
            wu_s[...] = wu_f[slot].astype(BF16)
            wd_s[...] = wd_f[slot].astype(BF16)

        d = functools.partial(jnp.dot, preferred_element_type=F32)
        for s in range(RT):
            x_s[:, s * LANES:(s + 1) * LANES] = x_ref[pl.ds(s, GROUP, stride=RT), :].astype(BF16)
        x = x_s[...]
        hb = _silu(d(x, wg_s[...])) * d(x, wu_s[...])
        _to_row_tiles(o_ref, d(hb.astype(BF16), wd_s[...]))


def _experts(blk_e, blk_next, blk_slot, n_used, xs, w_gate, w_up, w_down):
    n_rows = xs.shape[0] // RT
    rows = lambda b, be, nx, sl, nu: (jnp.minimum(b, nu[0] - 1), 0)
    hbm = pl.BlockSpec(memory_space=pl.ANY)
    gs = pltpu.PrefetchScalarGridSpec(
        num_scalar_prefetch=4, grid=(n_rows // GROUP,),
        in_specs=[pl.BlockSpec((GROUP * RT, LANES), rows), hbm, hbm, hbm],
        out_specs=pl.BlockSpec((GROUP * RT, LANES), rows),
        scratch_shapes=[pltpu.VMEM((2, D, DE), F32), pltpu.VMEM((2, D, DE), F32), pltpu.VMEM((2, DE, D), F32),
                        pltpu.VMEM((D, DE), BF16), pltpu.VMEM((D, DE), BF16), pltpu.VMEM((DE, D), BF16),
                        pltpu.VMEM((GROUP, D), BF16), pltpu.SemaphoreType.DMA((2, 3))])
    return pl.pallas_call(
        _experts_kernel, grid_spec=gs, out_shape=jax.ShapeDtypeStruct((n_rows * RT, LANES), F32),
        compiler_params=_cparams("arbitrary"), name="experts")(
            blk_e, blk_next, blk_slot, n_used, xs, w_gate, w_up, w_down)


def _combine_kernel(dest_ref, dnext_ref, ys_ref, wt_ref, base_ref, gt2_ref, o_ref, buf, sem):
    step = pl.program_id(0)
    last = pl.num_programs(0) - 1
    tc = base_ref.shape[0]
    slot = step % 2

    def issue(dref, s):
        def start(t, c):
            dst = pl.multiple_of(t * RT, RT)
            for kk in range(TOPK):
                src = pl.multiple_of(dref[0, 0, t * TOPK + kk] * RT, RT)
                _row_copy(ys_ref.at[pl.ds(src, RT)], buf.at[s, kk, pl.ds(dst, RT)], sem.at[s]).start(priority=kk % 2)
            return c
        lax.fori_loop(0, tc, start, 0)

    @pl.when(step == 0)
    def _first():
        issue(dest_ref, slot)

    @pl.when(step < last)
    def _next():
        issue(dnext_ref, 1 - slot)

    def wait(t, c):
        for kk in range(TOPK):
            _row_copy(ys_ref.at[pl.ds(0, RT)], buf.at[slot, kk, pl.ds(0, RT)], sem.at[slot]).wait()
        return c

    lax.fori_loop(0, tc, wait, 0)
    wt = wt_ref[...]
    acc = wt[:, 0:1] * _from_row_tiles(buf, tc, (slot, 0))
    for kk in range(1, TOPK):
        acc = acc + wt[:, kk:kk + 1] * _from_row_tiles(buf, tc, (slot, kk))
    o_ref[...] = base_ref[...] + gt2_ref[0] * acc


def _combine(dest, ys, wt, base, mod3, per_token, tc, tiles_per_batch):
    n = base.shape[0]
    nt = n // tc
    return pl.pallas_call(
        _combine_kernel, grid=(nt,),
        in_specs=[pl.BlockSpec((1, 1, tc * TOPK), lambda i: (i, 0, 0), memory_space=pltpu.SMEM),
                  pl.BlockSpec((1, 1, tc * TOPK), lambda i: (jnp.minimum(i + 1, nt - 1), 0, 0),
                               memory_space=pltpu.SMEM),
                  pl.BlockSpec(memory_space=pl.ANY),
                  pl.BlockSpec((tc, TOPK), lambda i: (i, 0)),
                  pl.BlockSpec((tc, D), lambda i: (i, 0)),
                  _mod_spec(per_token, tc, tiles_per_batch, 5)],
        out_specs=pl.BlockSpec((tc, D), lambda i: (i, 0)),
        out_shape=jax.ShapeDtypeStruct((n, D), F32),
        scratch_shapes=[pltpu.VMEM((2, TOPK, tc * RT, LANES), F32), pltpu.SemaphoreType.DMA((2,))],
        compiler_params=_cparams("arbitrary"), name="combine")(
            dest.reshape(nt, 1, tc * TOPK), dest.reshape(nt, 1, tc * TOPK), ys, wt, base, mod3)


def _pick_tile(n, pref):
    t = pref
    while n % t:
        t //= 2
    return t


def kernel(x_prompt, x_sample, cache_k, cache_v, page_table, state_mlstm_C, state_mlstm_n, state_mlstm_m,
           c_prompt, c_sample, w_ada, b_ada, g_norm1, w_in, b_gates, g_q, g_k, g_mh, w_br_m, w_br_a, w_out,
           g_norm2, w_router, b_router, w_gate, w_up, w_down, ws_gate, ws_up, ws_down):
    depth = w_ada.shape[0]
    assert depth == 1
    bsz, t, _ = x_prompt.shape
    db, ts, _ = x_sample.shape
    n_p, n_s = bsz * t, db * ts
    assert t % BLK == 0 and t % MCHUNK == 0 and ts <= 8
    l = 0

    nc = bsz + db
    ncp = -(-nc // 8) * 8
    c_all = jnp.pad(jnp.concatenate([c_prompt, c_sample], axis=0), ((0, ncp - nc), (0, 0)))
    mod = _ada(c_all, w_ada[l], b_ada[l])
    mod_p = mod[:bsz].reshape(bsz, 1, 6 * D)
    tm_s = _pick_tile(n_s, 256)
    mod_s = jnp.repeat(mod[bsz:nc], ts, axis=0).reshape(n_s // tm_s, tm_s, 6 * D)

    wi = w_in[l]
    sl = lambda name: wi[:, _OFF[name][0]:_OFF[name][1]]
    wm = jnp.concatenate([sl("mq"), sl("mk"), sl("mv"), sl("aq"), sl("ak"), sl("av")], axis=1).astype(BF16)
    wg = jnp.pad(jnp.concatenate([sl("mi"), sl("mf")], axis=1), ((0, 0), (0, 128 - 2 * MH))).astype(BF16)
    wl = jnp.concatenate([sl("mo"), sl("ga"), sl("gb")], axis=1).astype(BF16)
    g1 = g_norm1[l].reshape(1, D)
    g2 = g_norm2[l].reshape(1, D)
    gq = jnp.tile(g_q[l], AH).reshape(1, AW)
    gk = jnp.tile(g_k[l], AH).reshape(1, AW)
    gmh = g_mh[l].reshape(1, MW)
    grp = jnp.arange(AW) // AD
    bd = jnp.where(grp[:, None] == grp[None, :], 1.0 / AD, 0.0).astype(BF16)
    merge_w = (g1, g2, gmh, wl, w_br_m[l].astype(BF16), w_br_a[l].astype(BF16), w_out[l].astype(BF16),
               w_router[l], b_router[l].reshape(1, NE), ws_gate[l].astype(BF16), ws_up[l].astype(BF16),
               ws_down[l].astype(BF16))
    slopes = 2.0 ** (-8.0 * jnp.arange(1, AH + 1, dtype=F32) / AH)

    tm_p = _pick_tile(t, 512)
    xp2 = x_prompt.reshape(n_p, D)
    xs2 = x_sample.reshape(n_s, D)
    mq_p, mk_p, mv_p, gt_p, qa_p, kat_p, vat_p, kb_p, vt_p, km = _inproj(
        xp2, mod_p, False, tm_p, t // tm_p, g1, wm, wg, gq, gk, bd, True)
    mq_s, mk_s, mv_s, gt_s, qa_s, ka_s, va_s = _inproj(
        xs2, mod_s, True, tm_s, 1, g1, wm, wg, gq, gk, bd, False)

    zeros = functools.partial(jnp.zeros, dtype=F32)
    gp = gt_p.reshape(bsz, t, 2 * MH)
    hm_p, c_p, nn_p, m_p = _mlstm(
        mq_p.reshape(bsz, t, MW), mk_p.reshape(bsz, t, MW), mv_p.reshape(bsz, t, MW),
        gp.transpose(0, 2, 1), gp, b_gates[l], zeros((bsz, MH, MD, MD)), zeros((bsz, MH, MD)),
        zeros((bsz, MH, MD)), _pick_tile(bsz, 2), t, MCHUNK)
    ls = 16
    assert ts <= ls
    pad_t = lambda a: jnp.pad(a.reshape(db, ts, -1), ((0, 0), (0, ls - ts), (0, 0)))
    gs_ = pad_t(gt_s)
    m0 = jnp.broadcast_to(state_mlstm_m[l].astype(F32)[:, :, None], (db, MH, MD))
    hm_s, c_sm, nn_s, m_sm = _mlstm(
        pad_t(mq_s), pad_t(mk_s), pad_t(mv_s), gs_.transpose(0, 2, 1), gs_, b_gates[l],
        state_mlstm_C[l].astype(F32), state_mlstm_n[l].astype(F32), m0, _pick_tile(db, 8), ts, ls)
    hm_s = hm_s[:, :ts].reshape(n_s, MW)

    nb = t // BLK
    ha_p = _moba_prompt(qa_p.reshape(bsz, t, AW), kb_p.reshape(bsz, nb, BLK, AW), vt_p.reshape(bsz, nb, AW, BLK),
                        km.reshape(bsz, nb, AW), slopes).reshape(n_p, AW)
    qh = qa_s.astype(F32).reshape(db, ts, AH, AD).transpose(0, 2, 1, 3)
    qbd = (qh[:, :, :, None, :] * jnp.eye(AH, dtype=F32)[None, :, None, :, None]).reshape(db, AH * ts, AW)
    tn = 8
    pad_n = lambda a: jnp.pad(a.reshape(db, ts, AW), ((0, 0), (0, tn - ts), (0, 0)))
    to_t = lambda c: jnp.transpose(c[l], (0, 2, 3, 1))
    ha_s = _moba_sample_m(page_table, to_t(cache_k), to_t(cache_v), qbd, pad_n(ka_s), pad_n(va_s), ts)
    ha_s = ha_s.reshape(n_s, AW)

    tmm_p = _pick_tile(t, 512)
    base_p, h2_p, idx_p, wt_p, rk_p, cnt1 = _merge(
        xp2, hm_p.reshape(n_p, MW), ha_p, mod_p, False, tmm_p, t // tmm_p, merge_w, zeros((1, NE)))
    base_s, h2_s, idx_s, wt_s, rk_s, cnt2 = _merge(
        xs2, hm_s, ha_s, mod_s, True, tm_s, 1, merge_w, cnt1)

    n_all = n_p + n_s
    cnt = cnt2.reshape(NE).astype(I32)
    padded = (cnt + GROUP - 1) // GROUP * GROUP
    pend = jnp.cumsum(padded)
    pstart = pend - padded
    n_blocks = -(-(n_all * TOPK + NE * (GROUP - 1)) // GROUP)
    n_used = (pend[-1] // GROUP).reshape(1)
    blk = jnp.minimum(jnp.arange(n_blocks, dtype=I32), n_used[0] - 1) * GROUP
    blk_e = jnp.minimum(jnp.sum((pend[None, :] <= blk[:, None]).astype(I32), axis=1), NE - 1)
    ids = jnp.arange(NE, dtype=I32)
    later = (ids[None, :] > ids[:, None]) & (cnt[None, :] > 0)
    next_e = jnp.min(jnp.where(later, ids[None, :], NE), axis=1)
    slot_e = (jnp.cumsum((cnt > 0).astype(I32)) - 1) % 2
    blk_next = next_e[blk_e].astype(I32)
    blk_slot = slot_e[blk_e].astype(I32)

    h2 = jnp.concatenate([h2_p, h2_s], axis=0)
    dest_p = _dest(idx_p, rk_p, pstart)
    dest_s = _dest(idx_s, rk_s, pstart)
    td = _pick_tile(n_all, 256)
    xs = _dispatch(pend, cnt, jnp.concatenate([dest_p, dest_s], axis=0), h2, n_blocks * GROUP, td)
    ys = _experts(blk_e, blk_next, blk_slot, n_used, xs, w_gate[l], w_up[l], w_down[l])
    tc_p = _pick_tile(t, 128)
    y_p = _combine(dest_p, ys, wt_p, base_p, mod_p, False, tc_p, t // tc_p)
    tc_s = _pick_tile(tm_s, 64)
    mod_sc = mod_s.reshape(n_s // tc_s, tc_s, 6 * D)
    y_s = _combine(dest_s, ys, wt_s, base_s, mod_sc, True, tc_s, 1)

    st = lambda a: a[None]
    return (y_p.reshape(bsz, t, D), y_s.reshape(db, ts, D),
            st(kat_p.reshape(bsz, AH, AD, t).transpose(0, 3, 1, 2)),
            st(vat_p.reshape(bsz, AH, AD, t).transpose(0, 3, 1, 2)),
            st(c_p), st(nn_p), st(m_p[:, :, 0]),
            st(ka_s.reshape(db, ts, AH, AD)), st(va_s.reshape(db, ts, AH, AD)),
            st(c_sm), st(nn_s), st(m_sm[:, :, 0]))
```
